```python
import jax, jax.numpy as jnp
from jax import lax
import numpy as np

D_MODEL = 1024
BATCH = 8
SEQ = 4096
DEPTH = 2

D_MIX = D_MODEL
FOX_HEADS = 6
FOX_HEAD_DIM = 64
FOX_WIDTH = FOX_HEADS * FOX_HEAD_DIM
Q_BLOCK = 128
FOX_F_BIAS_INIT = 2.0
S5_CH_PER_GROUP = 16
S5_GROUPS = 16
S5_WIDTH = S5_GROUPS * S5_CH_PER_GROUP
S5_STATE = 64
S5_DT_MIN = 0.001
S5_DT_MAX = 0.1
HGRN_HEADS = 6
HGRN_KEY_DIM = 64
HGRN_VAL_DIM = 64
HGRN_WIDTH = HGRN_HEADS * HGRN_VAL_DIM
HGRN_CHUNK = 64
N_EXPERTS = 16
N_EXPERT_GROUPS = 4
EXPERTS_PER_GROUP = N_EXPERTS // N_EXPERT_GROUPS
TOP_K = 2
D_FF_EXPERT = 256
EPS = 1e-6

IN_SPLITS = (FOX_WIDTH, FOX_WIDTH, FOX_WIDTH, FOX_HEADS, S5_WIDTH, HGRN_WIDTH, HGRN_WIDTH, HGRN_WIDTH, HGRN_WIDTH)
N_IN = sum(IN_SPLITS)
IN_OFFSETS = tuple(int(v) for v in np.cumsum(IN_SPLITS)[:-1])

kernel_name = "hybrid_fox_s5_hgrn2_moe_adaln"

F32 = jnp.float32


def rms_norm(x, w):
    x32 = x.astype(F32)
    y = x32 * lax.rsqrt(jnp.mean(x32 * x32, axis=-1, keepdims=True) + EPS)
    return (y * w.astype(F32)).astype(x.dtype)


def fox_attention(q, k, v, log_f):
    bsz, seq = q.shape[0], q.shape[1]
    q = jnp.transpose(q, (0, 2, 1, 3)).astype(F32)
    k = jnp.transpose(k, (0, 2, 1, 3)).astype(F32)
    v = jnp.transpose(v, (0, 2, 1, 3)).astype(F32)
    cum = jnp.transpose(jnp.cumsum(log_f, axis=1), (0, 2, 1))
    scale = FOX_HEAD_DIM ** -0.5
    outs = []
    for blk in range(seq // Q_BLOCK):
        lo, hi = blk * Q_BLOCK, (blk + 1) * Q_BLOCK
        logits = jnp.einsum('bhqd,bhkd->bhqk', q[:, :, lo:hi], k[:, :, :hi]) * scale
        logits = logits + cum[:, :, lo:hi, None] - cum[:, :, None, :hi]
        causal = (lo + jnp.arange(Q_BLOCK))[:, None] >= jnp.arange(hi)[None, :]
        probs = jax.nn.softmax(jnp.where(causal, logits, -jnp.inf), axis=-1)
        outs.append(jnp.einsum('bhqk,bhkd->bhqd', probs, v[:, :, :hi]))
    o = jnp.concatenate(outs, axis=2)
    return jnp.transpose(o, (0, 2, 1, 3)).reshape(bsz, seq, FOX_WIDTH)


def _diag_linear_combine(e1, e2):
    a1, b1 = e1
    a2, b2 = e2
    return a1 * a2, a2 * b1 + b2


def s5_mixer(u, a_re, a_im, b_re, b_im, c_re, c_im, d, log_dt, glu_w, glu_b):
    bsz, seq = u.shape[0], u.shape[1]
    u32 = u.astype(F32).reshape(bsz, seq, S5_GROUPS, S5_CH_PER_GROUP)
    lam = lax.complex(a_re.astype(F32), a_im.astype(F32))
    dt = jnp.exp(log_dt.astype(F32))[:, None]
    a_bar = jnp.exp(lam * dt)
    b_bar = ((a_bar - 1.0) / lam)[..., None] * lax.complex(b_re.astype(F32), b_im.astype(F32))
    bu = jnp.einsum('btgc,gpc->btgp', u32.astype(jnp.complex64), b_bar)
    a_seq = jnp.broadcast_to(a_bar, bu.shape)
    _, states = lax.associative_scan(_diag_linear_combine, (a_seq, bu), axis=1)
    c_mat = lax.complex(c_re.astype(F32), c_im.astype(F32))
    y = jnp.real(jnp.einsum('btgp,gcp->btgc', states, c_mat)) + d.astype(F32) * u32
    y = jax.nn.gelu(y.reshape(bsz, seq, S5_WIDTH))
    gate = jax.nn.sigmoid(y @ glu_w.astype(F32) + glu_b.astype(F32))
    return (y * gate).astype(u.dtype)


def _hgrn2_chunk(state, chunk):
    q, k, v, log_f = chunk
    n = q.shape[2]
    b = jnp.cumsum(log_f, axis=2)
    causal = jnp.tril(jnp.ones((n, n), dtype=bool))
    diff = b[:, :, :, None, :] - b[:, :, None, :, :]
    decay = jnp.exp(jnp.where(causal[:, :, None], diff, -jnp.inf))
    scores = jnp.einsum('bhtd,bhsd,bhtsd->bhts', q, k, decay)
    o = jnp.einsum('bhts,bhsv->bhtv', scores, v) + jnp.einsum('bhtd,bhdv->bhtv', q * jnp.exp(b), state)
    b_last = b[:, :, -1:, :]
    new_state = jnp.exp(b_last[:, :, 0, :])[..., None] * state + jnp.einsum('bhsd,bhsv->bhdv', k * jnp.exp(b_last - b), v)
    return new_state, o


def hgrn2_mixer(q, f_logit, inp, g, lower_bound, norm_w):
    bsz, seq = q.shape[0], q.shape[1]
    n_chunks = seq // HGRN_CHUNK
    z = f_logit.astype(F32)
    lb = lower_bound.astype(F32)
    log_f = jnp.logaddexp(jnp.log(lb), jnp.log1p(-lb) + jax.nn.log_sigmoid(z))
    key = (1.0 - lb) * jax.nn.sigmoid(-z)
    qf = jax.nn.silu(q.astype(F32))

    def to_chunks(t, dim):
        t = t.reshape(bsz, n_chunks, HGRN_CHUNK, HGRN_HEADS, dim)
        return jnp.transpose(t, (1, 0, 3, 2, 4))

    xs = (to_chunks(qf, HGRN_KEY_DIM), to_chunks(key, HGRN_KEY_DIM),
          to_chunks(inp.astype(F32), HGRN_VAL_DIM), to_chunks(log_f, HGRN_KEY_DIM))
    s0 = jnp.zeros((bsz, HGRN_HEADS, HGRN_KEY_DIM, HGRN_VAL_DIM), F32)
    _, o = lax.scan(_hgrn2_chunk, s0, xs)
    o = jnp.transpose(o, (1, 0, 3, 2, 4)).reshape(bsz, seq, HGRN_HEADS, HGRN_VAL_DIM)
    o = rms_norm(o, norm_w).reshape(bsz, seq, HGRN_WIDTH) * jax.nn.silu(g.astype(F32))
    return o.astype(q.dtype)


def moe_ffn(h, router_w, router_b, w_gate, w_up, w_down):
    bsz, seq, d = h.shape
    t = h.reshape(-1, d)
    logits = (t @ router_w + router_b).astype(F32)
    probs = jax.nn.softmax(logits, axis=-1)
    grouped = probs.reshape(-1, N_EXPERT_GROUPS, EXPERTS_PER_GROUP)
    group_score = lax.top_k(grouped, TOP_K)[0].sum(-1)
    best = jnp.argmax(group_score, axis=-1)
    in_group = jnp.take_along_axis(grouped, best[:, None, None], axis=1)[:, 0]
    top_w, top_i = lax.top_k(in_group, TOP_K)
    top_w = top_w / jnp.sum(top_w, axis=-1, keepdims=True)
    idx = best[:, None] * EXPERTS_PER_GROUP + top_i
    combine = jnp.sum(jax.nn.one_hot(idx, N_EXPERTS, dtype=F32) * top_w[..., None], axis=1)
    y = jnp.zeros(t.shape, F32)
    for e in range(N_EXPERTS):
        hid = jax.nn.silu(t @ w_gate[e]) * (t @ w_up[e])
        y = y + combine[:, e:e + 1] * (hid @ w_down[e]).astype(F32)
    return y.reshape(bsz, seq, d).astype(h.dtype)


def setup_inputs(seed: int = 0) -> dict:
    key = jax.random.key(seed)
    ks = jax.random.split(key, 32)

    def nrm(k, shape, scale):
        return scale * jax.random.normal(k, shape, F32)

    n_idx = jnp.arange(S5_STATE, dtype=F32)
    return {
        "x": nrm(ks[0], (BATCH, SEQ, D_MODEL), 1.0),
        "c": nrm(ks[1], (BATCH, D_MODEL), 1.0),
        "ada_w": nrm(ks[2], (DEPTH, D_MODEL, 6 * D_MODEL), 0.3 * D_MODEL ** -0.5),
        "ada_b": nrm(ks[3], (DEPTH, 6 * D_MODEL), 0.01),
        "norm_mix_w": 1.0 + nrm(ks[4], (DEPTH, D_MODEL), 0.02),
        "norm_ffn_w": 1.0 + nrm(ks[5], (DEPTH, D_MODEL), 0.02),
        "w_in": nrm(ks[6], (DEPTH, D_MODEL, N_IN), D_MODEL ** -0.5),
        "fox_f_bias": FOX_F_BIAS_INIT + nrm(ks[7], (DEPTH, FOX_HEADS), 0.1),
        "s5_a_re": -0.5 + nrm(ks[8], (DEPTH, S5_GROUPS, S5_STATE), 0.01),
        "s5_a_im": jnp.broadcast_to(jnp.pi * n_idx, (DEPTH, S5_GROUPS, S5_STATE)) + nrm(ks[9], (DEPTH, S5_GROUPS, S5_STATE), 0.01),
        "s5_b_re": nrm(ks[10], (DEPTH, S5_GROUPS, S5_STATE, S5_CH_PER_GROUP), (2 * S5_CH_PER_GROUP) ** -0.5),
        "s5_b_im": nrm(ks[11], (DEPTH, S5_GROUPS, S5_STATE, S5_CH_PER_GROUP), (2 * S5_CH_PER_GROUP) ** -0.5),
        "s5_c_re": nrm(ks[12], (DEPTH, S5_GROUPS, S5_CH_PER_GROUP, S5_STATE), 0.5),
        "s5_c_im": nrm(ks[13], (DEPTH, S5_GROUPS, S5_CH_PER_GROUP, S5_STATE), 0.5),
        "s5_d": nrm(ks[14], (DEPTH, S5_GROUPS, S5_CH_PER_GROUP), 1.0),
        "s5_log_dt": jax.random.uniform(ks[15], (DEPTH, S5_GROUPS), F32, float(np.log(S5_DT_MIN)), float(np.log(S5_DT_MAX))),
        "s5_glu_w": nrm(ks[16], (DEPTH, S5_WIDTH, S5_WIDTH), S5_WIDTH ** -0.5),
        "s5_glu_b": nrm(ks[17], (DEPTH, S5_WIDTH), 0.01),
        "hgrn_lb_logits": nrm(ks[18], (DEPTH, HGRN_WIDTH), 0.1),
        "hgrn_norm_w": 1.0 + nrm(ks[19], (DEPTH, HGRN_VAL_DIM), 0.02),
        "branch_norm_fox": 1.0 + nrm(ks[20], (DEPTH, FOX_WIDTH), 0.02),
        "branch_norm_s5": 1.0 + nrm(ks[21], (DEPTH, S5_WIDTH), 0.02),
        "w_out": nrm(ks[22], (DEPTH, D_MIX, D_MODEL), D_MIX ** -0.5),
        "router_w": nrm(ks[23], (D_MODEL, N_EXPERTS), D_MODEL ** -0.5),
        "router_b": nrm(ks[24], (N_EXPERTS,), 0.01),
        "moe_w_gate": nrm(ks[25], (DEPTH, N_EXPERTS, D_MODEL, D_FF_EXPERT), D_MODEL ** -0.5),
        "moe_w_up": nrm(ks[26], (DEPTH, N_EXPERTS, D_MODEL, D_FF_EXPERT), D_MODEL ** -0.5),
        "moe_w_down": nrm(ks[27], (DEPTH, N_EXPERTS, D_FF_EXPERT, D_MODEL), D_FF_EXPERT ** -0.5),
        "final_norm_w": 1.0 + nrm(ks[28], (D_MODEL,), 0.02),
    }


def reference(x, c, ada_w, ada_b, norm_mix_w, norm_ffn_w, w_in, fox_f_bias,
              s5_a_re, s5_a_im, s5_b_re, s5_b_im, s5_c_re, s5_c_im, s5_d, s5_log_dt,
              s5_glu_w, s5_glu_b, hgrn_lb_logits, hgrn_norm_w, branch_norm_fox, branch_norm_s5,
              w_out, router_w, router_b, moe_w_gate, moe_w_up, moe_w_down, final_norm_w):
    bsz, seq, _ = x.shape
    lb_cum = jnp.cumsum(jax.nn.softmax(hgrn_lb_logits.astype(F32), axis=0), axis=0)
    lower_bounds = lb_cum - lb_cum[0:1]
    c_act = jax.nn.silu(c)
    for layer in range(DEPTH):
        mod = c_act @ ada_w[layer] + ada_b[layer]
        sh_m, sc_m, g_m, sh_f, sc_f, g_f = [m[:, None, :] for m in jnp.split(mod, 6, axis=-1)]
        h = rms_norm(x, norm_mix_w[layer]) * (1 + sc_m) + sh_m
        z = h @ w_in[layer]
        fq, fk, fv, ff, su, hq, hf, hi, hg = jnp.split(z, IN_OFFSETS, axis=-1)
        heads4 = (bsz, seq, FOX_HEADS, FOX_HEAD_DIM)
        fox_log_f = jax.nn.log_sigmoid(ff.astype(F32) + fox_f_bias[layer].astype(F32))
        o_fox = fox_attention(fq.reshape(heads4), fk.reshape(heads4), fv.reshape(heads4), fox_log_f).astype(x.dtype)
        o_fox = rms_norm(o_fox, branch_norm_fox[layer])
        o_s5 = s5_mixer(su, s5_a_re[layer], s5_a_im[layer], s5_b_re[layer], s5_b_im[layer],
                        s5_c_re[layer], s5_c_im[layer], s5_d[layer], s5_log_dt[layer],
                        s5_glu_w[layer], s5_glu_b[layer]).astype(x.dtype)
        o_s5 = rms_norm(o_s5, branch_norm_s5[layer])
        o_hgrn = hgrn2_mixer(hq, hf, hi, hg, lower_bounds[layer], hgrn_norm_w[layer]).astype(x.dtype)
        mix = jnp.concatenate([o_fox, o_s5, o_hgrn], axis=-1) @ w_out[layer]
        x = x + g_m * mix
        h = rms_norm(x, norm_ffn_w[layer]) * (1 + sc_f) + sh_f
        x = x + g_f * moe_ffn(h, router_w, router_b, moe_w_gate[layer], moe_w_up[layer], moe_w_down[layer])
    return rms_norm(x, final_norm_w)
```

```python
import functools
import math

import jax
import jax.numpy as jnp
from jax import lax
from jax.experimental import pallas as pl
from jax.experimental.pallas import tpu as pltpu

F32 = jnp.float32
BF16 = jnp.bfloat16
HIGHEST = lax.Precision.HIGHEST

D_MODEL = 1024
FOX_HEADS = 6
HEAD_DIM = 64
FOX_WIDTH = 384
S5_WIDTH = 256
S5_GROUPS = 16
S5_CH = 16
S5_STATE = 64
S5_NSTATE = S5_GROUPS * S5_STATE
HGRN_WIDTH = 384
N_EXPERTS = 16
N_GROUPS = 4
D_FF = 256
EPS = 1e-6
LANES = 128
NEG_BIG = -1e30

C_QKV = (0, 1152)
C_SU = (1152, 1408)
C_HQIG = (1408, 2560)
C_HF = (2560, 2944)
C_FF = (2944, 3072)
N_IN_PAD = 3072

NT_DIMS = (((1,), (1,)), ((), ()))

VMEM_LIMIT = 56 * 1024 * 1024


def _cparams(sem):
    return pltpu.CompilerParams(dimension_semantics=sem, vmem_limit_bytes=VMEM_LIMIT)


def _sigmoid(x):
    return 1.0 / (1.0 + jnp.exp(-x))


def _silu(x):
    return x * _sigmoid(x)


def _log_sigmoid(x):
    return jnp.minimum(x, 0.0) - jnp.log1p(jnp.exp(-jnp.abs(x)))


def _rms(x):
    return x * lax.rsqrt(jnp.mean(x * x, axis=-1, keepdims=True) + EPS)


def _ada_kernel(c_ref, w_ref, b_ref, o_ref):
    c = c_ref[...]
    o_ref[...] = jnp.dot(_silu(c), w_ref[...], precision=HIGHEST,
                         preferred_element_type=F32) + b_ref[...]


def _ada_mod(c, ada_w, ada_b):
    depth, d, n6 = ada_w.shape
    bsz = c.shape[0]
    nblk = n6 // d
    out = pl.pallas_call(
        _ada_kernel,
        grid=(depth, nblk),
        in_specs=[
            pl.BlockSpec((bsz, d), lambda l, j: (0, 0)),
            pl.BlockSpec((None, d, d), lambda l, j: (l, 0, j)),
            pl.BlockSpec((None, 1, d), lambda l, j: (l, 0, j)),
        ],
        out_specs=pl.BlockSpec((None, bsz, d), lambda l, j: (l, 0, j)),
        out_shape=jax.ShapeDtypeStruct((depth, bsz, n6), F32),
        compiler_params=_cparams(("arbitrary", "arbitrary")),
        name="ada_mod",
    )(c, ada_w, ada_b.reshape(depth, 1, n6))
    return out.reshape(depth, bsz, nblk, d)


def _inproj_kernel(x_ref, mod_ref, nw_ref, w_ref, fb_ref,
                   qkv_ref, su_ref, hqig_ref, hf_ref, cum_ref, carry_ref):
    i = pl.program_id(1)
    tm = x_ref.shape[0]

    @pl.when(i == 0)
    def _():
        carry_ref[...] = jnp.zeros_like(carry_ref)

    h = _rms(x_ref[...]) * nw_ref[...]
    h = h * (1.0 + mod_ref[1:2, :]) + mod_ref[0:1, :]
    hb = h.astype(BF16)

    def proj(c):
        return jnp.dot(hb, w_ref[:, c[0]:c[1]], preferred_element_type=F32)

    qkv_ref[...] = proj(C_QKV).astype(BF16)
    su = proj(C_SU)
    su_ref[0] = su[:, :LANES]
    su_ref[1] = su[:, LANES:]
    hqig_ref[...] = proj(C_HQIG).astype(BF16)
    hf_ref[...] = proj(C_HF)

    lf = _log_sigmoid(proj(C_FF) + fb_ref[...])
    lf_t = lf.T[0:8, :]
    r = lax.broadcasted_iota(jnp.int32, (tm, tm), 0)
    c = lax.broadcasted_iota(jnp.int32, (tm, tm), 1)
    tri_u = jnp.where(r <= c, 1.0, 0.0).astype(F32)
    cum = jnp.dot(lf_t, tri_u, precision=HIGHEST, preferred_element_type=F32) + carry_ref[:, 0:1]
    cum_ref[...] = cum
    carry_ref[...] = jnp.broadcast_to(cum[:, tm - 1:tm], carry_ref.shape)


def _in_proj(x, mod_l, norm_w, w_pad, fb_pad, tm):
    bsz, seq, d = x.shape
    tm = min(tm, seq)
    return pl.pallas_call(
        _inproj_kernel,
        grid=(bsz, seq // tm),
        in_specs=[
            pl.BlockSpec((None, tm, d), lambda b, i: (b, i, 0)),
            pl.BlockSpec((None, 6, d), lambda b, i: (b, 0, 0)),
            pl.BlockSpec((1, d), lambda b, i: (0, 0)),
            pl.BlockSpec((d, N_IN_PAD), lambda b, i: (0, 0)),
            pl.BlockSpec((1, LANES), lambda b, i: (0, 0)),
        ],
        out_specs=[
            pl.BlockSpec((None, tm, 1152), lambda b, i: (b, i, 0)),
            pl.BlockSpec((2, None, tm, LANES), lambda b, i: (0, b, i, 0)),
            pl.BlockSpec((None, tm, 1152), lambda b, i: (b, i, 0)),
            pl.BlockSpec((None, tm, HGRN_WIDTH), lambda b, i: (b, i, 0)),
            pl.BlockSpec((None, 8, tm), lambda b, i: (b, 0, i)),
        ],
        out_shape=[
            jax.ShapeDtypeStruct((bsz, seq, 1152), BF16),
            jax.ShapeDtypeStruct((2, bsz, seq, LANES), F32),
            jax.ShapeDtypeStruct((bsz, seq, 1152), BF16),
            jax.ShapeDtypeStruct((bsz, seq, HGRN_WIDTH), F32),
            jax.ShapeDtypeStruct((bsz, 8, seq), F32),
        ],
        scratch_shapes=[pltpu.VMEM((8, LANES), F32)],
        compiler_params=_cparams(("arbitrary", "arbitrary")),
        name="in_proj",
    )(x, mod_l, norm_w, w_pad, fb_pad)


def _fox_kernel(q_ref, k_ref, v_ref, cum_ref, o_ref, m_sc, l_sc, acc_sc, *, tq, tk):
    p = pl.program_id(1)
    qi = pl.program_id(2)
    q = q_ref[...]
    lane = lax.broadcasted_iota(jnp.int32, (tq, LANES), 1)
    first = lane < HEAD_DIM
    zero = jnp.zeros_like(q)
    q2 = jnp.concatenate([jnp.where(first, q, zero), jnp.where(first, zero, q)], axis=0)

    m_sc[...] = jnp.full(m_sc.shape, NEG_BIG, F32)
    l_sc[...] = jnp.zeros(l_sc.shape, F32)
    acc_sc[...] = jnp.zeros(acc_sc.shape, F32)

    def step(k0, masked):
        k = k_ref[pl.ds(k0, tk), :]
        v = v_ref[pl.ds(k0, tk), :]
        s = lax.dot_general(q2, k, NT_DIMS, preferred_element_type=F32)
        bias_a = cum_ref[pl.ds(2 * p, 1), pl.ds(k0, tk)]
        bias_b = cum_ref[pl.ds(2 * p + 1, 1), pl.ds(k0, tk)]
        s_a = s[:tq] - bias_a
        s_b = s[tq:] - bias_b
        if masked:
            rows = lax.broadcasted_iota(jnp.int32, (tq, tk), 0) + qi * tq
            cols = lax.broadcasted_iota(jnp.int32, (tq, tk), 1) + k0
            keep = cols <= rows
            s_a = jnp.where(keep, s_a, NEG_BIG)
            s_b = jnp.where(keep, s_b, NEG_BIG)
        s = jnp.concatenate([s_a, s_b], axis=0)
        m_prev = m_sc[...]
        m_cur = jnp.max(s, axis=1, keepdims=True)
        m_next = jnp.maximum(m_prev, m_cur)
        pr = jnp.exp(s - jnp.tile(m_next, (1, tk // LANES)))
        alpha = jnp.exp(m_prev - m_next)
        l_sc[...] = alpha * l_sc[...] + jnp.sum(pr, axis=1, keepdims=True)
        m_sc[...] = m_next
        acc_sc[...] = acc_sc[...] * alpha + jnp.dot(pr.astype(BF16), v, preferred_element_type=F32)

    n_full = (qi * tq) // tk

    def body(j, carry):
        step(pl.multiple_of(j * tk, tk), False)
        return carry

    lax.fori_loop(0, n_full, body, 0)
    for jj in range(max(tq // tk, 1)):
        step(pl.multiple_of(n_full * tk + jj * tk, tk), True)

    o = acc_sc[...] / l_sc[...]
    o_ref[...] = jnp.where(first, o[:tq], o[tq:])


def _fox_attention(qkv, cum_t, tq, tk):
    bsz, seq, _ = qkv.shape
    tq = min(tq, seq)
    tk = min(tk, tq)
    npair = FOX_HEADS // 2
    kern = functools.partial(_fox_kernel, tq=tq, tk=tk)
    return pl.pallas_call(
        kern,
        grid=(bsz, npair, seq // tq),
        in_specs=[
            pl.BlockSpec((None, tq, LANES), lambda b, p, i: (b, i, p)),
            pl.BlockSpec((None, seq, LANES), lambda b, p, i: (b, 0, npair + p)),
            pl.BlockSpec((None, seq, LANES), lambda b, p, i: (b, 0, 2 * npair + p)),
            pl.BlockSpec((None, 8, seq), lambda b, p, i: (b, 0, 0)),
        ],
        out_specs=pl.BlockSpec((None, tq, LANES), lambda b, p, i: (b, i, p)),
        out_shape=jax.ShapeDtypeStruct((bsz, seq, FOX_WIDTH), F32),
        scratch_shapes=[
            pltpu.VMEM((2 * tq, LANES), F32),
            pltpu.VMEM((2 * tq, LANES), F32),
            pltpu.VMEM((2 * tq, LANES), F32),
        ],
        compiler_params=_cparams(("arbitrary", "arbitrary", "arbitrary")),
        name="fox_attn",
    )(qkv, qkv, qkv, cum_t)


def _gelu_tanh(x):
    return 0.5 * x * (1.0 + jnp.tanh(math.sqrt(2.0 / math.pi) * (x + 0.044715 * (x * x * x))))


def _s5_kernel(su_ref, bm_ref, ar_ref, ai_ref, cm_ref, d_ref, gw_ref, gb_ref, o_ref,
               utb, bu, ytb, st):
    i = pl.program_id(0)
    nb = su_ref.shape[1]
    ln = su_ref.shape[2]
    ns = S5_NSTATE

    @pl.when(i == 0)
    def _():
        st[...] = jnp.zeros_like(st)

    for k in range(2):
        for b in range(nb):
            utb[k, pl.ds(b, ln, stride=nb), :] = su_ref[k, b]
    u = jnp.concatenate([utb[0], utb[1]], axis=1)
    bu[...] = jnp.dot(u.astype(BF16), bm_ref[...], preferred_element_type=F32)

    ar = ar_ref[...]
    ai = ai_ref[...]

    def body(t, carry):
        re, im = carry
        r0 = pl.multiple_of(t * nb, nb)
        bur = bu[pl.ds(r0, nb), 0:ns]
        bui = bu[pl.ds(r0, nb), ns:2 * ns]
        nre = ar * re - ai * im + bur
        nim = ar * im + ai * re + bui
        bu[pl.ds(r0, nb), 0:ns] = nre
        bu[pl.ds(r0, nb), ns:2 * ns] = nim
        return nre, nim

    re, im = lax.fori_loop(0, ln, body, (st[:, 0:ns], st[:, ns:2 * ns]))
    st[:, 0:ns] = re
    st[:, ns:2 * ns] = im

    y = jnp.dot(bu[...].astype(BF16), cm_ref[...], preferred_element_type=F32) + d_ref[...] * u
    y = _gelu_tanh(y)
    gate = _sigmoid(jnp.dot(y.astype(BF16), gw_ref[...], preferred_element_type=F32) + gb_ref[...])
    out = y * gate
    ytb[0] = out[:, :LANES]
    ytb[1] = out[:, LANES:]
    for k in range(2):
        for b in range(nb):
            o_ref[b, :, k * LANES:(k + 1) * LANES] = ytb[k, pl.ds(b, ln, stride=nb), :]


def _s5_mixer(su, bmat, ar, ai, cmat, dvec, glu_w, glu_b, ln):
    _, bsz, seq, _ = su.shape
    ln = min(ln, seq)
    rows = bsz * ln
    const = lambda shape: pl.BlockSpec(shape, lambda i: (0,) * len(shape))
    return pl.pallas_call(
        _s5_kernel,
        grid=(seq // ln,),
        in_specs=[
            pl.BlockSpec((2, bsz, ln, LANES), lambda i: (0, 0, i, 0)),
            const((S5_WIDTH, 2 * S5_NSTATE)),
            const((bsz, S5_NSTATE)),
            const((bsz, S5_NSTATE)),
            const((2 * S5_NSTATE, S5_WIDTH)),
            const((1, S5_WIDTH)),
            const((S5_WIDTH, S5_WIDTH)),
            const((1, S5_WIDTH)),
        ],
        out_specs=pl.BlockSpec((bsz, ln, S5_WIDTH), lambda i: (0, i, 0)),
        out_shape=jax.ShapeDtypeStruct((bsz, seq, S5_WIDTH), F32),
        scratch_shapes=[
            pltpu.VMEM((2, rows, LANES), F32),
            pltpu.VMEM((rows, 2 * S5_NSTATE), F32),
            pltpu.VMEM((2, rows, LANES), F32),
            pltpu.VMEM((bsz, 2 * S5_NSTATE), F32),
        ],
        compiler_params=_cparams(("arbitrary",)),
        name="s5_mixer",
    )(su, bmat, ar, ai, cmat, dvec, glu_w, glu_b)


def _block_ref(b, m):
    t = b.shape[0]
    if 2 * m >= 8:
        b3 = b.reshape(t // (2 * m), 2 * m, LANES)
        ref = jnp.broadcast_to(b3[:, m - 1:m, :], b3.shape)
        return ref.reshape(t, LANES)
    row = lax.broadcasted_iota(jnp.int32, b.shape, 0)
    res = b
    for r in range(2 * m):
        s = r - (m - 1)
        if s == 0:
            continue
        res = jnp.where(row % (2 * m) == r, pltpu.roll(b, s % t, axis=0), res)
    return res


def _hgrn_kernel(hq_ref, hi_ref, hg_ref, hf_ref, llb_ref, l1lb_ref, omlb_ref, nw_ref, o_ref, *, tc):
    seq = hf_ref.shape[0]
    lane = lax.broadcasted_iota(jnp.int32, (tc, LANES), 1)
    first = lane < HEAD_DIM
    row = lax.broadcasted_iota(jnp.int32, (tc, LANES), 0)
    rr = lax.broadcasted_iota(jnp.int32, (tc, tc), 0)
    cc = lax.broadcasted_iota(jnp.int32, (tc, tc), 1)
    tri_l = jnp.where(cc <= rr, 1.0, 0.0).astype(F32)
    sr = lax.broadcasted_iota(jnp.int32, (LANES, LANES), 0)
    sc = lax.broadcasted_iota(jnp.int32, (LANES, LANES), 1)
    blockdiag = (sr < HEAD_DIM) == (sc < HEAD_DIM)
    llb = llb_ref[...]
    l1lb = l1lb_ref[...]
    omlb = omlb_ref[...]
    nw = nw_ref[...]
    levels = []
    m = 1
    while 2 * m <= tc:
        levels.append(m)
        m *= 2

    def stack(a):
        z = jnp.zeros_like(a)
        return jnp.concatenate([jnp.where(first, a, z), jnp.where(first, z, a)], axis=0).astype(BF16)

    def chunk(c, st):
        r0 = pl.multiple_of(c * tc, tc)
        z = hf_ref[pl.ds(r0, tc), :]
        q = _silu(hq_ref[pl.ds(r0, tc), :].astype(F32))
        v = hi_ref[pl.ds(r0, tc), :]
        g = hg_ref[pl.ds(r0, tc), :].astype(F32)
        lsz = l1lb + _log_sigmoid(z)
        log_f = jnp.maximum(llb, lsz) + jnp.log1p(jnp.exp(-jnp.abs(llb - lsz)))
        kk = omlb * _sigmoid(-z)
        b = jnp.dot(tri_l, log_f, precision=HIGHEST, preferred_element_type=F32)

        st_b = st.astype(BF16)
        o = lax.dot_general((q * jnp.exp(b)).astype(BF16), st_b, NT_DIMS, preferred_element_type=F32)

        tot = jnp.where(jnp.concatenate([rr == cc, rr == cc], axis=0),
                        lax.dot_general(stack(q), kk.astype(BF16), NT_DIMS, preferred_element_type=F32), 0.0)
        for m in levels:
            dm = b - _block_ref(b, m)
            upper = (row // m) % 2 == 1
            qm = q * jnp.exp(jnp.where(upper, dm, NEG_BIG))
            km = kk * jnp.exp(jnp.where(upper, NEG_BIG, -dm))
            s = lax.dot_general(stack(qm), km.astype(BF16), NT_DIMS, preferred_element_type=F32)
            if 2 * m < tc:
                same = (rr // (2 * m)) == (cc // (2 * m))
                s = jnp.where(jnp.concatenate([same, same], axis=0), s, 0.0)
            tot = tot + s
        pv = jnp.dot(tot.astype(BF16), v, preferred_element_type=F32)
        o = o + jnp.where(first, pv[:tc], pv[tc:])

        b_last = b[tc - 1:tc, :]
        ke = kk * jnp.exp(b_last - b)
        upd = jnp.dot(v.astype(F32).T.astype(BF16), ke.astype(BF16), preferred_element_type=F32)
        st_new = st * jnp.exp(b_last) + jnp.where(blockdiag, upd, 0.0)

        o2 = o * o
        s_a = jnp.sum(jnp.where(first, o2, 0.0), axis=-1, keepdims=True)
        s_b = jnp.sum(jnp.where(first, 0.0, o2), axis=-1, keepdims=True)
        ms = jnp.where(first, s_a, s_b) * (1.0 / HEAD_DIM)
        o_ref[pl.ds(r0, tc), :] = o * lax.rsqrt(ms + EPS) * nw * _silu(g)
        return st_new

    lax.fori_loop(0, seq // tc, chunk, jnp.zeros((LANES, LANES), F32))


def _hgrn_mixer(hqig, hf, log_lb, log1m_lb, om_lb, nw2, tc):
    bsz, seq, _ = hf.shape
    tc = min(tc, seq)
    npair = HGRN_WIDTH // LANES
    kern = functools.partial(_hgrn_kernel, tc=tc)
    col = lambda off: pl.BlockSpec((None, seq, LANES), lambda b, p: (b, 0, off + p))
    par = pl.BlockSpec((1, LANES), lambda b, p: (0, p))
    return pl.pallas_call(
        kern,
        grid=(bsz, npair),
        in_specs=[col(0), col(npair), col(2 * npair), col(0), par, par, par,
                  pl.BlockSpec((1, LANES), lambda b, p: (0, 0))],
        out_specs=col(0),
        out_shape=jax.ShapeDtypeStruct((bsz, seq, HGRN_WIDTH), F32),
        compiler_params=_cparams(("arbitrary", "arbitrary")),
        name="hgrn_mixer",
    )(hqig, hqig, hqig, hf, log_lb, log1m_lb, om_lb, nw2)


def _outproj_kernel(fox_ref, s5_ref, hg_ref, x_ref, mod_ref, nf_ref, ns_ref, w_ref, o_ref):
    of = (_rms(fox_ref[...]) * nf_ref[...]).astype(BF16)
    os5 = (_rms(s5_ref[...]) * ns_ref[...]).astype(BF16)
    oh = hg_ref[...].astype(BF16)
    a, b = FOX_WIDTH, FOX_WIDTH + S5_WIDTH
    mix = jnp.dot(of, w_ref[0:a, :], preferred_element_type=F32)
    mix = mix + jnp.dot(os5, w_ref[a:b, :], preferred_element_type=F32)
    mix = mix + jnp.dot(oh, w_ref[b:, :], preferred_element_type=F32)
    o_ref[...] = x_ref[...] + mod_ref[2:3, :] * mix


def _out_proj(o_fox, o_s5, o_hgrn, x, mod_l, nf, ns, w_out, tm):
    bsz, seq, d = x.shape
    tm = min(tm, seq)
    row = lambda w: pl.BlockSpec((None, tm, w), lambda b, i: (b, i, 0))
    return pl.pallas_call(
        _outproj_kernel,
        grid=(bsz, seq // tm),
        in_specs=[
            row(FOX_WIDTH), row(S5_WIDTH), row(HGRN_WIDTH), row(d),
            pl.BlockSpec((None, 6, d), lambda b, i: (b, 0, 0)),
            pl.BlockSpec((1, FOX_WIDTH), lambda b, i: (0, 0)),
            pl.BlockSpec((1, S5_WIDTH), lambda b, i: (0, 0)),
            pl.BlockSpec((d, d), lambda b, i: (0, 0)),
        ],
        out_specs=row(d),
        out_shape=jax.ShapeDtypeStruct((bsz, seq, d), F32),
        compiler_params=_cparams(("arbitrary", "arbitrary")),
        name="out_proj",
    )(o_fox, o_s5, o_hgrn, x, mod_l, nf, ns, w_out)


def _routing(logits_t):
    mx = jnp.max(logits_t, axis=0, keepdims=True)
    ex = jnp.exp(logits_t - mx)
    probs = ex / jnp.sum(ex, axis=0, keepdims=True)
    p = [probs[e:e + 1, :] for e in range(N_EXPERTS)]
    epg = N_EXPERTS // N_GROUPS
    scores = []
    for g in range(N_GROUPS):
        a, b, c, d = p[epg * g:epg * g + epg]
        hi1, lo1 = jnp.maximum(a, b), jnp.minimum(a, b)
        hi2, lo2 = jnp.maximum(c, d), jnp.minimum(c, d)
        top1 = jnp.maximum(hi1, hi2)
        top2 = jnp.maximum(jnp.minimum(hi1, hi2), jnp.maximum(lo1, lo2))
        scores.append(top1 + top2)
    best = jnp.zeros_like(scores[0], dtype=jnp.int32)
    bs = scores[0]
    for g in range(1, N_GROUPS):
        upd = scores[g] > bs
        best = jnp.where(upd, g, best)
        bs = jnp.where(upd, scores[g], bs)
    ig = []
    for j in range(epg):
        val = p[j]
        for g in range(1, N_GROUPS):
            val = jnp.where(best == g, p[epg * g + j], val)
        ig.append(val)
    i1 = jnp.zeros_like(best)
    w1 = ig[0]
    for j in range(1, epg):
        upd = ig[j] > w1
        i1 = jnp.where(upd, j, i1)
        w1 = jnp.where(upd, ig[j], w1)
    i2 = jnp.zeros_like(best)
    w2 = jnp.full_like(w1, -1.0)
    for j in range(epg):
        upd = jnp.logical_and(i1 != j, ig[j] > w2)
        i2 = jnp.where(upd, j, i2)
        w2 = jnp.where(upd, ig[j], w2)
    den = w1 + w2
    tw1 = w1 / den
    tw2 = w2 / den
    e1 = best * epg + i1
    e2 = best * epg + i2
    return [jnp.where(e1 == e, tw1, 0.0) + jnp.where(e2 == e, tw2, 0.0) for e in range(N_EXPERTS)]


def _moe_kernel(x_ref, mod_ref, nw_ref, rw_ref, rb_ref, wg_ref, wu_ref, wd_ref, fnw_ref,
                o_ref, h_sc, comb_sc, combt_sc, acc_sc, *, final_norm):
    e = pl.program_id(2)
    tm = x_ref.shape[0]

    @pl.when(e == 0)
    def _():
        h = _rms(x_ref[...]) * nw_ref[...]
        h = h * (1.0 + mod_ref[4:5, :]) + mod_ref[3:4, :]
        h_sc[...] = h.astype(BF16)
        logits = jnp.dot(h, rw_ref[...], precision=HIGHEST, preferred_element_type=F32) + rb_ref[...]
        rows = _routing(logits.T[0:N_EXPERTS, :])
        combt_sc[...] = jnp.zeros_like(combt_sc)
        for j in range(N_EXPERTS):
            combt_sc[j:j + 1, :] = rows[j]
        comb_sc[...] = combt_sc[...].T
        acc_sc[...] = jnp.zeros_like(acc_sc)

    hb = h_sc[...]
    gate = jnp.dot(hb, wg_ref[...], preferred_element_type=F32)
    up = jnp.dot(hb, wu_ref[...], preferred_element_type=F32)
    lane = lax.broadcasted_iota(jnp.int32, (tm, LANES), 1)
    w_e = jnp.sum(jnp.where(lane == e, comb_sc[...], 0.0), axis=-1, keepdims=True)
    hid = (_silu(gate) * up * w_e).astype(BF16)
    acc_sc[...] += jnp.dot(hid, wd_ref[...], preferred_element_type=F32)

    @pl.when(e == N_EXPERTS - 1)
    def _():
        y = x_ref[...] + mod_ref[5:6, :] * acc_sc[...]
        if final_norm:
            y = _rms(y) * fnw_ref[...]
        o_ref[...] = y


def _moe(x, mod_l, norm_w, rw_pad, rb_pad, wg, wu, wd, fnw, tm, final_norm):
    bsz, seq, d = x.shape
    tm = min(tm, seq)
    kern = functools.partial(_moe_kernel, final_norm=final_norm)
    return pl.pallas_call(
        kern,
        grid=(bsz, seq // tm, N_EXPERTS),
        in_specs=[
            pl.BlockSpec((None, tm, d), lambda b, i, e: (b, i, 0)),
            pl.BlockSpec((None, 6, d), lambda b, i, e: (b, 0, 0)),
            pl.BlockSpec((1, d), lambda b, i, e: (0, 0)),
            pl.BlockSpec((d, LANES), lambda b, i, e: (0, 0)),
            pl.BlockSpec((1, LANES), lambda b, i, e: (0, 0)),
            pl.BlockSpec((None, d, D_FF), lambda b, i, e: (e, 0, 0)),
            pl.BlockSpec((None, d, D_FF), lambda b, i, e: (e, 0, 0)),
            pl.BlockSpec((None, D_FF, d), lambda b, i, e: (e, 0, 0)),
            pl.BlockSpec((1, d), lambda b, i, e: (0, 0)),
        ],
        out_specs=pl.BlockSpec((None, tm, d), lambda b, i, e: (b, i, 0)),
        out_shape=jax.ShapeDtypeStruct((bsz, seq, d), F32),
        scratch_shapes=[
            pltpu.VMEM((tm, d), BF16),
            pltpu.VMEM((tm, LANES), F32),
            pltpu.VMEM((LANES, tm), F32),
            pltpu.VMEM((tm, d), F32),
        ],
        compiler_params=_cparams(("arbitrary", "arbitrary", "arbitrary")),
        name="moe",
    )(x, mod_l, norm_w, rw_pad, rb_pad, wg, wu, wd, fnw)


def _prep_w_in(w_in_l, f_bias_l):
    a = FOX_WIDTH
    o_ff = 3 * a
    o_su = o_ff + FOX_HEADS
    o_hq = o_su + S5_WIDTH
    hw = HGRN_WIDTH
    fq = w_in_l[:, 0:a] * (HEAD_DIM ** -0.5)
    fk, fv = w_in_l[:, a:2 * a], w_in_l[:, 2 * a:3 * a]
    ff = jnp.pad(w_in_l[:, o_ff:o_su], ((0, 0), (0, LANES - FOX_HEADS)))
    su = w_in_l[:, o_su:o_hq]
    hq = w_in_l[:, o_hq:o_hq + hw]
    hf = w_in_l[:, o_hq + hw:o_hq + 2 * hw]
    hi = w_in_l[:, o_hq + 2 * hw:o_hq + 3 * hw]
    hg = w_in_l[:, o_hq + 3 * hw:o_hq + 4 * hw]
    w = jnp.concatenate([fq, fk, fv, su, hq, hi, hg, hf, ff], axis=1).astype(BF16)
    fb = jnp.pad(f_bias_l.astype(F32), (0, LANES - FOX_HEADS)).reshape(1, LANES)
    return w, fb


def _prep_s5(a_re, a_im, b_re, b_im, c_re, c_im, log_dt, bsz):
    lam = lax.complex(a_re.astype(F32), a_im.astype(F32))
    dt = jnp.exp(log_dt.astype(F32))[:, None]
    a_bar = jnp.exp(lam * dt)
    b_bar = ((a_bar - 1.0) / lam)[..., None] * lax.complex(b_re.astype(F32), b_im.astype(F32))
    eye = jnp.eye(S5_GROUPS, dtype=F32)
    bm_re = jnp.einsum('gpc,gh->gchp', jnp.real(b_bar), eye).reshape(S5_WIDTH, S5_NSTATE)
    bm_im = jnp.einsum('gpc,gh->gchp', jnp.imag(b_bar), eye).reshape(S5_WIDTH, S5_NSTATE)
    bmat = jnp.concatenate([bm_re, bm_im], axis=1).astype(BF16)
    cm_re = jnp.einsum('gcp,gh->gphc', c_re.astype(F32), eye).reshape(S5_NSTATE, S5_WIDTH)
    cm_im = jnp.einsum('gcp,gh->gphc', c_im.astype(F32), eye).reshape(S5_NSTATE, S5_WIDTH)
    cmat = jnp.concatenate([cm_re, -cm_im], axis=0).astype(BF16)
    ar = jnp.broadcast_to(jnp.real(a_bar).reshape(1, S5_NSTATE), (bsz, S5_NSTATE))
    ai = jnp.broadcast_to(jnp.imag(a_bar).reshape(1, S5_NSTATE), (bsz, S5_NSTATE))
    return bmat, ar, ai, cmat


def kernel(x, c, ada_w, ada_b, norm_mix_w, norm_ffn_w, w_in, fox_f_bias, s5_a_re, s5_a_im, s5_b_re, s5_b_im, s5_c_re, s5_c_im, s5_d, s5_log_dt, s5_glu_w, s5_glu_b, hgrn_lb_logits, hgrn_norm_w, branch_norm_fox, branch_norm_s5, w_out, router_w, router_b, moe_w_gate, moe_w_up, moe_w_down, final_norm_w):
    bsz, seq, d = x.shape
    depth = w_in.shape[0]
    lb_cum = jnp.cumsum(jax.nn.softmax(hgrn_lb_logits.astype(F32), axis=0), axis=0)
    lower = lb_cum - lb_cum[0:1]
    log_lb = jnp.log(lower)
    log1m_lb = jnp.log1p(-lower)
    om_lb = 1.0 - lower

    mod = _ada_mod(c, ada_w, ada_b)
    rw_pad = jnp.pad(router_w.astype(F32), ((0, 0), (0, LANES - N_EXPERTS)))
    rb_pad = jnp.pad(router_b.astype(F32), (0, LANES - N_EXPERTS)).reshape(1, LANES)
    fnw = final_norm_w.reshape(1, d).astype(F32)

    for l in range(depth):
        w_pad, fb_pad = _prep_w_in(w_in[l], fox_f_bias[l])
        qkv, su, hqig, hf, cum_t = _in_proj(x, mod[l], norm_mix_w[l].reshape(1, d), w_pad, fb_pad, tm=512)
        o_fox = _fox_attention(qkv, cum_t, tq=256, tk=512)
        bmat, ar, ai, cmat = _prep_s5(s5_a_re[l], s5_a_im[l], s5_b_re[l], s5_b_im[l],
                                      s5_c_re[l], s5_c_im[l], s5_log_dt[l], bsz)
        o_s5 = _s5_mixer(su, bmat, ar, ai, cmat, s5_d[l].reshape(1, S5_WIDTH).astype(F32),
                         s5_glu_w[l].astype(BF16), s5_glu_b[l].reshape(1, S5_WIDTH).astype(F32), ln=64)
        nw2 = jnp.tile(hgrn_norm_w[l].astype(F32), 2).reshape(1, LANES)
        o_hgrn = _hgrn_mixer(hqig, hf, log_lb[l].reshape(1, -1), log1m_lb[l].reshape(1, -1),
                             om_lb[l].reshape(1, -1), nw2, tc=128)
        x = _out_proj(o_fox, o_s5, o_hgrn, x, mod[l],
                      branch_norm_fox[l].reshape(1, -1).astype(F32),
                      branch_norm_s5[l].reshape(1, -1).astype(F32),
                      w_out[l].astype(BF16), tm=512)
        x = _moe(x, mod[l], norm_ffn_w[l].reshape(1, d).astype(F32), rw_pad, rb_pad,
                 moe_w_gate[l].astype(BF16), moe_w_up[l].astype(BF16), moe_w_down[l].astype(BF16),
                 fnw, tm=1024, final_norm=(l == depth - 1))
    return x
```

```python
import functools
import math

import jax
import jax.numpy as jnp
from jax import lax
from jax.experimental import pallas as pl
from jax.experimental.pallas import tpu as pltpu

F32 = jnp.float32
BF16 = jnp.bfloat16
HIGHEST = lax.Precision.HIGHEST

D_MODEL = 1024
FOX_HEADS = 6
HEAD_DIM = 64
FOX_WIDTH = 384
S5_WIDTH = 256
S5_GROUPS = 16
S5_CH = 16
S5_STATE = 64
S5_NSTATE = S5_GROUPS * S5_STATE
HGRN_WIDTH = 384
N_EXPERTS = 16
N_GROUPS = 4
D_FF = 256
EPS = 1e-6
LANES = 128
NEG_BIG = -1e30
LOG2E = math.log2(math.e)
SOFTMAX_ROWS = 64

C_QKV = (0, 1152)
C_SU = (1152, 1408)
C_HQIG = (1408, 2560)
C_HF = (2560, 2944)
C_FF = (2944, 3072)
N_IN_PAD = 3072

NT_DIMS = (((1,), (1,)), ((), ()))

VMEM_LIMIT = 56 * 1024 * 1024


def _cparams(sem):
    return pltpu.CompilerParams(dimension_semantics=sem, vmem_limit_bytes=VMEM_LIMIT)


def _sigmoid(x):
    return 1.0 / (1.0 + jnp.exp(-x))


def _silu(x):
    return x * _sigmoid(x)


def _log_sigmoid(x):
    return jnp.minimum(x, 0.0) - jnp.log1p(jnp.exp(-jnp.abs(x)))


def _rms(x):
    return x * lax.rsqrt(jnp.mean(x * x, axis=-1, keepdims=True) + EPS)


def _ada_kernel(c_ref, w_ref, b_ref, o_ref):
    c = c_ref[...]
    o_ref[...] = jnp.dot(_silu(c), w_ref[...], precision=HIGHEST,
                         preferred_element_type=F32) + b_ref[...]


def _ada_mod(c, ada_w, ada_b):
    depth, d, n6 = ada_w.shape
    bsz = c.shape[0]
    nblk = n6 // d
    out = pl.pallas_call(
        _ada_kernel,
        grid=(depth, nblk),
        in_specs=[
            pl.BlockSpec((bsz, d), lambda l, j: (0, 0)),
            pl.BlockSpec((None, d, d), lambda l, j: (l, 0, j)),
            pl.BlockSpec((None, 1, d), lambda l, j: (l, 0, j)),
        ],
        out_specs=pl.BlockSpec((None, bsz, d), lambda l, j: (l, 0, j)),
        out_shape=jax.ShapeDtypeStruct((depth, bsz, n6), F32),
        compiler_params=_cparams(("arbitrary", "arbitrary")),
        name="ada_mod",
    )(c, ada_w, ada_b.reshape(depth, 1, n6))
    return out.reshape(depth, bsz, nblk, d)


def _inproj_kernel(x_ref, mod_ref, nw_ref, w_ref, fb_ref,
                   qkv_ref, su_ref, hqig_ref, hf_ref, cum_ref, carry_ref):
    i = pl.program_id(1)
    tm = x_ref.shape[0]

    @pl.when(i == 0)
    def _():
        carry_ref[...] = jnp.zeros_like(carry_ref)

    h = _rms(x_ref[...]) * nw_ref[...]
    h = h * (1.0 + mod_ref[1:2, :]) + mod_ref[0:1, :]
    hb = h.astype(BF16)

    def proj(c):
        return jnp.dot(hb, w_ref[:, c[0]:c[1]], preferred_element_type=F32)

    qkv_ref[...] = proj(C_QKV).astype(BF16)
    su = proj(C_SU)
    su_ref[0] = su[:, :LANES]
    su_ref[1] = su[:, LANES:]
    hqig_ref[...] = proj(C_HQIG).astype(BF16)
    hf_ref[...] = proj(C_HF)

    lf = _log_sigmoid(proj(C_FF) + fb_ref[...])
    lf_t = lf.T[0:8, :]
    r = lax.broadcasted_iota(jnp.int32, (tm, tm), 0)
    c = lax.broadcasted_iota(jnp.int32, (tm, tm), 1)
    tri_u = jnp.where(r <= c, 1.0, 0.0).astype(F32)
    cum = jnp.dot(lf_t, tri_u, precision=HIGHEST, preferred_element_type=F32) + carry_ref[:, 0:1]
    cum_ref[...] = cum * LOG2E
    carry_ref[...] = jnp.broadcast_to(cum[:, tm - 1:tm], carry_ref.shape)


def _in_proj(x, mod_l, norm_w, w_pad, fb_pad, tm):
    bsz, seq, d = x.shape
    tm = min(tm, seq)
    return pl.pallas_call(
        _inproj_kernel,
        grid=(bsz, seq // tm),
        in_specs=[
            pl.BlockSpec((None, tm, d), lambda b, i: (b, i, 0)),
            pl.BlockSpec((None, 6, d), lambda b, i: (b, 0, 0)),
            pl.BlockSpec((1, d), lambda b, i: (0, 0)),
            pl.BlockSpec((d, N_IN_PAD), lambda b, i: (0, 0)),
            pl.BlockSpec((1, LANES), lambda b, i: (0, 0)),
        ],
        out_specs=[
            pl.BlockSpec((None, tm, 1152), lambda b, i: (b, i, 0)),
            pl.BlockSpec((2, None, tm, LANES), lambda b, i: (0, b, i, 0)),
            pl.BlockSpec((None, tm, 1152), lambda b, i: (b, i, 0)),
            pl.BlockSpec((None, tm, HGRN_WIDTH), lambda b, i: (b, i, 0)),
            pl.BlockSpec((None, 8, tm), lambda b, i: (b, 0, i)),
        ],
        out_shape=[
            jax.ShapeDtypeStruct((bsz, seq, 1152), BF16),
            jax.ShapeDtypeStruct((2, bsz, seq, LANES), F32),
            jax.ShapeDtypeStruct((bsz, seq, 1152), BF16),
            jax.ShapeDtypeStruct((bsz, seq, HGRN_WIDTH), F32),
            jax.ShapeDtypeStruct((bsz, 8, seq), F32),
        ],
        scratch_shapes=[pltpu.VMEM((8, LANES), F32)],
        compiler_params=_cparams(("arbitrary", "arbitrary")),
        name="in_proj",
    )(x, mod_l, norm_w, w_pad, fb_pad)


def _fox_kernel(q_ref, k_ref, v_ref, cum_ref, o_ref, s_sc, p_sc, al_sc, m_sc, l_sc, acc_sc, *, tq, tk):
    p = pl.program_id(1)
    qi = pl.program_id(2)
    q = q_ref[...]
    lane = lax.broadcasted_iota(jnp.int32, (tq, LANES), 1)
    first = lane < HEAD_DIM
    zero = jnp.zeros_like(q)
    q2 = jnp.concatenate([jnp.where(first, q, zero), jnp.where(first, zero, q)], axis=0)
    n_full = (qi * tq) // tk

    def qk(j):
        k0 = pl.multiple_of(j * tk, tk)
        s = lax.dot_general(q2, k_ref[pl.ds(k0, tk), :], NT_DIMS, preferred_element_type=F32)
        bias_a = cum_ref[pl.ds(2 * p, 1), pl.ds(k0, tk)]
        bias_b = cum_ref[pl.ds(2 * p + 1, 1), pl.ds(k0, tk)]
        return jnp.concatenate([s[:tq] - bias_a, s[tq:] - bias_b], axis=0)

    def softmax(masked):
        for r0 in range(0, 2 * tq, SOFTMAX_ROWS):
            rs = slice(r0, r0 + SOFTMAX_ROWS)
            s = s_sc[rs, :]
            if masked:
                rows = lax.broadcasted_iota(jnp.int32, s.shape, 0) + (qi * tq + r0 % tq)
                cols = lax.broadcasted_iota(jnp.int32, s.shape, 1) + n_full * tk
                s = jnp.where(cols <= rows, s, NEG_BIG)
            m_prev = m_sc[rs, :]
            m_next = jnp.maximum(m_prev, jnp.max(s, axis=1, keepdims=True))
            pr = jnp.exp2(s - jnp.tile(m_next, (1, tk // LANES)))
            alpha = jnp.exp2(m_prev - m_next)
            l_sc[rs, :] = alpha * l_sc[rs, :] + jnp.sum(pr, axis=1, keepdims=True)
            m_sc[rs, :] = m_next
            p_sc[rs, :] = pr.astype(BF16)
            al_sc[rs, :] = alpha

    def pv(j, pr, alpha):
        k0 = pl.multiple_of(j * tk, tk)
        acc_sc[...] = acc_sc[...] * alpha + jnp.dot(pr, v_ref[pl.ds(k0, tk), :], preferred_element_type=F32)

    m_sc[...] = jnp.full(m_sc.shape, NEG_BIG, F32)
    l_sc[...] = jnp.zeros(l_sc.shape, F32)
    acc_sc[...] = jnp.zeros(acc_sc.shape, F32)
    p_sc[...] = jnp.zeros(p_sc.shape, BF16)
    al_sc[...] = jnp.ones(al_sc.shape, F32)
    s_sc[...] = qk(0)

    def body(j, carry):
        pv(jnp.maximum(j - 1, 0), p_sc[...], al_sc[...])
        softmax(False)
        s_sc[...] = qk(j + 1)
        return carry

    lax.fori_loop(0, n_full, body, 0)

    pv(jnp.maximum(n_full - 1, 0), p_sc[...], al_sc[...])
    softmax(True)
    pv(n_full, p_sc[...], al_sc[...])
    o = acc_sc[...] / l_sc[...]
    o_ref[...] = jnp.where(first, o[:tq], o[tq:])


def _fox_attention(qkv, cum_t, tq, tk):
    bsz, seq, _ = qkv.shape
    tq = min(tq, seq)
    tk = min(max(tk, tq), seq)
    assert tk % tq == 0 and seq % tk == 0
    npair = FOX_HEADS // 2
    kern = functools.partial(_fox_kernel, tq=tq, tk=tk)
    return pl.pallas_call(
        kern,
        grid=(bsz, npair, seq // tq),
        in_specs=[
            pl.BlockSpec((None, tq, LANES), lambda b, p, i: (b, i, p)),
            pl.BlockSpec((None, seq, LANES), lambda b, p, i: (b, 0, npair + p)),
            pl.BlockSpec((None, seq, LANES), lambda b, p, i: (b, 0, 2 * npair + p)),
            pl.BlockSpec((None, 8, seq), lambda b, p, i: (b, 0, 0)),
        ],
        out_specs=pl.BlockSpec((None, tq, LANES), lambda b, p, i: (b, i, p)),
        out_shape=jax.ShapeDtypeStruct((bsz, seq, FOX_WIDTH), F32),
        scratch_shapes=[
            pltpu.VMEM((2 * tq, tk), F32),
            pltpu.VMEM((2 * tq, tk), BF16),
            pltpu.VMEM((2 * tq, LANES), F32),
            pltpu.VMEM((2 * tq, LANES), F32),
            pltpu.VMEM((2 * tq, LANES), F32),
            pltpu.VMEM((2 * tq, LANES), F32),
        ],
        compiler_params=_cparams(("arbitrary", "arbitrary", "arbitrary")),
        name="fox_attn",
    )(qkv, qkv, qkv, cum_t)


def _gelu_tanh(x):
    return 0.5 * x * (1.0 + jnp.tanh(math.sqrt(2.0 / math.pi) * (x + 0.044715 * (x * x * x))))


def _s5_kernel(su_ref, bm_ref, ar_ref, ai_ref, cm_ref, d_ref, gw_ref, gb_ref, o_ref,
               utb, bu, ytb, st):
    i = pl.program_id(0)
    nb = su_ref.shape[1]
    ln = su_ref.shape[2]
    ns = S5_NSTATE

    @pl.when(i == 0)
    def _():
        st[...] = jnp.zeros_like(st)

    for k in range(2):
        for b in range(nb):
            utb[k, pl.ds(b, ln, stride=nb), :] = su_ref[k, b]
    u = jnp.concatenate([utb[0], utb[1]], axis=1)
    bu[...] = jnp.dot(u.astype(BF16), bm_ref[...], preferred_element_type=F32)

    ar = ar_ref[...]
    ai = ai_ref[...]

    def body(t, carry):
        re, im = carry
        r0 = pl.multiple_of(t * nb, nb)
        bur = bu[pl.ds(r0, nb), 0:ns]
        bui = bu[pl.ds(r0, nb), ns:2 * ns]
        nre = ar * re - ai * im + bur
        nim = ar * im + ai * re + bui
        bu[pl.ds(r0, nb), 0:ns] = nre
        bu[pl.ds(r0, nb), ns:2 * ns] = nim
        return nre, nim

    re, im = lax.fori_loop(0, ln, body, (st[:, 0:ns], st[:, ns:2 * ns]))
    st[:, 0:ns] = re
    st[:, ns:2 * ns] = im

    y = jnp.dot(bu[...].astype(BF16), cm_ref[...], preferred_element_type=F32) + d_ref[...] * u
    y = _gelu_tanh(y)
    gate = _sigmoid(jnp.dot(y.astype(BF16), gw_ref[...], preferred_element_type=F32) + gb_ref[...])
    out = y * gate
    ytb[0] = out[:, :LANES]
    ytb[1] = out[:, LANES:]
    for k in range(2):
        for b in range(nb):
            o_ref[b, :, k * LANES:(k + 1) * LANES] = ytb[k, pl.ds(b, ln, stride=nb), :]


def _s5_mixer(su, bmat, ar, ai, cmat, dvec, glu_w, glu_b, ln):
    _, bsz, seq, _ = su.shape
    ln = min(ln, seq)
    rows = bsz * ln
    const = lambda shape: pl.BlockSpec(shape, lambda i: (0,) * len(shape))
    return pl.pallas_call(
        _s5_kernel,
        grid=(seq // ln,),
        in_specs=[
            pl.BlockSpec((2, bsz, ln, LANES), lambda i: (0, 0, i, 0)),
            const((S5_WIDTH, 2 * S5_NSTATE)),
            const((bsz, S5_NSTATE)),
            const((bsz, S5_NSTATE)),
            const((2 * S5_NSTATE, S5_WIDTH)),
            const((1, S5_WIDTH)),
            const((S5_WIDTH, S5_WIDTH)),
            const((1, S5_WIDTH)),
        ],
        out_specs=pl.BlockSpec((bsz, ln, S5_WIDTH), lambda i: (0, i, 0)),
        out_shape=jax.ShapeDtypeStruct((bsz, seq, S5_WIDTH), F32),
        scratch_shapes=[
            pltpu.VMEM((2, rows, LANES), F32),
            pltpu.VMEM((rows, 2 * S5_NSTATE), F32),
            pltpu.VMEM((2, rows, LANES), F32),
            pltpu.VMEM((bsz, 2 * S5_NSTATE), F32),
        ],
        compiler_params=_cparams(("arbitrary",)),
        name="s5_mixer",
    )(su, bmat, ar, ai, cmat, dvec, glu_w, glu_b)


def _block_ref(b, m):
    t = b.shape[0]
    if 2 * m >= 8:
        b3 = b.reshape(t // (2 * m), 2 * m, LANES)
        ref = jnp.broadcast_to(b3[:, m - 1:m, :], b3.shape)
        return ref.reshape(t, LANES)
    row = lax.broadcasted_iota(jnp.int32, b.shape, 0)
    res = b
    for r in range(2 * m):
        s = r - (m - 1)
        if s == 0:
            continue
        res = jnp.where(row % (2 * m) == r, pltpu.roll(b, s % t, axis=0), res)
    return res


def _hgrn_kernel(hq_ref, hi_ref, hg_ref, hf_ref, llb_ref, l1lb_ref, omlb_ref, nw_ref, o_ref, *, tc):
    seq = hf_ref.shape[0]
    lane = lax.broadcasted_iota(jnp.int32, (tc, LANES), 1)
    first = lane < HEAD_DIM
    row = lax.broadcasted_iota(jnp.int32, (tc, LANES), 0)
    rr = lax.broadcasted_iota(jnp.int32, (tc, tc), 0)
    cc = lax.broadcasted_iota(jnp.int32, (tc, tc), 1)
    tri_l = jnp.where(cc <= rr, 1.0, 0.0).astype(F32)
    sr = lax.broadcasted_iota(jnp.int32, (LANES, LANES), 0)
    sc = lax.broadcasted_iota(jnp.int32, (LANES, LANES), 1)
    blockdiag = (sr < HEAD_DIM) == (sc < HEAD_DIM)
    llb = llb_ref[...]
    l1lb = l1lb_ref[...]
    omlb = omlb_ref[...]
    nw = nw_ref[...]
    levels = []
    m = 1
    while 2 * m <= tc:
        levels.append(m)
        m *= 2

    def stack(a):
        z = jnp.zeros_like(a)
        return jnp.concatenate([jnp.where(first, a, z), jnp.where(first, z, a)], axis=0).astype(BF16)

    def chunk(c, st):
        r0 = pl.multiple_of(c * tc, tc)
        z = hf_ref[pl.ds(r0, tc), :]
        q = _silu(hq_ref[pl.ds(r0, tc), :].astype(F32))
        v = hi_ref[pl.ds(r0, tc), :]
        g = hg_ref[pl.ds(r0, tc), :].astype(F32)
        lsz = l1lb + _log_sigmoid(z)
        log_f = jnp.maximum(llb, lsz) + jnp.log1p(jnp.exp(-jnp.abs(llb - lsz)))
        kk = omlb * _sigmoid(-z)
        b = jnp.dot(tri_l, log_f, precision=HIGHEST, preferred_element_type=F32)

        st_b = st.astype(BF16)
        o = lax.dot_general((q * jnp.exp(b)).astype(BF16), st_b, NT_DIMS, preferred_element_type=F32)

        tot = jnp.where(jnp.concatenate([rr == cc, rr == cc], axis=0),
                        lax.dot_general(stack(q), kk.astype(BF16), NT_DIMS, preferred_element_type=F32), 0.0)
        for m in levels:
            dm = b - _block_ref(b, m)
            upper = (row // m) % 2 == 1
            qm = q * jnp.exp(jnp.where(upper, dm, NEG_BIG))
            km = kk * jnp.exp(jnp.where(upper, NEG_BIG, -dm))
            s = lax.dot_general(stack(qm), km.astype(BF16), NT_DIMS, preferred_element_type=F32)
            if 2 * m < tc:
                same = (rr // (2 * m)) == (cc // (2 * m))
                s = jnp.where(jnp.concatenate([same, same], axis=0), s, 0.0)
            tot = tot + s
        pv = jnp.dot(tot.astype(BF16), v, preferred_element_type=F32)
        o = o + jnp.where(first, pv[:tc], pv[tc:])

        b_last = b[tc - 1:tc, :]
        ke = kk * jnp.exp(b_last - b)
        upd = jnp.dot(v.astype(F32).T.astype(BF16), ke.astype(BF16), preferred_element_type=F32)
        st_new = st * jnp.exp(b_last) + jnp.where(blockdiag, upd, 0.0)

        o2 = o * o
        s_a = jnp.sum(jnp.where(first, o2, 0.0), axis=-1, keepdims=True)
        s_b = jnp.sum(jnp.where(first, 0.0, o2), axis=-1, keepdims=True)
        ms = jnp.where(first, s_a, s_b) * (1.0 / HEAD_DIM)
        o_ref[pl.ds(r0, tc), :] = o * lax.rsqrt(ms + EPS) * nw * _silu(g)
        return st_new

    lax.fori_loop(0, seq // tc, chunk, jnp.zeros((LANES, LANES), F32))


def _hgrn_mixer(hqig, hf, log_lb, log1m_lb, om_lb, nw2, tc):
    bsz, seq, _ = hf.shape
    tc = min(tc, seq)
    npair = HGRN_WIDTH // LANES
    kern = functools.partial(_hgrn_kernel, tc=tc)
    col = lambda off: pl.BlockSpec((None, seq, LANES), lambda b, p: (b, 0, off + p))
    par = pl.BlockSpec((1, LANES), lambda b, p: (0, p))
    return pl.pallas_call(
        kern,
        grid=(bsz, npair),
        in_specs=[col(0), col(npair), col(2 * npair), col(0), par, par, par,
                  pl.BlockSpec((1, LANES), lambda b, p: (0, 0))],
        out_specs=col(0),
        out_shape=jax.ShapeDtypeStruct((bsz, seq, HGRN_WIDTH), F32),
        compiler_params=_cparams(("arbitrary", "arbitrary")),
        name="hgrn_mixer",
    )(hqig, hqig, hqig, hf, log_lb, log1m_lb, om_lb, nw2)


def _outproj_kernel(fox_ref, s5_ref, hg_ref, x_ref, mod_ref, nf_ref, ns_ref, w_ref, o_ref):
    of = (_rms(fox_ref[...]) * nf_ref[...]).astype(BF16)
    os5 = (_rms(s5_ref[...]) * ns_ref[...]).astype(BF16)
    oh = hg_ref[...].astype(BF16)
    a, b = FOX_WIDTH, FOX_WIDTH + S5_WIDTH
    mix = jnp.dot(of, w_ref[0:a, :], preferred_element_type=F32)
    mix = mix + jnp.dot(os5, w_ref[a:b, :], preferred_element_type=F32)
    mix = mix + jnp.dot(oh, w_ref[b:, :], preferred_element_type=F32)
    o_ref[...] = x_ref[...] + mod_ref[2:3, :] * mix


def _out_proj(o_fox, o_s5, o_hgrn, x, mod_l, nf, ns, w_out, tm):
    bsz, seq, d = x.shape
    tm = min(tm, seq)
    row = lambda w: pl.BlockSpec((None, tm, w), lambda b, i: (b, i, 0))
    return pl.pallas_call(
        _outproj_kernel,
        grid=(bsz, seq // tm),
        in_specs=[
            row(FOX_WIDTH), row(S5_WIDTH), row(HGRN_WIDTH), row(d),
            pl.BlockSpec((None, 6, d), lambda b, i: (b, 0, 0)),
            pl.BlockSpec((1, FOX_WIDTH), lambda b, i: (0, 0)),
            pl.BlockSpec((1, S5_WIDTH), lambda b, i: (0, 0)),
            pl.BlockSpec((d, d), lambda b, i: (0, 0)),
        ],
        out_specs=row(d),
        out_shape=jax.ShapeDtypeStruct((bsz, seq, d), F32),
        compiler_params=_cparams(("arbitrary", "arbitrary")),
        name="out_proj",
    )(o_fox, o_s5, o_hgrn, x, mod_l, nf, ns, w_out)


def _routing(logits_t):
    mx = jnp.max(logits_t, axis=0, keepdims=True)
    ex = jnp.exp(logits_t - mx)
    probs = ex / jnp.sum(ex, axis=0, keepdims=True)
    p = [probs[e:e + 1, :] for e in range(N_EXPERTS)]
    epg = N_EXPERTS // N_GROUPS
    scores = []
    for g in range(N_GROUPS):
        a, b, c, d = p[epg * g:epg * g + epg]
        hi1, lo1 = jnp.maximum(a, b), jnp.minimum(a, b)
        hi2, lo2 = jnp.maximum(c, d), jnp.minimum(c, d)
        top1 = jnp.maximum(hi1, hi2)
        top2 = jnp.maximum(jnp.minimum(hi1, hi2), jnp.maximum(lo1, lo2))
        scores.append(top1 + top2)
    best = jnp.zeros_like(scores[0], dtype=jnp.int32)
    bs = scores[0]
    for g in range(1, N_GROUPS):
        upd = scores[g] > bs
        best = jnp.where(upd, g, best)
        bs = jnp.where(upd, scores[g], bs)
    ig = []
    for j in range(epg):
        val = p[j]
        for g in range(1, N_GROUPS):
            val = jnp.where(best == g, p[epg * g + j], val)
        ig.append(val)
    i1 = jnp.zeros_like(best)
    w1 = ig[0]
    for j in range(1, epg):
        upd = ig[j] > w1
        i1 = jnp.where(upd, j, i1)
        w1 = jnp.where(upd, ig[j], w1)
    i2 = jnp.zeros_like(best)
    w2 = jnp.full_like(w1, -1.0)
    for j in range(epg):
        upd = jnp.logical_and(i1 != j, ig[j] > w2)
        i2 = jnp.where(upd, j, i2)
        w2 = jnp.where(upd, ig[j], w2)
    den = w1 + w2
    tw1 = w1 / den
    tw2 = w2 / den
    e1 = best * epg + i1
    e2 = best * epg + i2
    return [jnp.where(e1 == e, tw1, 0.0) + jnp.where(e2 == e, tw2, 0.0) for e in range(N_EXPERTS)]


def _moe_kernel(x_ref, mod_ref, nw_ref, rw_ref, rb_ref, wg_ref, wu_ref, wd_ref, fnw_ref,
                o_ref, h_sc, comb_sc, combt_sc, acc_sc, *, final_norm):
    e = pl.program_id(2)
    tm = x_ref.shape[0]

    @pl.when(e == 0)
    def _():
        h = _rms(x_ref[...]) * nw_ref[...]
        h = h * (1.0 + mod_ref[4:5, :]) + mod_ref[3:4, :]
        h_sc[...] = h.astype(BF16)
        logits = jnp.dot(h, rw_ref[...], precision=HIGHEST, preferred_element_type=F32) + rb_ref[...]
        rows = _routing(logits.T[0:N_EXPERTS, :])
        combt_sc[...] = jnp.zeros_like(combt_sc)
        for j in range(N_EXPERTS):
            combt_sc[j:j + 1, :] = rows[j]
        comb_sc[...] = combt_sc[...].T
        acc_sc[...] = jnp.zeros_like(acc_sc)

    hb = h_sc[...]
    gate = jnp.dot(hb, wg_ref[...], preferred_element_type=F32)
    up = jnp.dot(hb, wu_ref[...], preferred_element_type=F32)
    lane = lax.broadcasted_iota(jnp.int32, (tm, LANES), 1)
    w_e = jnp.sum(jnp.where(lane == e, comb_sc[...], 0.0), axis=-1, keepdims=True)
    hid = (_silu(gate) * up * w_e).astype(BF16)
    acc_sc[...] += jnp.dot(hid, wd_ref[...], preferred_element_type=F32)

    @pl.when(e == N_EXPERTS - 1)
    def _():
        y = x_ref[...] + mod_ref[5:6, :] * acc_sc[...]
        if final_norm:
            y = _rms(y) * fnw_ref[...]
        o_ref[...] = y


def _moe(x, mod_l, norm_w, rw_pad, rb_pad, wg, wu, wd, fnw, tm, final_norm):
    bsz, seq, d = x.shape
    tm = min(tm, seq)
    kern = functools.partial(_moe_kernel, final_norm=final_norm)
    return pl.pallas_call(
        kern,
        grid=(bsz, seq // tm, N_EXPERTS),
        in_specs=[
            pl.BlockSpec((None, tm, d), lambda b, i, e: (b, i, 0)),
            pl.BlockSpec((None, 6, d), lambda b, i, e: (b, 0, 0)),
            pl.BlockSpec((1, d), lambda b, i, e: (0, 0)),
            pl.BlockSpec((d, LANES), lambda b, i, e: (0, 0)),
            pl.BlockSpec((1, LANES), lambda b, i, e: (0, 0)),
            pl.BlockSpec((None, d, D_FF), lambda b, i, e: (e, 0, 0)),
            pl.BlockSpec((None, d, D_FF), lambda b, i, e: (e, 0, 0)),
            pl.BlockSpec((None, D_FF, d), lambda b, i, e: (e, 0, 0)),
            pl.BlockSpec((1, d), lambda b, i, e: (0, 0)),
        ],
        out_specs=pl.BlockSpec((None, tm, d), lambda b, i, e: (b, i, 0)),
        out_shape=jax.ShapeDtypeStruct((bsz, seq, d), F32),
        scratch_shapes=[
            pltpu.VMEM((tm, d), BF16),
            pltpu.VMEM((tm, LANES), F32),
            pltpu.VMEM((LANES, tm), F32),
            pltpu.VMEM((tm, d), F32),
        ],
        compiler_params=_cparams(("arbitrary", "arbitrary", "arbitrary")),
        name="moe",
    )(x, mod_l, norm_w, rw_pad, rb_pad, wg, wu, wd, fnw)


def _prep_w_in(w_in_l, f_bias_l):
    a = FOX_WIDTH
    o_ff = 3 * a
    o_su = o_ff + FOX_HEADS
    o_hq = o_su + S5_WIDTH
    hw = HGRN_WIDTH
    fq = w_in_l[:, 0:a] * (HEAD_DIM ** -0.5 * LOG2E)
    fk, fv = w_in_l[:, a:2 * a], w_in_l[:, 2 * a:3 * a]
    ff = jnp.pad(w_in_l[:, o_ff:o_su], ((0, 0), (0, LANES - FOX_HEADS)))
    su = w_in_l[:, o_su:o_hq]
    hq = w_in_l[:, o_hq:o_hq + hw]
    hf = w_in_l[:, o_hq + hw:o_hq + 2 * hw]
    hi = w_in_l[:, o_hq + 2 * hw:o_hq + 3 * hw]
    hg = w_in_l[:, o_hq + 3 * hw:o_hq + 4 * hw]
    w = jnp.concatenate([fq, fk, fv, su, hq, hi, hg, hf, ff], axis=1).astype(BF16)
    fb = jnp.pad(f_bias_l.astype(F32), (0, LANES - FOX_HEADS)).reshape(1, LANES)
    return w, fb


def _prep_s5(a_re, a_im, b_re, b_im, c_re, c_im, log_dt, bsz):
    lam = lax.complex(a_re.astype(F32), a_im.astype(F32))
    dt = jnp.exp(log_dt.astype(F32))[:, None]
    a_bar = jnp.exp(lam * dt)
    b_bar = ((a_bar - 1.0) / lam)[..., None] * lax.complex(b_re.astype(F32), b_im.astype(F32))
    eye = jnp.eye(S5_GROUPS, dtype=F32)
    bm_re = jnp.einsum('gpc,gh->gchp', jnp.real(b_bar), eye).reshape(S5_WIDTH, S5_NSTATE)
    bm_im = jnp.einsum('gpc,gh->gchp', jnp.imag(b_bar), eye).reshape(S5_WIDTH, S5_NSTATE)
    bmat = jnp.concatenate([bm_re, bm_im], axis=1).astype(BF16)
    cm_re = jnp.einsum('gcp,gh->gphc', c_re.astype(F32), eye).reshape(S5_NSTATE, S5_WIDTH)
    cm_im = jnp.einsum('gcp,gh->gphc', c_im.astype(F32), eye).reshape(S5_NSTATE, S5_WIDTH)
    cmat = jnp.concatenate([cm_re, -cm_im], axis=0).astype(BF16)
    ar = jnp.broadcast_to(jnp.real(a_bar).reshape(1, S5_NSTATE), (bsz, S5_NSTATE))
    ai = jnp.broadcast_to(jnp.imag(a_bar).reshape(1, S5_NSTATE), (bsz, S5_NSTATE))
    return bmat, ar, ai, cmat


def kernel(x, c, ada_w, ada_b, norm_mix_w, norm_ffn_w, w_in, fox_f_bias, s5_a_re, s5_a_im, s5_b_re, s5_b_im, s5_c_re, s5_c_im, s5_d, s5_log_dt, s5_glu_w, s5_glu_b, hgrn_lb_logits, hgrn_norm_w, branch_norm_fox, branch_norm_s5, w_out, router_w, router_b, moe_w_gate, moe_w_up, moe_w_down, final_norm_w):
    bsz, seq, d = x.shape
    depth = w_in.shape[0]
    lb_cum = jnp.cumsum(jax.nn.softmax(hgrn_lb_logits.astype(F32), axis=0), axis=0)
    lower = lb_cum - lb_cum[0:1]
    log_lb = jnp.log(lower)
    log1m_lb = jnp.log1p(-lower)
    om_lb = 1.0 - lower

    mod = _ada_mod(c, ada_w, ada_b)
    rw_pad = jnp.pad(router_w.astype(F32), ((0, 0), (0, LANES - N_EXPERTS)))
    rb_pad = jnp.pad(router_b.astype(F32), (0, LANES - N_EXPERTS)).reshape(1, LANES)
    fnw = final_norm_w.reshape(1, d).astype(F32)

    for l in range(depth):
        w_pad, fb_pad = _prep_w_in(w_in[l], fox_f_bias[l])
        qkv, su, hqig, hf, cum_t = _in_proj(x, mod[l], norm_mix_w[l].reshape(1, d), w_pad, fb_pad, tm=512)
        o_fox = _fox_attention(qkv, cum_t, tq=256, tk=512)
        bmat, ar, ai, cmat = _prep_s5(s5_a_re[l], s5_a_im[l], s5_b_re[l], s5_b_im[l],
                                      s5_c_re[l], s5_c_im[l], s5_log_dt[l], bsz)
        o_s5 = _s5_mixer(su, bmat, ar, ai, cmat, s5_d[l].reshape(1, S5_WIDTH).astype(F32),
                         s5_glu_w[l].astype(BF16), s5_glu_b[l].reshape(1, S5_WIDTH).astype(F32), ln=64)
        nw2 = jnp.tile(hgrn_norm_w[l].astype(F32), 2).reshape(1, LANES)
        o_hgrn = _hgrn_mixer(hqig, hf, log_lb[l].reshape(1, -1), log1m_lb[l].reshape(1, -1),
                             om_lb[l].reshape(1, -1), nw2, tc=128)
        x = _out_proj(o_fox, o_s5, o_hgrn, x, mod[l],
                      branch_norm_fox[l].reshape(1, -1).astype(F32),
                      branch_norm_s5[l].reshape(1, -1).astype(F32),
                      w_out[l].astype(BF16), tm=512)
        x = _moe(x, mod[l], norm_ffn_w[l].reshape(1, d).astype(F32), rw_pad, rb_pad,
                 moe_w_gate[l].astype(BF16), moe_w_up[l].astype(BF16), moe_w_down[l].astype(BF16),
                 fnw, tm=1024, final_norm=(l == depth - 1))
    return x
```

```python
import functools
import math

import jax
import jax.numpy as jnp
from jax import lax
from jax.experimental import pallas as pl
from jax.experimental.pallas import tpu as pltpu

F32 = jnp.float32
BF16 = jnp.bfloat16
HIGHEST = lax.Precision.HIGHEST

D_MODEL = 1024
FOX_HEADS = 6
HEAD_DIM = 64
FOX_WIDTH = 384
S5_WIDTH = 256
S5_GROUPS = 16
S5_CH = 16
S5_STATE = 64
S5_NSTATE = S5_GROUPS * S5_STATE
HGRN_WIDTH = 384
N_EXPERTS = 16
N_GROUPS = 4
D_FF = 256
EPS = 1e-6
LANES = 128
NEG_BIG = -1e30
LOG2E = math.log2(math.e)
SOFTMAX_ROWS = 64

C_QKV = (0, 1152)
C_SU = (1152, 1408)
C_HQIG = (1408, 2560)
C_HF = (2560, 2944)
C_FF = (2944, 3072)
N_IN_PAD = 3072

NT_DIMS = (((1,), (1,)), ((), ()))

VMEM_LIMIT = 56 * 1024 * 1024


def _cparams(sem):
    return pltpu.CompilerParams(dimension_semantics=sem, vmem_limit_bytes=VMEM_LIMIT)


def _sigmoid(x):
    return 1.0 / (1.0 + jnp.exp(-x))


def _silu(x):
    return x * _sigmoid(x)


def _log_sigmoid(x):
    return jnp.minimum(x, 0.0) - jnp.log1p(jnp.exp(-jnp.abs(x)))


def _rms(x):
    return x * lax.rsqrt(jnp.mean(x * x, axis=-1, keepdims=True) + EPS)


def _ada_kernel(c_ref, w_ref, b_ref, o_ref):
    c = c_ref[...]
    o_ref[...] = jnp.dot(_silu(c), w_ref[...], precision=HIGHEST,
                         preferred_element_type=F32) + b_ref[...]


def _ada_mod(c, ada_w, ada_b):
    depth, d, n6 = ada_w.shape
    bsz = c.shape[0]
    nblk = n6 // d
    out = pl.pallas_call(
        _ada_kernel,
        grid=(depth, nblk),
        in_specs=[
            pl.BlockSpec((bsz, d), lambda l, j: (0, 0)),
            pl.BlockSpec((None, d, d), lambda l, j: (l, 0, j)),
            pl.BlockSpec((None, 1, d), lambda l, j: (l, 0, j)),
        ],
        out_specs=pl.BlockSpec((None, bsz, d), lambda l, j: (l, 0, j)),
        out_shape=jax.ShapeDtypeStruct((depth, bsz, n6), F32),
        compiler_params=_cparams(("arbitrary", "arbitrary")),
        name="ada_mod",
    )(c, ada_w, ada_b.reshape(depth, 1, n6))
    return out.reshape(depth, bsz, nblk, d)


def _inproj_kernel(x_ref, mod_ref, nw_ref, w_ref, fb_ref,
                   qkv_ref, su_ref, hqig_ref, hf_ref, cum_ref, carry_ref):
    i = pl.program_id(1)
    tm = x_ref.shape[0]

    @pl.when(i == 0)
    def _():
        carry_ref[...] = jnp.zeros_like(carry_ref)

    h = _rms(x_ref[...]) * nw_ref[...]
    h = h * (1.0 + mod_ref[1:2, :]) + mod_ref[0:1, :]
    hb = h.astype(BF16)

    def proj(c):
        return jnp.dot(hb, w_ref[:, c[0]:c[1]], preferred_element_type=F32)

    qkv_ref[...] = proj(C_QKV).astype(BF16)
    su = proj(C_SU)
    su_ref[0] = su[:, :LANES]
    su_ref[1] = su[:, LANES:]
    hqig_ref[...] = proj(C_HQIG).astype(BF16)
    hf_ref[...] = proj(C_HF)

    lf = _log_sigmoid(proj(C_FF) + fb_ref[...])
    lf_t = lf.T[0:8, :]
    r = lax.broadcasted_iota(jnp.int32, (tm, tm), 0)
    c = lax.broadcasted_iota(jnp.int32, (tm, tm), 1)
    tri_u = jnp.where(r <= c, 1.0, 0.0).astype(F32)
    cum = jnp.dot(lf_t, tri_u, precision=HIGHEST, preferred_element_type=F32) + carry_ref[:, 0:1]
    cum_ref[...] = cum * LOG2E
    carry_ref[...] = jnp.broadcast_to(cum[:, tm - 1:tm], carry_ref.shape)


def _in_proj(x, mod_l, norm_w, w_pad, fb_pad, tm):
    bsz, seq, d = x.shape
    tm = min(tm, seq)
    return pl.pallas_call(
        _inproj_kernel,
        grid=(bsz, seq // tm),
        in_specs=[
            pl.BlockSpec((None, tm, d), lambda b, i: (b, i, 0)),
            pl.BlockSpec((None, 6, d), lambda b, i: (b, 0, 0)),
            pl.BlockSpec((1, d), lambda b, i: (0, 0)),
            pl.BlockSpec((d, N_IN_PAD), lambda b, i: (0, 0)),
            pl.BlockSpec((1, LANES), lambda b, i: (0, 0)),
        ],
        out_specs=[
            pl.BlockSpec((None, tm, 1152), lambda b, i: (b, i, 0)),
            pl.BlockSpec((2, None, tm, LANES), lambda b, i: (0, b, i, 0)),
            pl.BlockSpec((None, tm, 1152), lambda b, i: (b, i, 0)),
            pl.BlockSpec((None, tm, HGRN_WIDTH), lambda b, i: (b, i, 0)),
            pl.BlockSpec((None, 8, tm), lambda b, i: (b, 0, i)),
        ],
        out_shape=[
            jax.ShapeDtypeStruct((bsz, seq, 1152), BF16),
            jax.ShapeDtypeStruct((2, bsz, seq, LANES), F32),
            jax.ShapeDtypeStruct((bsz, seq, 1152), BF16),
            jax.ShapeDtypeStruct((bsz, seq, HGRN_WIDTH), F32),
            jax.ShapeDtypeStruct((bsz, 8, seq), F32),
        ],
        scratch_shapes=[pltpu.VMEM((8, LANES), F32)],
        compiler_params=_cparams(("arbitrary", "arbitrary")),
        name="in_proj",
    )(x, mod_l, norm_w, w_pad, fb_pad)


def _fox_kernel(q_ref, k_ref, v_ref, cum_ref, o_ref, s_sc, p_sc, al_sc, m_sc, l_sc, acc_sc, *, tq, tk):
    p = pl.program_id(1)
    qi = pl.program_id(2)
    q = q_ref[...]
    lane = lax.broadcasted_iota(jnp.int32, (tq, LANES), 1)
    first = lane < HEAD_DIM
    zero = jnp.zeros_like(q)
    q2 = jnp.concatenate([jnp.where(first, q, zero), jnp.where(first, zero, q)], axis=0)
    n_full = (qi * tq) // tk

    def qk(j):
        k0 = pl.multiple_of(j * tk, tk)
        s = lax.dot_general(q2, k_ref[pl.ds(k0, tk), :], NT_DIMS, preferred_element_type=F32)
        bias_a = cum_ref[pl.ds(2 * p, 1), pl.ds(k0, tk)]
        bias_b = cum_ref[pl.ds(2 * p + 1, 1), pl.ds(k0, tk)]
        return jnp.concatenate([s[:tq] - bias_a, s[tq:] - bias_b], axis=0)

    def softmax(masked):
        for r0 in range(0, 2 * tq, SOFTMAX_ROWS):
            rs = slice(r0, r0 + SOFTMAX_ROWS)
            s = s_sc[rs, :]
            if masked:
                rows = lax.broadcasted_iota(jnp.int32, s.shape, 0) + (qi * tq + r0 % tq)
                cols = lax.broadcasted_iota(jnp.int32, s.shape, 1) + n_full * tk
                s = jnp.where(cols <= rows, s, NEG_BIG)
            m_prev = m_sc[rs, :]
            m_next = jnp.maximum(m_prev, jnp.max(s, axis=1, keepdims=True))
            pr = jnp.exp2(s - jnp.tile(m_next, (1, tk // LANES)))
            alpha = jnp.exp2(m_prev - m_next)
            l_sc[rs, :] = alpha * l_sc[rs, :] + jnp.sum(pr, axis=1, keepdims=True)
            m_sc[rs, :] = m_next
            p_sc[rs, :] = pr.astype(BF16)
            al_sc[rs, :] = alpha

    def pv(j, pr, alpha):
        k0 = pl.multiple_of(j * tk, tk)
        acc_sc[...] = acc_sc[...] * alpha + jnp.dot(pr, v_ref[pl.ds(k0, tk), :], preferred_element_type=F32)

    m_sc[...] = jnp.full(m_sc.shape, NEG_BIG, F32)
    l_sc[...] = jnp.zeros(l_sc.shape, F32)
    acc_sc[...] = jnp.zeros(acc_sc.shape, F32)
    p_sc[...] = jnp.zeros(p_sc.shape, BF16)
    al_sc[...] = jnp.ones(al_sc.shape, F32)
    s_sc[...] = qk(0)

    def body(j, carry):
        pv(jnp.maximum(j - 1, 0), p_sc[...], al_sc[...])
        softmax(False)
        s_sc[...] = qk(j + 1)
        return carry

    lax.fori_loop(0, n_full, body, 0)

    pv(jnp.maximum(n_full - 1, 0), p_sc[...], al_sc[...])
    softmax(True)
    pv(n_full, p_sc[...], al_sc[...])
    o = acc_sc[...] / l_sc[...]
    o_ref[...] = jnp.where(first, o[:tq], o[tq:])


def _fox_attention(qkv, cum_t, tq, tk):
    bsz, seq, _ = qkv.shape
    tq = min(tq, seq)
    tk = min(max(tk, tq), seq)
    assert tk % tq == 0 and seq % tk == 0
    npair = FOX_HEADS // 2
    kern = functools.partial(_fox_kernel, tq=tq, tk=tk)
    return pl.pallas_call(
        kern,
        grid=(bsz, npair, seq // tq),
        in_specs=[
            pl.BlockSpec((None, tq, LANES), lambda b, p, i: (b, i, p)),
            pl.BlockSpec((None, seq, LANES), lambda b, p, i: (b, 0, npair + p)),
            pl.BlockSpec((None, seq, LANES), lambda b, p, i: (b, 0, 2 * npair + p)),
            pl.BlockSpec((None, 8, seq), lambda b, p, i: (b, 0, 0)),
        ],
        out_specs=pl.BlockSpec((None, tq, LANES), lambda b, p, i: (b, i, p)),
        out_shape=jax.ShapeDtypeStruct((bsz, seq, FOX_WIDTH), F32),
        scratch_shapes=[
            pltpu.VMEM((2 * tq, tk), F32),
            pltpu.VMEM((2 * tq, tk), BF16),
            pltpu.VMEM((2 * tq, LANES), F32),
            pltpu.VMEM((2 * tq, LANES), F32),
            pltpu.VMEM((2 * tq, LANES), F32),
            pltpu.VMEM((2 * tq, LANES), F32),
        ],
        compiler_params=_cparams(("arbitrary", "arbitrary", "arbitrary")),
        name="fox_attn",
    )(qkv, qkv, qkv, cum_t)


def _gelu_tanh(x):
    return 0.5 * x * (1.0 + jnp.tanh(math.sqrt(2.0 / math.pi) * (x + 0.044715 * (x * x * x))))


def _s5_kernel(su_ref, bm_ref, ar_ref, ai_ref, cm_ref, d_ref, gw_ref, gb_ref, o_ref,
               utb, bu, ytb, st):
    i = pl.program_id(0)
    nb = su_ref.shape[1]
    ln = su_ref.shape[2]
    ns = S5_NSTATE

    @pl.when(i == 0)
    def _():
        st[...] = jnp.zeros_like(st)

    for k in range(2):
        for b in range(nb):
            utb[k, pl.ds(b, ln, stride=nb), :] = su_ref[k, b]
    u = jnp.concatenate([utb[0], utb[1]], axis=1)
    bu[...] = jnp.dot(u.astype(BF16), bm_ref[...], preferred_element_type=F32)

    ar = ar_ref[...]
    ai = ai_ref[...]

    def body(t, carry):
        re, im = carry
        r0 = pl.multiple_of(t * nb, nb)
        bur = bu[pl.ds(r0, nb), 0:ns]
        bui = bu[pl.ds(r0, nb), ns:2 * ns]
        nre = ar * re - ai * im + bur
        nim = ar * im + ai * re + bui
        bu[pl.ds(r0, nb), 0:ns] = nre
        bu[pl.ds(r0, nb), ns:2 * ns] = nim
        return nre, nim

    re, im = lax.fori_loop(0, ln, body, (st[:, 0:ns], st[:, ns:2 * ns]))
    st[:, 0:ns] = re
    st[:, ns:2 * ns] = im

    y = jnp.dot(bu[...].astype(BF16), cm_ref[...], preferred_element_type=F32) + d_ref[...] * u
    y = _gelu_tanh(y)
    gate = _sigmoid(jnp.dot(y.astype(BF16), gw_ref[...], preferred_element_type=F32) + gb_ref[...])
    out = y * gate
    ytb[0] = out[:, :LANES]
    ytb[1] = out[:, LANES:]
    for k in range(2):
        for b in range(nb):
            o_ref[b, :, k * LANES:(k + 1) * LANES] = ytb[k, pl.ds(b, ln, stride=nb), :]


def _s5_mixer(su, bmat, ar, ai, cmat, dvec, glu_w, glu_b, ln):
    _, bsz, seq, _ = su.shape
    ln = min(ln, seq)
    rows = bsz * ln
    const = lambda shape: pl.BlockSpec(shape, lambda i: (0,) * len(shape))
    return pl.pallas_call(
        _s5_kernel,
        grid=(seq // ln,),
        in_specs=[
            pl.BlockSpec((2, bsz, ln, LANES), lambda i: (0, 0, i, 0)),
            const((S5_WIDTH, 2 * S5_NSTATE)),
            const((bsz, S5_NSTATE)),
            const((bsz, S5_NSTATE)),
            const((2 * S5_NSTATE, S5_WIDTH)),
            const((1, S5_WIDTH)),
            const((S5_WIDTH, S5_WIDTH)),
            const((1, S5_WIDTH)),
        ],
        out_specs=pl.BlockSpec((bsz, ln, S5_WIDTH), lambda i: (0, i, 0)),
        out_shape=jax.ShapeDtypeStruct((bsz, seq, S5_WIDTH), F32),
        scratch_shapes=[
            pltpu.VMEM((2, rows, LANES), F32),
            pltpu.VMEM((rows, 2 * S5_NSTATE), F32),
            pltpu.VMEM((2, rows, LANES), F32),
            pltpu.VMEM((bsz, 2 * S5_NSTATE), F32),
        ],
        compiler_params=_cparams(("arbitrary",)),
        name="s5_mixer",
    )(su, bmat, ar, ai, cmat, dvec, glu_w, glu_b)


def _block_row(a, blk, r):
    t = a.shape[0]
    if blk >= 8:
        a3 = a.reshape(t // blk, blk, LANES)
        return jnp.broadcast_to(a3[:, r:r + 1, :], a3.shape).reshape(t, LANES)
    row = lax.broadcasted_iota(jnp.int32, a.shape, 0)
    res = a
    for q in range(blk):
        if q != r:
            res = jnp.where(row % blk == q, pltpu.roll(a, (q - r) % t, axis=0), res)
    return res


def _hgrn_kernel(hq_ref, hi_ref, hg_ref, hf_ref, lb_ref, omlb_ref, nw_ref, o_ref, lvl_sc, mask_sc, *, tc):
    seq = hf_ref.shape[0]
    nlev = tc.bit_length() - 1
    lane = lax.broadcasted_iota(jnp.int32, (tc, LANES), 1)
    first = lane < HEAD_DIM
    row = lax.broadcasted_iota(jnp.int32, (tc, LANES), 0)
    rr = lax.broadcasted_iota(jnp.int32, (tc, tc), 0)
    cc = lax.broadcasted_iota(jnp.int32, (tc, tc), 1)
    diff = rr ^ cc
    bits = jnp.zeros((tc, tc), jnp.int32)
    for i in range(nlev):
        bits = bits + jnp.where((diff >> i) != 0, 1, 0)
    lvl_sc[...] = jnp.where(rr >= cc, bits, -1)
    for i in range(nlev):
        upper = ((row >> i) & 1) == 1
        mask_sc[i, 0] = jnp.where(jnp.logical_and(upper, first), 1.0, 0.0)
        mask_sc[i, 1] = jnp.where(jnp.logical_and(upper, jnp.logical_not(first)), 1.0, 0.0)
        mask_sc[i, 2] = jnp.where(upper, 0.0, 1.0)
    sr = lax.broadcasted_iota(jnp.int32, (LANES, LANES), 0)
    sc = lax.broadcasted_iota(jnp.int32, (LANES, LANES), 1)
    blockdiag = (sr < HEAD_DIM) == (sc < HEAD_DIM)
    lb = lb_ref[...]
    omlb = omlb_ref[...]
    nw = nw_ref[...]

    def chunk(c, st):
        r0 = pl.multiple_of(c * tc, tc)
        z = hf_ref[pl.ds(r0, tc), :]
        q = _silu(hq_ref[pl.ds(r0, tc), :].astype(F32))
        v = hi_ref[pl.ds(r0, tc), :]
        g = hg_ref[pl.ds(r0, tc), :].astype(F32)
        e = jnp.exp(-jnp.abs(z))
        s_big = 1.0 / (1.0 + e)
        s_small = e * s_big
        pos = z >= 0.0
        f = lb + omlb * jnp.where(pos, s_big, s_small)
        kk = omlb * jnp.where(pos, s_small, s_big)

        def scores(q_a, q_b, km):
            q2 = jnp.concatenate([q_a, q_b], axis=0).astype(BF16)
            return lax.dot_general(q2, km.astype(BF16), NT_DIMS, preferred_element_type=F32)

        s = scores(jnp.where(first, q, 0.0), jnp.where(first, 0.0, q), kk)
        on_diag = lvl_sc[...] == 0
        tot_a = jnp.where(on_diag, s[:tc], 0.0)
        tot_b = jnp.where(on_diag, s[tc:], 0.0)
        a_m = f
        b_m = jnp.ones_like(f)
        for i in range(nlev):
            m = 1 << i
            lo = mask_sc[i, 2]
            qa = q * a_m
            s = scores(qa * mask_sc[i, 0], qa * mask_sc[i, 1], kk * b_m * lo)
            sel = lvl_sc[...] == i + 1
            tot_a = jnp.where(sel, s[:tc], tot_a)
            tot_b = jnp.where(sel, s[tc:], tot_b)
            lower_total = _block_row(a_m, 2 * m, m - 1)
            upper_total = _block_row(a_m, 2 * m, 2 * m - 1)
            upper = lo < 0.5
            a_m = jnp.where(upper, a_m * lower_total, a_m)
            b_m = jnp.where(upper, b_m, b_m * upper_total)

        o = lax.dot_general((q * a_m).astype(BF16), st.astype(BF16), NT_DIMS, preferred_element_type=F32)
        pv = jnp.dot(jnp.concatenate([tot_a, tot_b], axis=0).astype(BF16), v, preferred_element_type=F32)
        o = o + jnp.where(first, pv[:tc], pv[tc:])

        upd = jnp.dot(v.astype(F32).T.astype(BF16), (kk * b_m).astype(BF16), preferred_element_type=F32)
        st_new = st * a_m[tc - 1:tc, :] + jnp.where(blockdiag, upd, 0.0)

        o2 = o * o
        s_a = jnp.sum(jnp.where(first, o2, 0.0), axis=-1, keepdims=True)
        s_b = jnp.sum(jnp.where(first, 0.0, o2), axis=-1, keepdims=True)
        ms = jnp.where(first, s_a, s_b) * (1.0 / HEAD_DIM)
        o_ref[pl.ds(r0, tc), :] = o * lax.rsqrt(ms + EPS) * nw * _silu(g)
        return st_new

    lax.fori_loop(0, seq // tc, chunk, jnp.zeros((LANES, LANES), F32), unroll=2)


def _hgrn_mixer(hqig, hf, lb, om_lb, nw2, tc):
    bsz, seq, _ = hf.shape
    tc = min(tc, seq)
    npair = HGRN_WIDTH // LANES
    kern = functools.partial(_hgrn_kernel, tc=tc)
    col = lambda off: pl.BlockSpec((None, seq, LANES), lambda b, p: (b, 0, off + p))
    par = pl.BlockSpec((1, LANES), lambda b, p: (0, p))
    return pl.pallas_call(
        kern,
        grid=(bsz, npair),
        in_specs=[col(0), col(npair), col(2 * npair), col(0), par, par,
                  pl.BlockSpec((1, LANES), lambda b, p: (0, 0))],
        out_specs=col(0),
        out_shape=jax.ShapeDtypeStruct((bsz, seq, HGRN_WIDTH), F32),
        scratch_shapes=[
            pltpu.VMEM((tc, tc), jnp.int32),
            pltpu.VMEM((tc.bit_length() - 1, 3, tc, LANES), F32),
        ],
        compiler_params=_cparams(("arbitrary", "arbitrary")),
        name="hgrn_mixer",
    )(hqig, hqig, hqig, hf, lb, om_lb, nw2)


def _outproj_kernel(fox_ref, s5_ref, hg_ref, x_ref, mod_ref, nf_ref, ns_ref, w_ref, o_ref):
    of = (_rms(fox_ref[...]) * nf_ref[...]).astype(BF16)
    os5 = (_rms(s5_ref[...]) * ns_ref[...]).astype(BF16)
    oh = hg_ref[...].astype(BF16)
    a, b = FOX_WIDTH, FOX_WIDTH + S5_WIDTH
    mix = jnp.dot(of, w_ref[0:a, :], preferred_element_type=F32)
    mix = mix + jnp.dot(os5, w_ref[a:b, :], preferred_element_type=F32)
    mix = mix + jnp.dot(oh, w_ref[b:, :], preferred_element_type=F32)
    o_ref[...] = x_ref[...] + mod_ref[2:3, :] * mix


def _out_proj(o_fox, o_s5, o_hgrn, x, mod_l, nf, ns, w_out, tm):
    bsz, seq, d = x.shape
    tm = min(tm, seq)
    row = lambda w: pl.BlockSpec((None, tm, w), lambda b, i: (b, i, 0))
    return pl.pallas_call(
        _outproj_kernel,
        grid=(bsz, seq // tm),
        in_specs=[
            row(FOX_WIDTH), row(S5_WIDTH), row(HGRN_WIDTH), row(d),
            pl.BlockSpec((None, 6, d), lambda b, i: (b, 0, 0)),
            pl.BlockSpec((1, FOX_WIDTH), lambda b, i: (0, 0)),
            pl.BlockSpec((1, S5_WIDTH), lambda b, i: (0, 0)),
            pl.BlockSpec((d, d), lambda b, i: (0, 0)),
        ],
        out_specs=row(d),
        out_shape=jax.ShapeDtypeStruct((bsz, seq, d), F32),
        compiler_params=_cparams(("arbitrary", "arbitrary")),
        name="out_proj",
    )(o_fox, o_s5, o_hgrn, x, mod_l, nf, ns, w_out)


def _routing(logits_t):
    mx = jnp.max(logits_t, axis=0, keepdims=True)
    ex = jnp.exp(logits_t - mx)
    probs = ex / jnp.sum(ex, axis=0, keepdims=True)
    p = [probs[e:e + 1, :] for e in range(N_EXPERTS)]
    epg = N_EXPERTS // N_GROUPS
    scores = []
    for g in range(N_GROUPS):
        a, b, c, d = p[epg * g:epg * g + epg]
        hi1, lo1 = jnp.maximum(a, b), jnp.minimum(a, b)
        hi2, lo2 = jnp.maximum(c, d), jnp.minimum(c, d)
        top1 = jnp.maximum(hi1, hi2)
        top2 = jnp.maximum(jnp.minimum(hi1, hi2), jnp.maximum(lo1, lo2))
        scores.append(top1 + top2)
    best = jnp.zeros_like(scores[0], dtype=jnp.int32)
    bs = scores[0]
    for g in range(1, N_GROUPS):
        upd = scores[g] > bs
        best = jnp.where(upd, g, best)
        bs = jnp.where(upd, scores[g], bs)
    ig = []
    for j in range(epg):
        val = p[j]
        for g in range(1, N_GROUPS):
            val = jnp.where(best == g, p[epg * g + j], val)
        ig.append(val)
    i1 = jnp.zeros_like(best)
    w1 = ig[0]
    for j in range(1, epg):
        upd = ig[j] > w1
        i1 = jnp.where(upd, j, i1)
        w1 = jnp.where(upd, ig[j], w1)
    i2 = jnp.zeros_like(best)
    w2 = jnp.full_like(w1, -1.0)
    for j in range(epg):
        upd = jnp.logical_and(i1 != j, ig[j] > w2)
        i2 = jnp.where(upd, j, i2)
        w2 = jnp.where(upd, ig[j], w2)
    den = w1 + w2
    tw1 = w1 / den
    tw2 = w2 / den
    e1 = best * epg + i1
    e2 = best * epg + i2
    return [jnp.where(e1 == e, tw1, 0.0) + jnp.where(e2 == e, tw2, 0.0) for e in range(N_EXPERTS)]


def _moe_kernel(x_ref, mod_ref, nw_ref, rw_ref, rb_ref, wg_ref, wu_ref, wd_ref, fnw_ref,
                o_ref, h_sc, comb_sc, combt_sc, acc_sc, *, final_norm):
    grp = pl.program_id(2)
    tm = x_ref.shape[0]
    epg = N_EXPERTS // N_GROUPS

    @pl.when(grp == 0)
    def _():
        h = _rms(x_ref[...]) * nw_ref[...]
        h = h * (1.0 + mod_ref[4:5, :]) + mod_ref[3:4, :]
        hb = h.astype(BF16)
        h_sc[...] = hb
        h_lo = (h - hb.astype(F32)).astype(BF16)
        hi_part = jnp.dot(hb, rw_ref[...], preferred_element_type=F32)
        lo_part = jnp.dot(h_lo, rw_ref[:, 0:LANES], preferred_element_type=F32)
        logits = hi_part[:, 0:LANES] + hi_part[:, LANES:] + lo_part + rb_ref[...]
        rows = _routing(logits.T[0:N_EXPERTS, :])
        combt_sc[...] = jnp.zeros_like(combt_sc)
        for j in range(N_EXPERTS):
            combt_sc[j:j + 1, :] = rows[j]
        comb_sc[...] = combt_sc[...].T
        acc_sc[...] = jnp.zeros_like(acc_sc)

    hb = h_sc[...]
    lane = lax.broadcasted_iota(jnp.int32, (tm, LANES), 1)
    comb = comb_sc[...]
    hid = []
    for j in range(epg):
        gate = jnp.dot(hb, wg_ref[j], preferred_element_type=F32)
        up = jnp.dot(hb, wu_ref[j], preferred_element_type=F32)
        w_e = jnp.sum(jnp.where(lane == grp * epg + j, comb, 0.0), axis=-1, keepdims=True)
        hid.append((_silu(gate) * up * w_e).astype(BF16))
    wd = wd_ref[...].reshape(epg * D_FF, wd_ref.shape[-1])
    acc_sc[...] += jnp.dot(jnp.concatenate(hid, axis=1), wd, preferred_element_type=F32)

    @pl.when(grp == N_GROUPS - 1)
    def _():
        y = x_ref[...] + mod_ref[5:6, :] * acc_sc[...]
        if final_norm:
            y = _rms(y) * fnw_ref[...]
        o_ref[...] = y


def _moe(x, mod_l, norm_w, rw_pad, rb_pad, wg, wu, wd, fnw, tm, final_norm):
    bsz, seq, d = x.shape
    tm = min(tm, seq)
    epg = N_EXPERTS // N_GROUPS
    kern = functools.partial(_moe_kernel, final_norm=final_norm)
    return pl.pallas_call(
        kern,
        grid=(bsz, seq // tm, N_GROUPS),
        in_specs=[
            pl.BlockSpec((None, tm, d), lambda b, i, e: (b, i, 0)),
            pl.BlockSpec((None, 6, d), lambda b, i, e: (b, 0, 0)),
            pl.BlockSpec((1, d), lambda b, i, e: (0, 0)),
            pl.BlockSpec((d, 2 * LANES), lambda b, i, e: (0, 0)),
            pl.BlockSpec((1, LANES), lambda b, i, e: (0, 0)),
            pl.BlockSpec((epg, d, D_FF), lambda b, i, e: (e, 0, 0)),
            pl.BlockSpec((epg, d, D_FF), lambda b, i, e: (e, 0, 0)),
            pl.BlockSpec((epg, D_FF, d), lambda b, i, e: (e, 0, 0)),
            pl.BlockSpec((1, d), lambda b, i, e: (0, 0)),
        ],
        out_specs=pl.BlockSpec((None, tm, d), lambda b, i, e: (b, i, 0)),
        out_shape=jax.ShapeDtypeStruct((bsz, seq, d), F32),
        scratch_shapes=[
            pltpu.VMEM((tm, d), BF16),
            pltpu.VMEM((tm, LANES), F32),
            pltpu.VMEM((LANES, tm), F32),
            pltpu.VMEM((tm, d), F32),
        ],
        compiler_params=_cparams(("arbitrary", "arbitrary", "arbitrary")),
        name="moe",
    )(x, mod_l, norm_w, rw_pad, rb_pad, wg, wu, wd, fnw)


def _prep_w_in(w_in_l, f_bias_l):
    a = FOX_WIDTH
    o_ff = 3 * a
    o_su = o_ff + FOX_HEADS
    o_hq = o_su + S5_WIDTH
    hw = HGRN_WIDTH
    fq = w_in_l[:, 0:a] * (HEAD_DIM ** -0.5 * LOG2E)
    fk, fv = w_in_l[:, a:2 * a], w_in_l[:, 2 * a:3 * a]
    ff = jnp.pad(w_in_l[:, o_ff:o_su], ((0, 0), (0, LANES - FOX_HEADS)))
    su = w_in_l[:, o_su:o_hq]
    hq = w_in_l[:, o_hq:o_hq + hw]
    hf = w_in_l[:, o_hq + hw:o_hq + 2 * hw]
    hi = w_in_l[:, o_hq + 2 * hw:o_hq + 3 * hw]
    hg = w_in_l[:, o_hq + 3 * hw:o_hq + 4 * hw]
    w = jnp.concatenate([fq, fk, fv, su, hq, hi, hg, hf, ff], axis=1).astype(BF16)
    fb = jnp.pad(f_bias_l.astype(F32), (0, LANES - FOX_HEADS)).reshape(1, LANES)
    return w, fb


def _prep_s5(a_re, a_im, b_re, b_im, c_re, c_im, log_dt, bsz):
    lam = lax.complex(a_re.astype(F32), a_im.astype(F32))
    dt = jnp.exp(log_dt.astype(F32))[:, None]
    a_bar = jnp.exp(lam * dt)
    b_bar = ((a_bar - 1.0) / lam)[..., None] * lax.complex(b_re.astype(F32), b_im.astype(F32))
    eye = jnp.eye(S5_GROUPS, dtype=F32)
    bm_re = jnp.einsum('gpc,gh->gchp', jnp.real(b_bar), eye).reshape(S5_WIDTH, S5_NSTATE)
    bm_im = jnp.einsum('gpc,gh->gchp', jnp.imag(b_bar), eye).reshape(S5_WIDTH, S5_NSTATE)
    bmat = jnp.concatenate([bm_re, bm_im], axis=1).astype(BF16)
    cm_re = jnp.einsum('gcp,gh->gphc', c_re.astype(F32), eye).reshape(S5_NSTATE, S5_WIDTH)
    cm_im = jnp.einsum('gcp,gh->gphc', c_im.astype(F32), eye).reshape(S5_NSTATE, S5_WIDTH)
    cmat = jnp.concatenate([cm_re, -cm_im], axis=0).astype(BF16)
    ar = jnp.broadcast_to(jnp.real(a_bar).reshape(1, S5_NSTATE), (bsz, S5_NSTATE))
    ai = jnp.broadcast_to(jnp.imag(a_bar).reshape(1, S5_NSTATE), (bsz, S5_NSTATE))
    return bmat, ar, ai, cmat


def kernel(x, c, ada_w, ada_b, norm_mix_w, norm_ffn_w, w_in, fox_f_bias, s5_a_re, s5_a_im, s5_b_re, s5_b_im, s5_c_re, s5_c_im, s5_d, s5_log_dt, s5_glu_w, s5_glu_b, hgrn_lb_logits, hgrn_norm_w, branch_norm_fox, branch_norm_s5, w_out, router_w, router_b, moe_w_gate, moe_w_up, moe_w_down, final_norm_w):
    bsz, seq, d = x.shape
    depth = w_in.shape[0]
    lb_cum = jnp.cumsum(jax.nn.softmax(hgrn_lb_logits.astype(F32), axis=0), axis=0)
    lower = lb_cum - lb_cum[0:1]
    om_lb = 1.0 - lower

    mod = _ada_mod(c, ada_w, ada_b)
    rw32 = jnp.pad(router_w.astype(F32), ((0, 0), (0, LANES - N_EXPERTS)))
    rw_hi = rw32.astype(BF16)
    rw_pad = jnp.concatenate([rw_hi, (rw32 - rw_hi.astype(F32)).astype(BF16)], axis=1)
    rb_pad = jnp.pad(router_b.astype(F32), (0, LANES - N_EXPERTS)).reshape(1, LANES)
    fnw = final_norm_w.reshape(1, d).astype(F32)

    for l in range(depth):
        w_pad, fb_pad = _prep_w_in(w_in[l], fox_f_bias[l])
        qkv, su, hqig, hf, cum_t = _in_proj(x, mod[l], norm_mix_w[l].reshape(1, d), w_pad, fb_pad, tm=512)
        o_fox = _fox_attention(qkv, cum_t, tq=256, tk=512)
        bmat, ar, ai, cmat = _prep_s5(s5_a_re[l], s5_a_im[l], s5_b_re[l], s5_b_im[l],
                                      s5_c_re[l], s5_c_im[l], s5_log_dt[l], bsz)
        o_s5 = _s5_mixer(su, bmat, ar, ai, cmat, s5_d[l].reshape(1, S5_WIDTH).astype(F32),
                         s5_glu_w[l].astype(BF16), s5_glu_b[l].reshape(1, S5_WIDTH).astype(F32), ln=64)
        nw2 = jnp.tile(hgrn_norm_w[l].astype(F32), 2).reshape(1, LANES)
        o_hgrn = _hgrn_mixer(hqig, hf, lower[l].reshape(1, -1), om_lb[l].reshape(1, -1), nw2, tc=128)
        x = _out_proj(o_fox, o_s5, o_hgrn, x, mod[l],
                      branch_norm_fox[l].reshape(1, -1).astype(F32),
                      branch_norm_s5[l].reshape(1, -1).astype(F32),
                      w_out[l].astype(BF16), tm=512)
        x = _moe(x, mod[l], norm_ffn_w[l].reshape(1, d).astype(F32), rw_pad, rb_pad,
                 moe_w_gate[l].astype(BF16), moe_w_up[l].astype(BF16), moe_w_down[l].astype(BF16),
                 fnw, tm=1024, final_norm=(l == depth - 1))
    return x
```

```python
import functools
import math

import jax
import jax.numpy as jnp
from jax import lax
from jax.experimental import pallas as pl
from jax.experimental.pallas import tpu as pltpu

F32 = jnp.float32
BF16 = jnp.bfloat16
HIGHEST = lax.Precision.HIGHEST

D_MODEL = 1024
FOX_HEADS = 6
HEAD_DIM = 64
FOX_WIDTH = 384
S5_WIDTH = 256
S5_GROUPS = 16
S5_CH = 16
S5_STATE = 64
S5_NSTATE = S5_GROUPS * S5_STATE
HGRN_WIDTH = 384
N_EXPERTS = 16
N_GROUPS = 4
D_FF = 256
EPS = 1e-6
LANES = 128
NEG_BIG = -1e30
LOG2E = math.log2(math.e)
SOFTMAX_ROWS = 64
MOE_SUB = 256
MOE_CAP = 96

C_QKV = (0, 1152)
C_SU = (1152, 1408)
C_HQIG = (1408, 2560)
C_HF = (2560, 2944)
C_FF = (2944, 3072)
N_IN_PAD = 3072

NT_DIMS = (((1,), (1,)), ((), ()))

VMEM_LIMIT = 56 * 1024 * 1024


def _cparams(sem):
    return pltpu.CompilerParams(dimension_semantics=sem, vmem_limit_bytes=VMEM_LIMIT)


def _sigmoid(x):
    return 1.0 / (1.0 + jnp.exp(-x))


def _silu(x):
    return x * _sigmoid(x)


def _log_sigmoid(x):
    return jnp.minimum(x, 0.0) - jnp.log1p(jnp.exp(-jnp.abs(x)))


def _rms(x):
    return x * lax.rsqrt(jnp.mean(x * x, axis=-1, keepdims=True) + EPS)


def _ada_kernel(c_ref, w_ref, b_ref, o_ref):
    c = c_ref[...]
    o_ref[...] = jnp.dot(_silu(c), w_ref[...], precision=HIGHEST,
                         preferred_element_type=F32) + b_ref[...]


def _ada_mod(c, ada_w, ada_b):
    depth, d, n6 = ada_w.shape
    bsz = c.shape[0]
    nblk = n6 // d
    out = pl.pallas_call(
        _ada_kernel,
        grid=(depth, nblk),
        in_specs=[
            pl.BlockSpec((bsz, d), lambda l, j: (0, 0)),
            pl.BlockSpec((None, d, d), lambda l, j: (l, 0, j)),
            pl.BlockSpec((None, 1, d), lambda l, j: (l, 0, j)),
        ],
        out_specs=pl.BlockSpec((None, bsz, d), lambda l, j: (l, 0, j)),
        out_shape=jax.ShapeDtypeStruct((depth, bsz, n6), F32),
        compiler_params=_cparams(("arbitrary", "arbitrary")),
        name="ada_mod",
    )(c, ada_w, ada_b.reshape(depth, 1, n6))
    return out.reshape(depth, bsz, nblk, d)


def _inproj_kernel(x_ref, mod_ref, nw_ref, w_ref, fb_ref,
                   qkv_ref, su_ref, hqig_ref, hf_ref, cum_ref, carry_ref):
    i = pl.program_id(1)
    tm = x_ref.shape[0]

    @pl.when(i == 0)
    def _():
        carry_ref[...] = jnp.zeros_like(carry_ref)

    h = _rms(x_ref[...]) * nw_ref[...]
    h = h * (1.0 + mod_ref[1:2, :]) + mod_ref[0:1, :]
    hb = h.astype(BF16)

    def proj(c):
        return jnp.dot(hb, w_ref[:, c[0]:c[1]], preferred_element_type=F32)

    qkv_ref[...] = proj(C_QKV).astype(BF16)
    su = proj(C_SU)
    su_ref[0] = su[:, :LANES]
    su_ref[1] = su[:, LANES:]
    hqig_ref[...] = proj(C_HQIG).astype(BF16)
    hf_ref[...] = proj(C_HF)

    lf = _log_sigmoid(proj(C_FF) + fb_ref[...])
    lf_t = lf.T[0:8, :]
    r = lax.broadcasted_iota(jnp.int32, (tm, tm), 0)
    c = lax.broadcasted_iota(jnp.int32, (tm, tm), 1)
    tri_u = jnp.where(r <= c, 1.0, 0.0).astype(F32)
    cum = jnp.dot(lf_t, tri_u, precision=HIGHEST, preferred_element_type=F32) + carry_ref[:, 0:1]
    cum_ref[...] = cum * LOG2E
    carry_ref[...] = jnp.broadcast_to(cum[:, tm - 1:tm], carry_ref.shape)


def _in_proj(x, mod_l, norm_w, w_pad, fb_pad, tm):
    bsz, seq, d = x.shape
    tm = min(tm, seq)
    return pl.pallas_call(
        _inproj_kernel,
        grid=(bsz, seq // tm),
        in_specs=[
            pl.BlockSpec((None, tm, d), lambda b, i: (b, i, 0)),
            pl.BlockSpec((None, 6, d), lambda b, i: (b, 0, 0)),
            pl.BlockSpec((1, d), lambda b, i: (0, 0)),
            pl.BlockSpec((d, N_IN_PAD), lambda b, i: (0, 0)),
            pl.BlockSpec((1, LANES), lambda b, i: (0, 0)),
        ],
        out_specs=[
            pl.BlockSpec((None, tm, 1152), lambda b, i: (b, i, 0)),
            pl.BlockSpec((2, None, tm, LANES), lambda b, i: (0, b, i, 0)),
            pl.BlockSpec((None, tm, 1152), lambda b, i: (b, i, 0)),
            pl.BlockSpec((None, tm, HGRN_WIDTH), lambda b, i: (b, i, 0)),
            pl.BlockSpec((None, 8, tm), lambda b, i: (b, 0, i)),
        ],
        out_shape=[
            jax.ShapeDtypeStruct((bsz, seq, 1152), BF16),
            jax.ShapeDtypeStruct((2, bsz, seq, LANES), F32),
            jax.ShapeDtypeStruct((bsz, seq, 1152), BF16),
            jax.ShapeDtypeStruct((bsz, seq, HGRN_WIDTH), F32),
            jax.ShapeDtypeStruct((bsz, 8, seq), F32),
        ],
        scratch_shapes=[pltpu.VMEM((8, LANES), F32)],
        compiler_params=_cparams(("arbitrary", "arbitrary")),
        name="in_proj",
    )(x, mod_l, norm_w, w_pad, fb_pad)


def _fox_kernel(q_ref, k_ref, v_ref, cum_ref, o_ref, s_sc, p_sc, al_sc, m_sc, l_sc, acc_sc, *, tq, tk):
    p = pl.program_id(1)
    qi = pl.program_id(2)
    q = q_ref[...]
    lane = lax.broadcasted_iota(jnp.int32, (tq, LANES), 1)
    first = lane < HEAD_DIM
    zero = jnp.zeros_like(q)
    q2 = jnp.concatenate([jnp.where(first, q, zero), jnp.where(first, zero, q)], axis=0)
    n_full = (qi * tq) // tk

    def qk(j):
        k0 = pl.multiple_of(j * tk, tk)
        s = lax.dot_general(q2, k_ref[pl.ds(k0, tk), :], NT_DIMS, preferred_element_type=F32)
        bias_a = cum_ref[pl.ds(2 * p, 1), pl.ds(k0, tk)]
        bias_b = cum_ref[pl.ds(2 * p + 1, 1), pl.ds(k0, tk)]
        return jnp.concatenate([s[:tq] - bias_a, s[tq:] - bias_b], axis=0)

    def softmax(masked):
        for r0 in range(0, 2 * tq, SOFTMAX_ROWS):
            rs = slice(r0, r0 + SOFTMAX_ROWS)
            s = s_sc[rs, :]
            if masked:
                rows = lax.broadcasted_iota(jnp.int32, s.shape, 0) + (qi * tq + r0 % tq)
                cols = lax.broadcasted_iota(jnp.int32, s.shape, 1) + n_full * tk
                s = jnp.where(cols <= rows, s, NEG_BIG)
            m_prev = m_sc[rs, :]
            m_next = jnp.maximum(m_prev, jnp.max(s, axis=1, keepdims=True))
            pr = jnp.exp2(s - jnp.tile(m_next, (1, tk // LANES)))
            alpha = jnp.exp2(m_prev - m_next)
            l_sc[rs, :] = alpha * l_sc[rs, :] + jnp.sum(pr, axis=1, keepdims=True)
            m_sc[rs, :] = m_next
            p_sc[rs, :] = pr.astype(BF16)
            al_sc[rs, :] = alpha

    def pv(j, pr, alpha):
        k0 = pl.multiple_of(j * tk, tk)
        acc_sc[...] = acc_sc[...] * alpha + jnp.dot(pr, v_ref[pl.ds(k0, tk), :], preferred_element_type=F32)

    m_sc[...] = jnp.full(m_sc.shape, NEG_BIG, F32)
    l_sc[...] = jnp.zeros(l_sc.shape, F32)
    acc_sc[...] = jnp.zeros(acc_sc.shape, F32)
    p_sc[...] = jnp.zeros(p_sc.shape, BF16)
    al_sc[...] = jnp.ones(al_sc.shape, F32)
    s_sc[...] = qk(0)

    def body(j, carry):
        pv(jnp.maximum(j - 1, 0), p_sc[...], al_sc[...])
        softmax(False)
        s_sc[...] = qk(j + 1)
        return carry

    lax.fori_loop(0, n_full, body, 0)

    pv(jnp.maximum(n_full - 1, 0), p_sc[...], al_sc[...])
    softmax(True)
    pv(n_full, p_sc[...], al_sc[...])
    o = acc_sc[...] / l_sc[...]
    o_ref[...] = jnp.where(first, o[:tq], o[tq:])


def _fox_attention(qkv, cum_t, tq, tk):
    bsz, seq, _ = qkv.shape
    tq = min(tq, seq)
    tk = min(max(tk, tq), seq)
    assert tk % tq == 0 and seq % tk == 0
    npair = FOX_HEADS // 2
    kern = functools.partial(_fox_kernel, tq=tq, tk=tk)
    return pl.pallas_call(
        kern,
        grid=(bsz, npair, seq // tq),
        in_specs=[
            pl.BlockSpec((None, tq, LANES), lambda b, p, i: (b, i, p)),
            pl.BlockSpec((None, seq, LANES), lambda b, p, i: (b, 0, npair + p)),
            pl.BlockSpec((None, seq, LANES), lambda b, p, i: (b, 0, 2 * npair + p)),
            pl.BlockSpec((None, 8, seq), lambda b, p, i: (b, 0, 0)),
        ],
        out_specs=pl.BlockSpec((None, tq, LANES), lambda b, p, i: (b, i, p)),
        out_shape=jax.ShapeDtypeStruct((bsz, seq, FOX_WIDTH), F32),
        scratch_shapes=[
            pltpu.VMEM((2 * tq, tk), F32),
            pltpu.VMEM((2 * tq, tk), BF16),
            pltpu.VMEM((2 * tq, LANES), F32),
            pltpu.VMEM((2 * tq, LANES), F32),
            pltpu.VMEM((2 * tq, LANES), F32),
            pltpu.VMEM((2 * tq, LANES), F32),
        ],
        compiler_params=_cparams(("arbitrary", "arbitrary", "arbitrary")),
        name="fox_attn",
    )(qkv, qkv, qkv, cum_t)


def _gelu_tanh(x):
    return 0.5 * x * (1.0 + jnp.tanh(math.sqrt(2.0 / math.pi) * (x + 0.044715 * (x * x * x))))


def _s5_kernel(su_ref, bm_ref, ar_ref, ai_ref, cm_ref, d_ref, gw_ref, gb_ref, o_ref,
               utb, bu, ytb, st):
    i = pl.program_id(0)
    nb = su_ref.shape[1]
    ln = su_ref.shape[2]
    ns = S5_NSTATE

    @pl.when(i == 0)
    def _():
        st[...] = jnp.zeros_like(st)

    for k in range(2):
        for b in range(nb):
            utb[k, pl.ds(b, ln, stride=nb), :] = su_ref[k, b]
    u = jnp.concatenate([utb[0], utb[1]], axis=1)
    bu[...] = jnp.dot(u.astype(BF16), bm_ref[...], preferred_element_type=F32)

    ar = ar_ref[...]
    ai = ai_ref[...]

    def body(t, carry):
        re, im = carry
        r0 = pl.multiple_of(t * nb, nb)
        bur = bu[pl.ds(r0, nb), 0:ns]
        bui = bu[pl.ds(r0, nb), ns:2 * ns]
        nre = ar * re - ai * im + bur
        nim = ar * im + ai * re + bui
        bu[pl.ds(r0, nb), 0:ns] = nre
        bu[pl.ds(r0, nb), ns:2 * ns] = nim
        return nre, nim

    re, im = lax.fori_loop(0, ln, body, (st[:, 0:ns], st[:, ns:2 * ns]))
    st[:, 0:ns] = re
    st[:, ns:2 * ns] = im

    y = jnp.dot(bu[...].astype(BF16), cm_ref[...], preferred_element_type=F32) + d_ref[...] * u
    y = _gelu_tanh(y)
    gate = _sigmoid(jnp.dot(y.astype(BF16), gw_ref[...], preferred_element_type=F32) + gb_ref[...])
    out = y * gate
    ytb[0] = out[:, :LANES]
    ytb[1] = out[:, LANES:]
    for k in range(2):
        for b in range(nb):
            o_ref[b, :, k * LANES:(k + 1) * LANES] = ytb[k, pl.ds(b, ln, stride=nb), :]


def _s5_mixer(su, bmat, ar, ai, cmat, dvec, glu_w, glu_b, ln):
    _, bsz, seq, _ = su.shape
    ln = min(ln, seq)
    rows = bsz * ln
    const = lambda shape: pl.BlockSpec(shape, lambda i: (0,) * len(shape))
    return pl.pallas_call(
        _s5_kernel,
        grid=(seq // ln,),
        in_specs=[
            pl.BlockSpec((2, bsz, ln, LANES), lambda i: (0, 0, i, 0)),
            const((S5_WIDTH, 2 * S5_NSTATE)),
            const((bsz, S5_NSTATE)),
            const((bsz, S5_NSTATE)),
            const((2 * S5_NSTATE, S5_WIDTH)),
            const((1, S5_WIDTH)),
            const((S5_WIDTH, S5_WIDTH)),
            const((1, S5_WIDTH)),
        ],
        out_specs=pl.BlockSpec((bsz, ln, S5_WIDTH), lambda i: (0, i, 0)),
        out_shape=jax.ShapeDtypeStruct((bsz, seq, S5_WIDTH), F32),
        scratch_shapes=[
            pltpu.VMEM((2, rows, LANES), F32),
            pltpu.VMEM((rows, 2 * S5_NSTATE), F32),
            pltpu.VMEM((2, rows, LANES), F32),
            pltpu.VMEM((bsz, 2 * S5_NSTATE), F32),
        ],
        compiler_params=_cparams(("arbitrary",)),
        name="s5_mixer",
    )(su, bmat, ar, ai, cmat, dvec, glu_w, glu_b)


def _block_row(a, blk, r):
    t = a.shape[0]
    if blk >= 8:
        a3 = a.reshape(t // blk, blk, LANES)
        return jnp.broadcast_to(a3[:, r:r + 1, :], a3.shape).reshape(t, LANES)
    row = lax.broadcasted_iota(jnp.int32, a.shape, 0)
    res = a
    for q in range(blk):
        if q != r:
            res = jnp.where(row % blk == q, pltpu.roll(a, (q - r) % t, axis=0), res)
    return res


def _hgrn_kernel(hq_ref, hi_ref, hg_ref, hf_ref, lb_ref, omlb_ref, nw_ref, o_ref, lvl_sc, mask_sc, *, tc):
    seq = hf_ref.shape[0]
    nlev = tc.bit_length() - 1
    lane = lax.broadcasted_iota(jnp.int32, (tc, LANES), 1)
    first = lane < HEAD_DIM
    row = lax.broadcasted_iota(jnp.int32, (tc, LANES), 0)
    rr = lax.broadcasted_iota(jnp.int32, (tc, tc), 0)
    cc = lax.broadcasted_iota(jnp.int32, (tc, tc), 1)
    diff = rr ^ cc
    bits = jnp.zeros((tc, tc), jnp.int32)
    for i in range(nlev):
        bits = bits + jnp.where((diff >> i) != 0, 1, 0)
    lvl_sc[...] = jnp.where(rr >= cc, bits, -1)
    for i in range(nlev):
        upper = ((row >> i) & 1) == 1
        mask_sc[i, 0] = jnp.where(jnp.logical_and(upper, first), 1.0, 0.0)
        mask_sc[i, 1] = jnp.where(jnp.logical_and(upper, jnp.logical_not(first)), 1.0, 0.0)
        mask_sc[i, 2] = jnp.where(upper, 0.0, 1.0)
    sr = lax.broadcasted_iota(jnp.int32, (LANES, LANES), 0)
    sc = lax.broadcasted_iota(jnp.int32, (LANES, LANES), 1)
    blockdiag = (sr < HEAD_DIM) == (sc < HEAD_DIM)
    lb = lb_ref[...]
    omlb = omlb_ref[...]
    nw = nw_ref[...]

    def chunk(c, st):
        r0 = pl.multiple_of(c * tc, tc)
        z = hf_ref[pl.ds(r0, tc), :]
        q = _silu(hq_ref[pl.ds(r0, tc), :].astype(F32))
        v = hi_ref[pl.ds(r0, tc), :]
        g = hg_ref[pl.ds(r0, tc), :].astype(F32)
        e = jnp.exp(-jnp.abs(z))
        s_big = 1.0 / (1.0 + e)
        s_small = e * s_big
        pos = z >= 0.0
        f = lb + omlb * jnp.where(pos, s_big, s_small)
        kk = omlb * jnp.where(pos, s_small, s_big)

        def scores(q_a, q_b, km):
            q2 = jnp.concatenate([q_a, q_b], axis=0).astype(BF16)
            return lax.dot_general(q2, km.astype(BF16), NT_DIMS, preferred_element_type=F32)

        s = scores(jnp.where(first, q, 0.0), jnp.where(first, 0.0, q), kk)
        on_diag = lvl_sc[...] == 0
        tot_a = jnp.where(on_diag, s[:tc], 0.0)
        tot_b = jnp.where(on_diag, s[tc:], 0.0)
        a_m = f
        b_m = jnp.ones_like(f)
        for i in range(nlev):
            m = 1 << i
            lo = mask_sc[i, 2]
            qa = q * a_m
            s = scores(qa * mask_sc[i, 0], qa * mask_sc[i, 1], kk * b_m * lo)
            sel = lvl_sc[...] == i + 1
            tot_a = jnp.where(sel, s[:tc], tot_a)
            tot_b = jnp.where(sel, s[tc:], tot_b)
            lower_total = _block_row(a_m, 2 * m, m - 1)
            upper_total = _block_row(a_m, 2 * m, 2 * m - 1)
            upper = lo < 0.5
            a_m = jnp.where(upper, a_m * lower_total, a_m)
            b_m = jnp.where(upper, b_m, b_m * upper_total)

        o = lax.dot_general((q * a_m).astype(BF16), st.astype(BF16), NT_DIMS, preferred_element_type=F32)
        pv = jnp.dot(jnp.concatenate([tot_a, tot_b], axis=0).astype(BF16), v, preferred_element_type=F32)
        o = o + jnp.where(first, pv[:tc], pv[tc:])

        upd = jnp.dot(v.astype(F32).T.astype(BF16), (kk * b_m).astype(BF16), preferred_element_type=F32)
        st_new = st * a_m[tc - 1:tc, :] + jnp.where(blockdiag, upd, 0.0)

        o2 = o * o
        s_a = jnp.sum(jnp.where(first, o2, 0.0), axis=-1, keepdims=True)
        s_b = jnp.sum(jnp.where(first, 0.0, o2), axis=-1, keepdims=True)
        ms = jnp.where(first, s_a, s_b) * (1.0 / HEAD_DIM)
        o_ref[pl.ds(r0, tc), :] = o * lax.rsqrt(ms + EPS) * nw * _silu(g)
        return st_new

    lax.fori_loop(0, seq // tc, chunk, jnp.zeros((LANES, LANES), F32), unroll=2)


def _hgrn_mixer(hqig, hf, lb, om_lb, nw2, tc):
    bsz, seq, _ = hf.shape
    tc = min(tc, seq)
    npair = HGRN_WIDTH // LANES
    kern = functools.partial(_hgrn_kernel, tc=tc)
    col = lambda off: pl.BlockSpec((None, seq, LANES), lambda b, p: (b, 0, off + p))
    par = pl.BlockSpec((1, LANES), lambda b, p: (0, p))
    return pl.pallas_call(
        kern,
        grid=(bsz, npair),
        in_specs=[col(0), col(npair), col(2 * npair), col(0), par, par,
                  pl.BlockSpec((1, LANES), lambda b, p: (0, 0))],
        out_specs=col(0),
        out_shape=jax.ShapeDtypeStruct((bsz, seq, HGRN_WIDTH), F32),
        scratch_shapes=[
            pltpu.VMEM((tc, tc), jnp.int32),
            pltpu.VMEM((tc.bit_length() - 1, 3, tc, LANES), F32),
        ],
        compiler_params=_cparams(("arbitrary", "arbitrary")),
        name="hgrn_mixer",
    )(hqig, hqig, hqig, hf, lb, om_lb, nw2)


def _outproj_kernel(fox_ref, s5_ref, hg_ref, x_ref, mod_ref, nf_ref, ns_ref, w_ref, o_ref):
    of = (_rms(fox_ref[...]) * nf_ref[...]).astype(BF16)
    os5 = (_rms(s5_ref[...]) * ns_ref[...]).astype(BF16)
    oh = hg_ref[...].astype(BF16)
    a, b = FOX_WIDTH, FOX_WIDTH + S5_WIDTH
    mix = jnp.dot(of, w_ref[0:a, :], preferred_element_type=F32)
    mix = mix + jnp.dot(os5, w_ref[a:b, :], preferred_element_type=F32)
    mix = mix + jnp.dot(oh, w_ref[b:, :], preferred_element_type=F32)
    o_ref[...] = x_ref[...] + mod_ref[2:3, :] * mix


def _out_proj(o_fox, o_s5, o_hgrn, x, mod_l, nf, ns, w_out, tm):
    bsz, seq, d = x.shape
    tm = min(tm, seq)
    row = lambda w: pl.BlockSpec((None, tm, w), lambda b, i: (b, i, 0))
    return pl.pallas_call(
        _outproj_kernel,
        grid=(bsz, seq // tm),
        in_specs=[
            row(FOX_WIDTH), row(S5_WIDTH), row(HGRN_WIDTH), row(d),
            pl.BlockSpec((None, 6, d), lambda b, i: (b, 0, 0)),
            pl.BlockSpec((1, FOX_WIDTH), lambda b, i: (0, 0)),
            pl.BlockSpec((1, S5_WIDTH), lambda b, i: (0, 0)),
            pl.BlockSpec((d, d), lambda b, i: (0, 0)),
        ],
        out_specs=row(d),
        out_shape=jax.ShapeDtypeStruct((bsz, seq, d), F32),
        compiler_params=_cparams(("arbitrary", "arbitrary")),
        name="out_proj",
    )(o_fox, o_s5, o_hgrn, x, mod_l, nf, ns, w_out)


def _routing(logits_t):
    mx = jnp.max(logits_t, axis=0, keepdims=True)
    ex = jnp.exp(logits_t - mx)
    probs = ex / jnp.sum(ex, axis=0, keepdims=True)
    p = [probs[e:e + 1, :] for e in range(N_EXPERTS)]
    epg = N_EXPERTS // N_GROUPS
    scores = []
    for g in range(N_GROUPS):
        a, b, c, d = p[epg * g:epg * g + epg]
        hi1, lo1 = jnp.maximum(a, b), jnp.minimum(a, b)
        hi2, lo2 = jnp.maximum(c, d), jnp.minimum(c, d)
        top1 = jnp.maximum(hi1, hi2)
        top2 = jnp.maximum(jnp.minimum(hi1, hi2), jnp.maximum(lo1, lo2))
        scores.append(top1 + top2)
    best = jnp.zeros_like(scores[0], dtype=jnp.int32)
    bs = scores[0]
    for g in range(1, N_GROUPS):
        upd = scores[g] > bs
        best = jnp.where(upd, g, best)
        bs = jnp.where(upd, scores[g], bs)
    ig = []
    for j in range(epg):
        val = p[j]
        for g in range(1, N_GROUPS):
            val = jnp.where(best == g, p[epg * g + j], val)
        ig.append(val)
    i1 = jnp.zeros_like(best)
    w1 = ig[0]
    for j in range(1, epg):
        upd = ig[j] > w1
        i1 = jnp.where(upd, j, i1)
        w1 = jnp.where(upd, ig[j], w1)
    i2 = jnp.zeros_like(best)
    w2 = jnp.full_like(w1, -1.0)
    for j in range(epg):
        upd = jnp.logical_and(i1 != j, ig[j] > w2)
        i2 = jnp.where(upd, j, i2)
        w2 = jnp.where(upd, ig[j], w2)
    den = w1 + w2
    tw1 = w1 / den
    tw2 = w2 / den
    e1 = best * epg + i1
    e2 = best * epg + i2
    return [jnp.where(e1 == e, tw1, 0.0) + jnp.where(e2 == e, tw2, 0.0) for e in range(N_EXPERTS)], best


def _moe_kernel(x_ref, mod_ref, nw_ref, rw_ref, rb_ref, wg_ref, wu_ref, wd_ref, fnw_ref,
                o_ref, h_sc, comb_sc, combt_sc, hs_sc, ws_sc, ys_sc, acc_sc, cnt_sm, *, final_norm):
    grp = pl.program_id(2)
    tm, d = x_ref.shape
    epg = N_EXPERTS // N_GROUPS
    sub, cap = MOE_SUB, MOE_CAP
    ns = tm // sub
    gc = N_GROUPS * cap
    row_oh, row_slot = N_EXPERTS, N_EXPERTS + 8

    @pl.when(grp == 0)
    def _():
        h = _rms(x_ref[...]) * nw_ref[...]
        h = h * (1.0 + mod_ref[4:5, :]) + mod_ref[3:4, :]
        hb = h.astype(BF16)
        h_sc[...] = hb
        h_lo = (h - hb.astype(F32)).astype(BF16)
        hi_part = jnp.dot(hb, rw_ref[...], preferred_element_type=F32)
        lo_part = jnp.dot(h_lo, rw_ref[:, 0:LANES], preferred_element_type=F32)
        logits = hi_part[:, 0:LANES] + hi_part[:, LANES:] + lo_part + rb_ref[...]
        rows, best = _routing(logits.T[0:N_EXPERTS, :])
        combt_sc[...] = jnp.zeros_like(combt_sc)
        for j in range(N_EXPERTS):
            combt_sc[j:j + 1, :] = rows[j]
        for g in range(N_GROUPS):
            combt_sc[row_oh + g:row_oh + g + 1, :] = jnp.where(best == g, 1.0, 0.0)
        oh8 = combt_sc[row_oh:row_oh + 8, :]
        r = lax.broadcasted_iota(jnp.int32, (sub, sub), 0)
        c = lax.broadcasted_iota(jnp.int32, (sub, sub), 1)
        tri_u = jnp.where(r <= c, 1.0, 0.0).astype(BF16)
        incl = [jnp.dot(oh8[:, s * sub:(s + 1) * sub].astype(BF16), tri_u, preferred_element_type=F32)
                for s in range(ns)]
        cmax = incl[0][:, sub - 1:sub]
        for s in range(1, ns):
            cmax = jnp.maximum(cmax, incl[s][:, sub - 1:sub])
        for g in range(N_GROUPS):
            cnt_sm[g] = jnp.max(cmax[g:g + 1, :]).astype(jnp.int32)
        over = jnp.where(cmax > cap, 1.0, 0.0)
        best_f = best.astype(F32)
        for s in range(ns):
            cols = slice(s * sub, (s + 1) * sub)
            oh_s = oh8[:, cols]
            rank = jnp.sum(oh_s * incl[s], axis=0, keepdims=True) - 1.0
            dense = jnp.sum(oh_s * over, axis=0, keepdims=True)
            combt_sc[row_slot:row_slot + 1, cols] = jnp.where(dense > 0.5, -1.0, best_f[:, cols] * cap + rank)
        comb_sc[...] = combt_sc[...].T
        for s in range(ns):
            toks = slice(s * sub, (s + 1) * sub)
            slot_row = combt_sc[row_slot:row_slot + 1, toks].astype(jnp.int32)
            ri = lax.broadcasted_iota(jnp.int32, (gc, sub), 0)
            p = jnp.where(ri == slot_row, 1.0, 0.0).astype(BF16)
            hs_all = jnp.dot(p, h_sc[toks, :], preferred_element_type=F32).astype(BF16)
            cs = comb_sc[toks, :]
            c_hi = cs.astype(BF16)
            c_lo = (cs - c_hi.astype(F32)).astype(BF16)
            ws_all = jnp.dot(p, jnp.concatenate([c_hi, c_lo], axis=1), preferred_element_type=F32)
            ws_all = ws_all[:, 0:LANES] + ws_all[:, LANES:]
            for g in range(N_GROUPS):
                hs_sc[g, s * cap:(s + 1) * cap, :] = hs_all[g * cap:(g + 1) * cap, :]
                ws_sc[g, s * cap:(s + 1) * cap, :] = ws_all[g * cap:(g + 1) * cap, :]
        acc_sc[...] = jnp.zeros_like(acc_sc)

    def experts(h_rows, w_rows):
        lane = lax.broadcasted_iota(jnp.int32, w_rows.shape, 1)
        hid = []
        for j in range(epg):
            gate = jnp.dot(h_rows, wg_ref[j], preferred_element_type=F32)
            up = jnp.dot(h_rows, wu_ref[j], preferred_element_type=F32)
            w_e = jnp.sum(jnp.where(lane == grp * epg + j, w_rows, 0.0), axis=-1, keepdims=True)
            hid.append((_silu(gate) * up * w_e).astype(BF16))
        wd = wd_ref[...].reshape(epg * D_FF, d)
        return jnp.dot(jnp.concatenate(hid, axis=1), wd, preferred_element_type=F32)

    cnt = cnt_sm[grp]

    @pl.when(cnt <= cap)
    def _():
        ys = experts(hs_sc[grp], ws_sc[grp]).astype(BF16)
        for s in range(ns):
            ys_sc[s, grp] = ys[s * cap:(s + 1) * cap, :]

    @pl.when(cnt > cap)
    def _():
        acc_sc[...] += experts(h_sc[...], comb_sc[...])
        for s in range(ns):
            ys_sc[s, grp] = jnp.zeros((cap, d), BF16)

    @pl.when(grp == N_GROUPS - 1)
    def _():
        for s in range(ns):
            toks = slice(s * sub, (s + 1) * sub)
            slot_col = comb_sc[toks, row_slot:row_slot + 1].astype(jnp.int32)
            li = lax.broadcasted_iota(jnp.int32, (sub, gc), 1)
            pt = jnp.where(li == slot_col, 1.0, 0.0).astype(BF16)
            y = jnp.dot(pt, ys_sc[s].reshape(gc, d), preferred_element_type=F32) + acc_sc[toks, :]
            y = x_ref[toks, :] + mod_ref[5:6, :] * y
            if final_norm:
                y = _rms(y) * fnw_ref[...]
            o_ref[toks, :] = y


def _moe(x, mod_l, norm_w, rw_pad, rb_pad, wg, wu, wd, fnw, tm, final_norm):
    bsz, seq, d = x.shape
    tm = min(tm, seq)
    assert tm % MOE_SUB == 0
    ns = tm // MOE_SUB
    epg = N_EXPERTS // N_GROUPS
    kern = functools.partial(_moe_kernel, final_norm=final_norm)
    return pl.pallas_call(
        kern,
        grid=(bsz, seq // tm, N_GROUPS),
        in_specs=[
            pl.BlockSpec((None, tm, d), lambda b, i, e: (b, i, 0)),
            pl.BlockSpec((None, 6, d), lambda b, i, e: (b, 0, 0)),
            pl.BlockSpec((1, d), lambda b, i, e: (0, 0)),
            pl.BlockSpec((d, 2 * LANES), lambda b, i, e: (0, 0)),
            pl.BlockSpec((1, LANES), lambda b, i, e: (0, 0)),
            pl.BlockSpec((epg, d, D_FF), lambda b, i, e: (e, 0, 0)),
            pl.BlockSpec((epg, d, D_FF), lambda b, i, e: (e, 0, 0)),
            pl.BlockSpec((epg, D_FF, d), lambda b, i, e: (e, 0, 0)),
            pl.BlockSpec((1, d), lambda b, i, e: (0, 0)),
        ],
        out_specs=pl.BlockSpec((None, tm, d), lambda b, i, e: (b, i, 0)),
        out_shape=jax.ShapeDtypeStruct((bsz, seq, d), F32),
        scratch_shapes=[
            pltpu.VMEM((tm, d), BF16),
            pltpu.VMEM((tm, LANES), F32),
            pltpu.VMEM((LANES, tm), F32),
            pltpu.VMEM((N_GROUPS, ns * MOE_CAP, d), BF16),
            pltpu.VMEM((N_GROUPS, ns * MOE_CAP, LANES), F32),
            pltpu.VMEM((ns, N_GROUPS, MOE_CAP, d), BF16),
            pltpu.VMEM((tm, d), F32),
            pltpu.SMEM((N_GROUPS,), jnp.int32),
        ],
        compiler_params=_cparams(("arbitrary", "arbitrary", "arbitrary")),
        name="moe",
    )(x, mod_l, norm_w, rw_pad, rb_pad, wg, wu, wd, fnw)


def _prep_w_in(w_in_l, f_bias_l):
    a = FOX_WIDTH
    o_ff = 3 * a
    o_su = o_ff + FOX_HEADS
    o_hq = o_su + S5_WIDTH
    hw = HGRN_WIDTH
    fq = w_in_l[:, 0:a] * (HEAD_DIM ** -0.5 * LOG2E)
    fk, fv = w_in_l[:, a:2 * a], w_in_l[:, 2 * a:3 * a]
    ff = jnp.pad(w_in_l[:, o_ff:o_su], ((0, 0), (0, LANES - FOX_HEADS)))
    su = w_in_l[:, o_su:o_hq]
    hq = w_in_l[:, o_hq:o_hq + hw]
    hf = w_in_l[:, o_hq + hw:o_hq + 2 * hw]
    hi = w_in_l[:, o_hq + 2 * hw:o_hq + 3 * hw]
    hg = w_in_l[:, o_hq + 3 * hw:o_hq + 4 * hw]
    w = jnp.concatenate([fq, fk, fv, su, hq, hi, hg, hf, ff], axis=1).astype(BF16)
    fb = jnp.pad(f_bias_l.astype(F32), (0, LANES - FOX_HEADS)).reshape(1, LANES)
    return w, fb


def _prep_s5(a_re, a_im, b_re, b_im, c_re, c_im, log_dt, bsz):
    lam = lax.complex(a_re.astype(F32), a_im.astype(F32))
    dt = jnp.exp(log_dt.astype(F32))[:, None]
    a_bar = jnp.exp(lam * dt)
    b_bar = ((a_bar - 1.0) / lam)[..., None] * lax.complex(b_re.astype(F32), b_im.astype(F32))
    eye = jnp.eye(S5_GROUPS, dtype=F32)
    bm_re = jnp.einsum('gpc,gh->gchp', jnp.real(b_bar), eye).reshape(S5_WIDTH, S5_NSTATE)
    bm_im = jnp.einsum('gpc,gh->gchp', jnp.imag(b_bar), eye).reshape(S5_WIDTH, S5_NSTATE)
    bmat = jnp.concatenate([bm_re, bm_im], axis=1).astype(BF16)
    cm_re = jnp.einsum('gcp,gh->gphc', c_re.astype(F32), eye).reshape(S5_NSTATE, S5_WIDTH)
    cm_im = jnp.einsum('gcp,gh->gphc', c_im.astype(F32), eye).reshape(S5_NSTATE, S5_WIDTH)
    cmat = jnp.concatenate([cm_re, -cm_im], axis=0).astype(BF16)
    ar = jnp.broadcast_to(jnp.real(a_bar).reshape(1, S5_NSTATE), (bsz, S5_NSTATE))
    ai = jnp.broadcast_to(jnp.imag(a_bar).reshape(1, S5_NSTATE), (bsz, S5_NSTATE))
    return bmat, ar, ai, cmat


def kernel(x, c, ada_w, ada_b, norm_mix_w, norm_ffn_w, w_in, fox_f_bias, s5_a_re, s5_a_im, s5_b_re, s5_b_im, s5_c_re, s5_c_im, s5_d, s5_log_dt, s5_glu_w, s5_glu_b, hgrn_lb_logits, hgrn_norm_w, branch_norm_fox, branch_norm_s5, w_out, router_w, router_b, moe_w_gate, moe_w_up, moe_w_down, final_norm_w):
    bsz, seq, d = x.shape
    depth = w_in.shape[0]
    lb_cum = jnp.cumsum(jax.nn.softmax(hgrn_lb_logits.astype(F32), axis=0), axis=0)
    lower = lb_cum - lb_cum[0:1]
    om_lb = 1.0 - lower

    mod = _ada_mod(c, ada_w, ada_b)
    rw32 = jnp.pad(router_w.astype(F32), ((0, 0), (0, LANES - N_EXPERTS)))
    rw_hi = rw32.astype(BF16)
    rw_pad = jnp.concatenate([rw_hi, (rw32 - rw_hi.astype(F32)).astype(BF16)], axis=1)
    rb_pad = jnp.pad(router_b.astype(F32), (0, LANES - N_EXPERTS)).reshape(1, LANES)
    fnw = final_norm_w.reshape(1, d).astype(F32)

    for l in range(depth):
        w_pad, fb_pad = _prep_w_in(w_in[l], fox_f_bias[l])
        qkv, su, hqig, hf, cum_t = _in_proj(x, mod[l], norm_mix_w[l].reshape(1, d), w_pad, fb_pad, tm=512)
        o_fox = _fox_attention(qkv, cum_t, tq=256, tk=512)
        bmat, ar, ai, cmat = _prep_s5(s5_a_re[l], s5_a_im[l], s5_b_re[l], s5_b_im[l],
                                      s5_c_re[l], s5_c_im[l], s5_log_dt[l], bsz)
        o_s5 = _s5_mixer(su, bmat, ar, ai, cmat, s5_d[l].reshape(1, S5_WIDTH).astype(F32),
                         s5_glu_w[l].astype(BF16), s5_glu_b[l].reshape(1, S5_WIDTH).astype(F32), ln=64)
        nw2 = jnp.tile(hgrn_norm_w[l].astype(F32), 2).reshape(1, LANES)
        o_hgrn = _hgrn_mixer(hqig, hf, lower[l].reshape(1, -1), om_lb[l].reshape(1, -1), nw2, tc=128)
        x = _out_proj(o_fox, o_s5, o_hgrn, x, mod[l],
                      branch_norm_fox[l].reshape(1, -1).astype(F32),
                      branch_norm_s5[l].reshape(1, -1).astype(F32),
                      w_out[l].astype(BF16), tm=512)
        x = _moe(x, mod[l], norm_ffn_w[l].reshape(1, d).astype(F32), rw_pad, rb_pad,
                 moe_w_gate[l].astype(BF16), moe_w_up[l].astype(BF16), moe_w_down[l].astype(BF16),
                 fnw, tm=1024, final_norm=(l == depth - 1))
    return x
```

```python
import functools
import math

import jax
import jax.numpy as jnp
from jax import lax
from jax.experimental import pallas as pl
from jax.experimental.pallas import tpu as pltpu

F32 = jnp.float32
BF16 = jnp.bfloat16
HIGHEST = lax.Precision.HIGHEST

D_MODEL = 1024
FOX_HEADS = 6
HEAD_DIM = 64
FOX_WIDTH = 384
S5_WIDTH = 256
S5_GROUPS = 16
S5_CH = 16
S5_STATE = 64
S5_NSTATE = S5_GROUPS * S5_STATE
HGRN_WIDTH = 384
N_EXPERTS = 16
N_GROUPS = 4
D_FF = 256
EPS = 1e-6
LANES = 128
NEG_BIG = -1e30
LOG2E = math.log2(math.e)
SOFTMAX_ROWS = 64
MOE_SUB = 256
MOE_CAP = 96

C_QKV = (0, 1152)
C_SU = (1152, 1408)
C_HQIG = (1408, 2560)
C_HF = (2560, 2944)
C_FF = (2944, 3072)
N_IN_PAD = 3072

NT_DIMS = (((1,), (1,)), ((), ()))

VMEM_LIMIT = 56 * 1024 * 1024


def _cparams(sem):
    return pltpu.CompilerParams(dimension_semantics=sem, vmem_limit_bytes=VMEM_LIMIT)


def _sigmoid(x):
    return 1.0 / (1.0 + jnp.exp(-x))


def _silu(x):
    return x * _sigmoid(x)


def _log_sigmoid(x):
    return jnp.minimum(x, 0.0) - jnp.log1p(jnp.exp(-jnp.abs(x)))


def _rms(x):
    return x * lax.rsqrt(jnp.mean(x * x, axis=-1, keepdims=True) + EPS)


def _ada_kernel(c_ref, w_ref, b_ref, o_ref):
    c = c_ref[...]
    o_ref[...] = jnp.dot(_silu(c), w_ref[...], precision=HIGHEST,
                         preferred_element_type=F32) + b_ref[...]


def _ada_mod(c, ada_w, ada_b):
    depth, d, n6 = ada_w.shape
    bsz = c.shape[0]
    nblk = n6 // d
    out = pl.pallas_call(
        _ada_kernel,
        grid=(depth, nblk),
        in_specs=[
            pl.BlockSpec((bsz, d), lambda l, j: (0, 0)),
            pl.BlockSpec((None, d, d), lambda l, j: (l, 0, j)),
            pl.BlockSpec((None, 1, d), lambda l, j: (l, 0, j)),
        ],
        out_specs=pl.BlockSpec((None, bsz, d), lambda l, j: (l, 0, j)),
        out_shape=jax.ShapeDtypeStruct((depth, bsz, n6), F32),
        compiler_params=_cparams(("arbitrary", "arbitrary")),
        name="ada_mod",
    )(c, ada_w, ada_b.reshape(depth, 1, n6))
    return out.reshape(depth, bsz, nblk, d)


def _inproj_kernel(x_ref, mod_ref, nw_ref, w_ref, fb_ref,
                   qkv_ref, su_ref, hqig_ref, hf_ref, cum_ref, carry_ref):
    i = pl.program_id(1)
    tm = x_ref.shape[0]

    @pl.when(i == 0)
    def _():
        carry_ref[...] = jnp.zeros_like(carry_ref)

    h = _rms(x_ref[...]) * nw_ref[...]
    h = h * (1.0 + mod_ref[1:2, :]) + mod_ref[0:1, :]
    hb = h.astype(BF16)

    def proj(c):
        return jnp.dot(hb, w_ref[:, c[0]:c[1]], preferred_element_type=F32)

    qkv_ref[...] = proj(C_QKV).astype(BF16)
    su = proj(C_SU)
    su_ref[0] = su[:, :LANES]
    su_ref[1] = su[:, LANES:]
    hqig_ref[...] = proj(C_HQIG).astype(BF16)
    hf_ref[...] = proj(C_HF)

    lf = _log_sigmoid(proj(C_FF) + fb_ref[...])
    lf_t = lf.T[0:8, :]
    r = lax.broadcasted_iota(jnp.int32, (tm, tm), 0)
    c = lax.broadcasted_iota(jnp.int32, (tm, tm), 1)
    tri_u = jnp.where(r <= c, 1.0, 0.0).astype(F32)
    cum = jnp.dot(lf_t, tri_u, precision=HIGHEST, preferred_element_type=F32) + carry_ref[:, 0:1]
    cum_ref[...] = cum * LOG2E
    carry_ref[...] = jnp.broadcast_to(cum[:, tm - 1:tm], carry_ref.shape)


def _in_proj(x, mod_l, norm_w, w_pad, fb_pad, tm):
    bsz, seq, d = x.shape
    tm = min(tm, seq)
    return pl.pallas_call(
        _inproj_kernel,
        grid=(bsz, seq // tm),
        in_specs=[
            pl.BlockSpec((None, tm, d), lambda b, i: (b, i, 0)),
            pl.BlockSpec((None, 6, d), lambda b, i: (b, 0, 0)),
            pl.BlockSpec((1, d), lambda b, i: (0, 0)),
            pl.BlockSpec((d, N_IN_PAD), lambda b, i: (0, 0)),
            pl.BlockSpec((1, LANES), lambda b, i: (0, 0)),
        ],
        out_specs=[
            pl.BlockSpec((None, tm, 1152), lambda b, i: (b, i, 0)),
            pl.BlockSpec((2, None, tm, LANES), lambda b, i: (0, b, i, 0)),
            pl.BlockSpec((None, tm, 1152), lambda b, i: (b, i, 0)),
            pl.BlockSpec((None, tm, HGRN_WIDTH), lambda b, i: (b, i, 0)),
            pl.BlockSpec((None, 8, tm), lambda b, i: (b, 0, i)),
        ],
        out_shape=[
            jax.ShapeDtypeStruct((bsz, seq, 1152), BF16),
            jax.ShapeDtypeStruct((2, bsz, seq, LANES), F32),
            jax.ShapeDtypeStruct((bsz, seq, 1152), BF16),
            jax.ShapeDtypeStruct((bsz, seq, HGRN_WIDTH), F32),
            jax.ShapeDtypeStruct((bsz, 8, seq), F32),
        ],
        scratch_shapes=[pltpu.VMEM((8, LANES), F32)],
        compiler_params=_cparams(("arbitrary", "arbitrary")),
        name="in_proj",
    )(x, mod_l, norm_w, w_pad, fb_pad)


def _fox_kernel(q_ref, k_ref, v_ref, cum_ref, o_ref, s_sc, p_sc, al_sc, m_sc, l_sc, acc_sc, d_sc,
                *, tq, tk, n_steps, unroll):
    p = pl.program_id(1)
    seq = q_ref.shape[0]
    nq = seq // tq
    lane = lax.broadcasted_iota(jnp.int32, (tq, LANES), 1)
    first = lane < HEAD_DIM

    def advance(i, j):
        last = j == (i * tq) // tk
        return jnp.minimum(i + last.astype(jnp.int32), nq - 1), jnp.where(last, 0, j + 1)

    def stage_qk(i, j):
        q = q_ref[pl.ds(pl.multiple_of(i * tq, tq), tq), :]
        zero = jnp.zeros_like(q)
        q2 = jnp.concatenate([jnp.where(first, q, zero), jnp.where(first, zero, q)], axis=0)
        k0 = pl.multiple_of(j * tk, tk)
        s = lax.dot_general(q2, k_ref[pl.ds(k0, tk), :], NT_DIMS, preferred_element_type=F32)
        bias_a = cum_ref[pl.ds(2 * p, 1), pl.ds(k0, tk)]
        bias_b = cum_ref[pl.ds(2 * p + 1, 1), pl.ds(k0, tk)]
        is_last = j == (i * tq) // tk
        mask = d_sc[jnp.where(is_last, 1 + i - j * (tk // tq), 0)]
        s_sc[...] = jnp.concatenate([s[:tq] + (mask - bias_a), s[tq:] + (mask - bias_b)], axis=0)

    def stage_softmax(j):
        cap = jnp.where(j == 0, NEG_BIG, -NEG_BIG)
        for r0 in range(0, 2 * tq, SOFTMAX_ROWS):
            rs = slice(r0, r0 + SOFTMAX_ROWS)
            m_prev = jnp.minimum(m_sc[rs, :], cap)
            s = s_sc[rs, :]
            m_next = jnp.maximum(m_prev, jnp.max(s, axis=1, keepdims=True))
            pr = jnp.exp2(s - jnp.tile(m_next, (1, tk // LANES)))
            alpha = jnp.exp2(m_prev - m_next)
            l_sc[rs, :] = alpha * l_sc[rs, :] + jnp.sum(pr, axis=1, keepdims=True)
            m_sc[rs, :] = m_next
            p_sc[rs, :] = pr.astype(BF16)
            al_sc[rs, :] = alpha

    def stage_pv(i, j):
        k0 = pl.multiple_of(j * tk, tk)
        acc = acc_sc[...] * al_sc[...] + jnp.dot(p_sc[...], v_ref[pl.ds(k0, tk), :],
                                                  preferred_element_type=F32)
        acc_sc[...] = acc
        o = acc / l_sc[...]
        o_ref[pl.ds(pl.multiple_of(i * tq, tq), tq), :] = jnp.where(first, o[:tq], o[tq:])

    col_minus_row = (lax.broadcasted_iota(jnp.int32, (tq, tk), 1)
                     - lax.broadcasted_iota(jnp.int32, (tq, tk), 0))
    d_sc[0] = jnp.zeros((tq, tk), F32)
    for k in range(tk // tq):
        d_sc[1 + k] = jnp.where(col_minus_row <= k * tq, 0.0, NEG_BIG)
    s_sc[...] = jnp.zeros(s_sc.shape, F32)
    p_sc[...] = jnp.zeros(p_sc.shape, BF16)
    al_sc[...] = jnp.ones(al_sc.shape, F32)
    m_sc[...] = jnp.full(m_sc.shape, NEG_BIG, F32)
    l_sc[...] = jnp.ones(l_sc.shape, F32)
    acc_sc[...] = jnp.zeros(acc_sc.shape, F32)

    def body(t, carry):
        ia, ja, ib, jb, ic, jc = carry
        stage_pv(ic, jc)
        stage_softmax(jb)
        stage_qk(ia, ja)
        na, nja = advance(ia, ja)
        return na, nja, ia, ja, ib, jb

    zero = jnp.int32(0)
    lax.fori_loop(0, n_steps + 2, body, (zero, zero, zero, zero, zero, zero), unroll=unroll)


def _fox_attention(qkv, cum_t, tq, tk):
    bsz, seq, _ = qkv.shape
    tq = min(tq, seq)
    tk = min(max(tk, tq), seq)
    assert tk % tq == 0 and seq % tk == 0
    npair = FOX_HEADS // 2
    n_steps = sum((i * tq) // tk + 1 for i in range(seq // tq))
    kern = functools.partial(_fox_kernel, tq=tq, tk=tk, n_steps=n_steps,
                             unroll=2 if n_steps % 2 == 0 else 1)
    col = lambda off: pl.BlockSpec((None, seq, LANES), lambda b, p: (b, 0, off + p))
    return pl.pallas_call(
        kern,
        grid=(bsz, npair),
        in_specs=[col(0), col(npair), col(2 * npair),
                  pl.BlockSpec((None, 8, seq), lambda b, p: (b, 0, 0))],
        out_specs=col(0),
        out_shape=jax.ShapeDtypeStruct((bsz, seq, FOX_WIDTH), F32),
        scratch_shapes=[
            pltpu.VMEM((2 * tq, tk), F32),
            pltpu.VMEM((2 * tq, tk), BF16),
            pltpu.VMEM((2 * tq, LANES), F32),
            pltpu.VMEM((2 * tq, LANES), F32),
            pltpu.VMEM((2 * tq, LANES), F32),
            pltpu.VMEM((2 * tq, LANES), F32),
            pltpu.VMEM((1 + tk // tq, tq, tk), F32),
        ],
        compiler_params=_cparams(("arbitrary", "arbitrary")),
        name="fox_attn",
    )(qkv, qkv, qkv, cum_t)


def _gelu_tanh(x):
    return 0.5 * x * (1.0 + jnp.tanh(math.sqrt(2.0 / math.pi) * (x + 0.044715 * (x * x * x))))


def _s5_kernel(su_ref, bm_ref, ar_ref, ai_ref, cm_ref, d_ref, gw_ref, gb_ref, o_ref,
               utb, bu, ytb, st):
    i = pl.program_id(0)
    nb = su_ref.shape[1]
    ln = su_ref.shape[2]
    ns = S5_NSTATE

    @pl.when(i == 0)
    def _():
        st[...] = jnp.zeros_like(st)

    for k in range(2):
        for b in range(nb):
            utb[k, pl.ds(b, ln, stride=nb), :] = su_ref[k, b]
    u = jnp.concatenate([utb[0], utb[1]], axis=1)
    bu[...] = jnp.dot(u.astype(BF16), bm_ref[...], preferred_element_type=F32)

    ar = ar_ref[...]
    ai = ai_ref[...]

    def body(t, carry):
        re, im = carry
        r0 = pl.multiple_of(t * nb, nb)
        bur = bu[pl.ds(r0, nb), 0:ns]
        bui = bu[pl.ds(r0, nb), ns:2 * ns]
        nre = ar * re - ai * im + bur
        nim = ar * im + ai * re + bui
        bu[pl.ds(r0, nb), 0:ns] = nre
        bu[pl.ds(r0, nb), ns:2 * ns] = nim
        return nre, nim

    re, im = lax.fori_loop(0, ln, body, (st[:, 0:ns], st[:, ns:2 * ns]))
    st[:, 0:ns] = re
    st[:, ns:2 * ns] = im

    y = jnp.dot(bu[...].astype(BF16), cm_ref[...], preferred_element_type=F32) + d_ref[...] * u
    y = _gelu_tanh(y)
    gate = _sigmoid(jnp.dot(y.astype(BF16), gw_ref[...], preferred_element_type=F32) + gb_ref[...])
    out = y * gate
    ytb[0] = out[:, :LANES]
    ytb[1] = out[:, LANES:]
    for k in range(2):
        for b in range(nb):
            o_ref[b, :, k * LANES:(k + 1) * LANES] = ytb[k, pl.ds(b, ln, stride=nb), :]


def _s5_mixer(su, bmat, ar, ai, cmat, dvec, glu_w, glu_b, ln):
    _, bsz, seq, _ = su.shape
    ln = min(ln, seq)
    rows = bsz * ln
    const = lambda shape: pl.BlockSpec(shape, lambda i: (0,) * len(shape))
    return pl.pallas_call(
        _s5_kernel,
        grid=(seq // ln,),
        in_specs=[
            pl.BlockSpec((2, bsz, ln, LANES), lambda i: (0, 0, i, 0)),
            const((S5_WIDTH, 2 * S5_NSTATE)),
            const((bsz, S5_NSTATE)),
            const((bsz, S5_NSTATE)),
            const((2 * S5_NSTATE, S5_WIDTH)),
            const((1, S5_WIDTH)),
            const((S5_WIDTH, S5_WIDTH)),
            const((1, S5_WIDTH)),
        ],
        out_specs=pl.BlockSpec((bsz, ln, S5_WIDTH), lambda i: (0, i, 0)),
        out_shape=jax.ShapeDtypeStruct((bsz, seq, S5_WIDTH), F32),
        scratch_shapes=[
            pltpu.VMEM((2, rows, LANES), F32),
            pltpu.VMEM((rows, 2 * S5_NSTATE), F32),
            pltpu.VMEM((2, rows, LANES), F32),
            pltpu.VMEM((bsz, 2 * S5_NSTATE), F32),
        ],
        compiler_params=_cparams(("arbitrary",)),
        name="s5_mixer",
    )(su, bmat, ar, ai, cmat, dvec, glu_w, glu_b)


def _block_row(a, blk, r):
    t = a.shape[0]
    if blk >= 8:
        a3 = a.reshape(t // blk, blk, LANES)
        return jnp.broadcast_to(a3[:, r:r + 1, :], a3.shape).reshape(t, LANES)
    row = lax.broadcasted_iota(jnp.int32, a.shape, 0)
    res = a
    for q in range(blk):
        if q != r:
            res = jnp.where(row % blk == q, pltpu.roll(a, (q - r) % t, axis=0), res)
    return res


def _hgrn_kernel(hq_ref, hi_ref, hg_ref, hf_ref, lb_ref, omlb_ref, nw_ref, o_ref, lvl_sc, mask_sc, *, tc):
    seq = hf_ref.shape[0]
    nlev = tc.bit_length() - 1
    lane = lax.broadcasted_iota(jnp.int32, (tc, LANES), 1)
    first = lane < HEAD_DIM
    row = lax.broadcasted_iota(jnp.int32, (tc, LANES), 0)
    rr = lax.broadcasted_iota(jnp.int32, (tc, tc), 0)
    cc = lax.broadcasted_iota(jnp.int32, (tc, tc), 1)
    diff = rr ^ cc
    bits = jnp.zeros((tc, tc), jnp.int32)
    for i in range(nlev):
        bits = bits + jnp.where((diff >> i) != 0, 1, 0)
    lvl_sc[...] = jnp.where(rr >= cc, bits, -1)
    for i in range(nlev):
        upper = ((row >> i) & 1) == 1
        mask_sc[i, 0] = jnp.where(jnp.logical_and(upper, first), 1.0, 0.0)
        mask_sc[i, 1] = jnp.where(jnp.logical_and(upper, jnp.logical_not(first)), 1.0, 0.0)
        mask_sc[i, 2] = jnp.where(upper, 0.0, 1.0)
    sr = lax.broadcasted_iota(jnp.int32, (LANES, LANES), 0)
    sc = lax.broadcasted_iota(jnp.int32, (LANES, LANES), 1)
    blockdiag = (sr < HEAD_DIM) == (sc < HEAD_DIM)
    lb = lb_ref[...]
    omlb = omlb_ref[...]
    nw = nw_ref[...]

    def chunk(c, st):
        r0 = pl.multiple_of(c * tc, tc)
        z = hf_ref[pl.ds(r0, tc), :]
        q = _silu(hq_ref[pl.ds(r0, tc), :].astype(F32))
        v = hi_ref[pl.ds(r0, tc), :]
        g = hg_ref[pl.ds(r0, tc), :].astype(F32)
        e = jnp.exp(-jnp.abs(z))
        s_big = 1.0 / (1.0 + e)
        s_small = e * s_big
        pos = z >= 0.0
        f = lb + omlb * jnp.where(pos, s_big, s_small)
        kk = omlb * jnp.where(pos, s_small, s_big)

        def scores(q_a, q_b, km):
            q2 = jnp.concatenate([q_a, q_b], axis=0).astype(BF16)
            return lax.dot_general(q2, km.astype(BF16), NT_DIMS, preferred_element_type=F32)

        s = scores(jnp.where(first, q, 0.0), jnp.where(first, 0.0, q), kk)
        on_diag = lvl_sc[...] == 0
        tot_a = jnp.where(on_diag, s[:tc], 0.0)
        tot_b = jnp.where(on_diag, s[tc:], 0.0)
        a_m = f
        b_m = jnp.ones_like(f)
        for i in range(nlev):
            m = 1 << i
            lo = mask_sc[i, 2]
            qa = q * a_m
            s = scores(qa * mask_sc[i, 0], qa * mask_sc[i, 1], kk * b_m * lo)
            sel = lvl_sc[...] == i + 1
            tot_a = jnp.where(sel, s[:tc], tot_a)
            tot_b = jnp.where(sel, s[tc:], tot_b)
            lower_total = _block_row(a_m, 2 * m, m - 1)
            upper_total = _block_row(a_m, 2 * m, 2 * m - 1)
            upper = lo < 0.5
            a_m = jnp.where(upper, a_m * lower_total, a_m)
            b_m = jnp.where(upper, b_m, b_m * upper_total)

        o = lax.dot_general((q * a_m).astype(BF16), st.astype(BF16), NT_DIMS, preferred_element_type=F32)
        pv = jnp.dot(jnp.concatenate([tot_a, tot_b], axis=0).astype(BF16), v, preferred_element_type=F32)
        o = o + jnp.where(first, pv[:tc], pv[tc:])

        upd = jnp.dot(v.astype(F32).T.astype(BF16), (kk * b_m).astype(BF16), preferred_element_type=F32)
        st_new = st * a_m[tc - 1:tc, :] + jnp.where(blockdiag, upd, 0.0)

        o2 = o * o
        s_a = jnp.sum(jnp.where(first, o2, 0.0), axis=-1, keepdims=True)
        s_b = jnp.sum(jnp.where(first, 0.0, o2), axis=-1, keepdims=True)
        ms = jnp.where(first, s_a, s_b) * (1.0 / HEAD_DIM)
        o_ref[pl.ds(r0, tc), :] = o * lax.rsqrt(ms + EPS) * nw * _silu(g)
        return st_new

    lax.fori_loop(0, seq // tc, chunk, jnp.zeros((LANES, LANES), F32), unroll=2)


def _hgrn_mixer(hqig, hf, lb, om_lb, nw2, tc):
    bsz, seq, _ = hf.shape
    tc = min(tc, seq)
    npair = HGRN_WIDTH // LANES
    kern = functools.partial(_hgrn_kernel, tc=tc)
    col = lambda off: pl.BlockSpec((None, seq, LANES), lambda b, p: (b, 0, off + p))
    par = pl.BlockSpec((1, LANES), lambda b, p: (0, p))
    return pl.pallas_call(
        kern,
        grid=(bsz, npair),
        in_specs=[col(0), col(npair), col(2 * npair), col(0), par, par,
                  pl.BlockSpec((1, LANES), lambda b, p: (0, 0))],
        out_specs=col(0),
        out_shape=jax.ShapeDtypeStruct((bsz, seq, HGRN_WIDTH), F32),
        scratch_shapes=[
            pltpu.VMEM((tc, tc), jnp.int32),
            pltpu.VMEM((tc.bit_length() - 1, 3, tc, LANES), F32),
        ],
        compiler_params=_cparams(("arbitrary", "arbitrary")),
        name="hgrn_mixer",
    )(hqig, hqig, hqig, hf, lb, om_lb, nw2)


def _outproj_kernel(fox_ref, s5_ref, hg_ref, x_ref, mod_ref, nf_ref, ns_ref, w_ref, o_ref):
    of = (_rms(fox_ref[...]) * nf_ref[...]).astype(BF16)
    os5 = (_rms(s5_ref[...]) * ns_ref[...]).astype(BF16)
    oh = hg_ref[...].astype(BF16)
    a, b = FOX_WIDTH, FOX_WIDTH + S5_WIDTH
    mix = jnp.dot(of, w_ref[0:a, :], preferred_element_type=F32)
    mix = mix + jnp.dot(os5, w_ref[a:b, :], preferred_element_type=F32)
    mix = mix + jnp.dot(oh, w_ref[b:, :], preferred_element_type=F32)
    o_ref[...] = x_ref[...] + mod_ref[2:3, :] * mix


def _out_proj(o_fox, o_s5, o_hgrn, x, mod_l, nf, ns, w_out, tm):
    bsz, seq, d = x.shape
    tm = min(tm, seq)
    row = lambda w: pl.BlockSpec((None, tm, w), lambda b, i: (b, i, 0))
    return pl.pallas_call(
        _outproj_kernel,
        grid=(bsz, seq // tm),
        in_specs=[
            row(FOX_WIDTH), row(S5_WIDTH), row(HGRN_WIDTH), row(d),
            pl.BlockSpec((None, 6, d), lambda b, i: (b, 0, 0)),
            pl.BlockSpec((1, FOX_WIDTH), lambda b, i: (0, 0)),
            pl.BlockSpec((1, S5_WIDTH), lambda b, i: (0, 0)),
            pl.BlockSpec((d, d), lambda b, i: (0, 0)),
        ],
        out_specs=row(d),
        out_shape=jax.ShapeDtypeStruct((bsz, seq, d), F32),
        compiler_params=_cparams(("arbitrary", "arbitrary")),
        name="out_proj",
    )(o_fox, o_s5, o_hgrn, x, mod_l, nf, ns, w_out)


def _routing(logits_t):
    mx = jnp.max(logits_t, axis=0, keepdims=True)
    ex = jnp.exp(logits_t - mx)
    probs = ex / jnp.sum(ex, axis=0, keepdims=True)
    p = [probs[e:e + 1, :] for e in range(N_EXPERTS)]
    epg = N_EXPERTS // N_GROUPS
    scores = []
    for g in range(N_GROUPS):
        a, b, c, d = p[epg * g:epg * g + epg]
        hi1, lo1 = jnp.maximum(a, b), jnp.minimum(a, b)
        hi2, lo2 = jnp.maximum(c, d), jnp.minimum(c, d)
        top1 = jnp.maximum(hi1, hi2)
        top2 = jnp.maximum(jnp.minimum(hi1, hi2), jnp.maximum(lo1, lo2))
        scores.append(top1 + top2)
    best = jnp.zeros_like(scores[0], dtype=jnp.int32)
    bs = scores[0]
    for g in range(1, N_GROUPS):
        upd = scores[g] > bs
        best = jnp.where(upd, g, best)
        bs = jnp.where(upd, scores[g], bs)
    ig = []
    for j in range(epg):
        val = p[j]
        for g in range(1, N_GROUPS):
            val = jnp.where(best == g, p[epg * g + j], val)
        ig.append(val)
    i1 = jnp.zeros_like(best)
    w1 = ig[0]
    for j in range(1, epg):
        upd = ig[j] > w1
        i1 = jnp.where(upd, j, i1)
        w1 = jnp.where(upd, ig[j], w1)
    i2 = jnp.zeros_like(best)
    w2 = jnp.full_like(w1, -1.0)
    for j in range(epg):
        upd = jnp.logical_and(i1 != j, ig[j] > w2)
        i2 = jnp.where(upd, j, i2)
        w2 = jnp.where(upd, ig[j], w2)
    den = w1 + w2
    tw1 = w1 / den
    tw2 = w2 / den
    e1 = best * epg + i1
    e2 = best * epg + i2
    return [jnp.where(e1 == e, tw1, 0.0) + jnp.where(e2 == e, tw2, 0.0) for e in range(N_EXPERTS)], best


def _moe_kernel(x_ref, mod_ref, nw_ref, rw_ref, rb_ref, wg_ref, wu_ref, wd_ref, fnw_ref,
                o_ref, h_sc, comb_sc, combt_sc, hs_sc, ws_sc, ys_sc, acc_sc, cnt_sm, *, final_norm):
    grp = pl.program_id(2)
    tm, d = x_ref.shape
    epg = N_EXPERTS // N_GROUPS
    sub, cap = MOE_SUB, MOE_CAP
    ns = tm // sub
    gc = N_GROUPS * cap
    row_oh, row_slot = N_EXPERTS, N_EXPERTS + 8

    @pl.when(grp == 0)
    def _():
        h = _rms(x_ref[...]) * nw_ref[...]
        h = h * (1.0 + mod_ref[4:5, :]) + mod_ref[3:4, :]
        hb = h.astype(BF16)
        h_sc[...] = hb
        h_lo = (h - hb.astype(F32)).astype(BF16)
        hi_part = jnp.dot(hb, rw_ref[...], preferred_element_type=F32)
        lo_part = jnp.dot(h_lo, rw_ref[:, 0:LANES], preferred_element_type=F32)
        logits = hi_part[:, 0:LANES] + hi_part[:, LANES:] + lo_part + rb_ref[...]
        rows, best = _routing(logits.T[0:N_EXPERTS, :])
        combt_sc[...] = jnp.zeros_like(combt_sc)
        for j in range(N_EXPERTS):
            combt_sc[j:j + 1, :] = rows[j]
        for g in range(N_GROUPS):
            combt_sc[row_oh + g:row_oh + g + 1, :] = jnp.where(best == g, 1.0, 0.0)
        oh8 = combt_sc[row_oh:row_oh + 8, :]
        r = lax.broadcasted_iota(jnp.int32, (sub, sub), 0)
        c = lax.broadcasted_iota(jnp.int32, (sub, sub), 1)
        tri_u = jnp.where(r <= c, 1.0, 0.0).astype(BF16)
        incl = [jnp.dot(oh8[:, s * sub:(s + 1) * sub].astype(BF16), tri_u, preferred_element_type=F32)
                for s in range(ns)]
        cmax = incl[0][:, sub - 1:sub]
        for s in range(1, ns):
            cmax = jnp.maximum(cmax, incl[s][:, sub - 1:sub])
        for g in range(N_GROUPS):
            cnt_sm[g] = jnp.max(cmax[g:g + 1, :]).astype(jnp.int32)
        over = jnp.where(cmax > cap, 1.0, 0.0)
        best_f = best.astype(F32)
        for s in range(ns):
            cols = slice(s * sub, (s + 1) * sub)
            oh_s = oh8[:, cols]
            rank = jnp.sum(oh_s * incl[s], axis=0, keepdims=True) - 1.0
            dense = jnp.sum(oh_s * over, axis=0, keepdims=True)
            combt_sc[row_slot:row_slot + 1, cols] = jnp.where(dense > 0.5, -1.0, best_f[:, cols] * cap + rank)
        comb_sc[...] = combt_sc[...].T
        for s in range(ns):
            toks = slice(s * sub, (s + 1) * sub)
            slot_row = combt_sc[row_slot:row_slot + 1, toks].astype(jnp.int32)
            ri = lax.broadcasted_iota(jnp.int32, (gc, sub), 0)
            p = jnp.where(ri == slot_row, 1.0, 0.0).astype(BF16)
            hs_all = jnp.dot(p, h_sc[toks, :], preferred_element_type=F32).astype(BF16)
            cs = comb_sc[toks, :]
            c_hi = cs.astype(BF16)
            c_lo = (cs - c_hi.astype(F32)).astype(BF16)
            ws_all = jnp.dot(p, jnp.concatenate([c_hi, c_lo], axis=1), preferred_element_type=F32)
            ws_all = ws_all[:, 0:LANES] + ws_all[:, LANES:]
            for g in range(N_GROUPS):
                hs_sc[g, s * cap:(s + 1) * cap, :] = hs_all[g * cap:(g + 1) * cap, :]
                ws_sc[g, s * cap:(s + 1) * cap, :] = ws_all[g * cap:(g + 1) * cap, :]
        acc_sc[...] = jnp.zeros_like(acc_sc)

    def experts(h_rows, w_rows):
        lane = lax.broadcasted_iota(jnp.int32, w_rows.shape, 1)
        hid = []
        for j in range(epg):
            gate = jnp.dot(h_rows, wg_ref[j], preferred_element_type=F32)
            up = jnp.dot(h_rows, wu_ref[j], preferred_element_type=F32)
            w_e = jnp.sum(jnp.where(lane == grp * epg + j, w_rows, 0.0), axis=-1, keepdims=True)
            hid.append((_silu(gate) * up * w_e).astype(BF16))
        wd = wd_ref[...].reshape(epg * D_FF, d)
        return jnp.dot(jnp.concatenate(hid, axis=1), wd, preferred_element_type=F32)

    cnt = cnt_sm[grp]

    @pl.when(cnt <= cap)
    def _():
        ys = experts(hs_sc[grp], ws_sc[grp]).astype(BF16)
        for s in range(ns):
            ys_sc[s, grp] = ys[s * cap:(s + 1) * cap, :]

    @pl.when(cnt > cap)
    def _():
        acc_sc[...] += experts(h_sc[...], comb_sc[...])
        for s in range(ns):
            ys_sc[s, grp] = jnp.zeros((cap, d), BF16)

    @pl.when(grp == N_GROUPS - 1)
    def _():
        for s in range(ns):
            toks = slice(s * sub, (s + 1) * sub)
            slot_col = comb_sc[toks, row_slot:row_slot + 1].astype(jnp.int32)
            li = lax.broadcasted_iota(jnp.int32, (sub, gc), 1)
            pt = jnp.where(li == slot_col, 1.0, 0.0).astype(BF16)
            y = jnp.dot(pt, ys_sc[s].reshape(gc, d), preferred_element_type=F32) + acc_sc[toks, :]
            y = x_ref[toks, :] + mod_ref[5:6, :] * y
            if final_norm:
                y = _rms(y) * fnw_ref[...]
            o_ref[toks, :] = y


def _moe(x, mod_l, norm_w, rw_pad, rb_pad, wg, wu, wd, fnw, tm, final_norm):
    bsz, seq, d = x.shape
    tm = min(tm, seq)
    assert tm % MOE_SUB == 0
    ns = tm // MOE_SUB
    epg = N_EXPERTS // N_GROUPS
    kern = functools.partial(_moe_kernel, final_norm=final_norm)
    return pl.pallas_call(
        kern,
        grid=(bsz, seq // tm, N_GROUPS),
        in_specs=[
            pl.BlockSpec((None, tm, d), lambda b, i, e: (b, i, 0)),
            pl.BlockSpec((None, 6, d), lambda b, i, e: (b, 0, 0)),
            pl.BlockSpec((1, d), lambda b, i, e: (0, 0)),
            pl.BlockSpec((d, 2 * LANES), lambda b, i, e: (0, 0)),
            pl.BlockSpec((1, LANES), lambda b, i, e: (0, 0)),
            pl.BlockSpec((epg, d, D_FF), lambda b, i, e: (e, 0, 0)),
            pl.BlockSpec((epg, d, D_FF), lambda b, i, e: (e, 0, 0)),
            pl.BlockSpec((epg, D_FF, d), lambda b, i, e: (e, 0, 0)),
            pl.BlockSpec((1, d), lambda b, i, e: (0, 0)),
        ],
        out_specs=pl.BlockSpec((None, tm, d), lambda b, i, e: (b, i, 0)),
        out_shape=jax.ShapeDtypeStruct((bsz, seq, d), F32),
        scratch_shapes=[
            pltpu.VMEM((tm, d), BF16),
            pltpu.VMEM((tm, LANES), F32),
            pltpu.VMEM((LANES, tm), F32),
            pltpu.VMEM((N_GROUPS, ns * MOE_CAP, d), BF16),
            pltpu.VMEM((N_GROUPS, ns * MOE_CAP, LANES), F32),
            pltpu.VMEM((ns, N_GROUPS, MOE_CAP, d), BF16),
            pltpu.VMEM((tm, d), F32),
            pltpu.SMEM((N_GROUPS,), jnp.int32),
        ],
        compiler_params=_cparams(("arbitrary", "arbitrary", "arbitrary")),
        name="moe",
    )(x, mod_l, norm_w, rw_pad, rb_pad, wg, wu, wd, fnw)


def _prep_w_in(w_in_l, f_bias_l):
    a = FOX_WIDTH
    o_ff = 3 * a
    o_su = o_ff + FOX_HEADS
    o_hq = o_su + S5_WIDTH
    hw = HGRN_WIDTH
    fq = w_in_l[:, 0:a] * (HEAD_DIM ** -0.5 * LOG2E)
    fk, fv = w_in_l[:, a:2 * a], w_in_l[:, 2 * a:3 * a]
    ff = jnp.pad(w_in_l[:, o_ff:o_su], ((0, 0), (0, LANES - FOX_HEADS)))
    su = w_in_l[:, o_su:o_hq]
    hq = w_in_l[:, o_hq:o_hq + hw]
    hf = w_in_l[:, o_hq + hw:o_hq + 2 * hw]
    hi = w_in_l[:, o_hq + 2 * hw:o_hq + 3 * hw]
    hg = w_in_l[:, o_hq + 3 * hw:o_hq + 4 * hw]
    w = jnp.concatenate([fq, fk, fv, su, hq, hi, hg, hf, ff], axis=1).astype(BF16)
    fb = jnp.pad(f_bias_l.astype(F32), (0, LANES - FOX_HEADS)).reshape(1, LANES)
    return w, fb


def _prep_s5(a_re, a_im, b_re, b_im, c_re, c_im, log_dt, bsz):
    lam = lax.complex(a_re.astype(F32), a_im.astype(F32))
    dt = jnp.exp(log_dt.astype(F32))[:, None]
    a_bar = jnp.exp(lam * dt)
    b_bar = ((a_bar - 1.0) / lam)[..., None] * lax.complex(b_re.astype(F32), b_im.astype(F32))
    eye = jnp.eye(S5_GROUPS, dtype=F32)
    bm_re = jnp.einsum('gpc,gh->gchp', jnp.real(b_bar), eye).reshape(S5_WIDTH, S5_NSTATE)
    bm_im = jnp.einsum('gpc,gh->gchp', jnp.imag(b_bar), eye).reshape(S5_WIDTH, S5_NSTATE)
    bmat = jnp.concatenate([bm_re, bm_im], axis=1).astype(BF16)
    cm_re = jnp.einsum('gcp,gh->gphc', c_re.astype(F32), eye).reshape(S5_NSTATE, S5_WIDTH)
    cm_im = jnp.einsum('gcp,gh->gphc', c_im.astype(F32), eye).reshape(S5_NSTATE, S5_WIDTH)
    cmat = jnp.concatenate([cm_re, -cm_im], axis=0).astype(BF16)
    ar = jnp.broadcast_to(jnp.real(a_bar).reshape(1, S5_NSTATE), (bsz, S5_NSTATE))
    ai = jnp.broadcast_to(jnp.imag(a_bar).reshape(1, S5_NSTATE), (bsz, S5_NSTATE))
    return bmat, ar, ai, cmat


def kernel(x, c, ada_w, ada_b, norm_mix_w, norm_ffn_w, w_in, fox_f_bias, s5_a_re, s5_a_im, s5_b_re, s5_b_im, s5_c_re, s5_c_im, s5_d, s5_log_dt, s5_glu_w, s5_glu_b, hgrn_lb_logits, hgrn_norm_w, branch_norm_fox, branch_norm_s5, w_out, router_w, router_b, moe_w_gate, moe_w_up, moe_w_down, final_norm_w):
    bsz, seq, d = x.shape
    depth = w_in.shape[0]
    lb_cum = jnp.cumsum(jax.nn.softmax(hgrn_lb_logits.astype(F32), axis=0), axis=0)
    lower = lb_cum - lb_cum[0:1]
    om_lb = 1.0 - lower

    mod = _ada_mod(c, ada_w, ada_b)
    rw32 = jnp.pad(router_w.astype(F32), ((0, 0), (0, LANES - N_EXPERTS)))
    rw_hi = rw32.astype(BF16)
    rw_pad = jnp.concatenate([rw_hi, (rw32 - rw_hi.astype(F32)).astype(BF16)], axis=1)
    rb_pad = jnp.pad(router_b.astype(F32), (0, LANES - N_EXPERTS)).reshape(1, LANES)
    fnw = final_norm_w.reshape(1, d).astype(F32)

    for l in range(depth):
        w_pad, fb_pad = _prep_w_in(w_in[l], fox_f_bias[l])
        qkv, su, hqig, hf, cum_t = _in_proj(x, mod[l], norm_mix_w[l].reshape(1, d), w_pad, fb_pad, tm=512)
        o_fox = _fox_attention(qkv, cum_t, tq=256, tk=512)
        bmat, ar, ai, cmat = _prep_s5(s5_a_re[l], s5_a_im[l], s5_b_re[l], s5_b_im[l],
                                      s5_c_re[l], s5_c_im[l], s5_log_dt[l], bsz)
        o_s5 = _s5_mixer(su, bmat, ar, ai, cmat, s5_d[l].reshape(1, S5_WIDTH).astype(F32),
                         s5_glu_w[l].astype(BF16), s5_glu_b[l].reshape(1, S5_WIDTH).astype(F32), ln=64)
        nw2 = jnp.tile(hgrn_norm_w[l].astype(F32), 2).reshape(1, LANES)
        o_hgrn = _hgrn_mixer(hqig, hf, lower[l].reshape(1, -1), om_lb[l].reshape(1, -1), nw2, tc=128)
        x = _out_proj(o_fox, o_s5, o_hgrn, x, mod[l],
                      branch_norm_fox[l].reshape(1, -1).astype(F32),
                      branch_norm_s5[l].reshape(1, -1).astype(F32),
                      w_out[l].astype(BF16), tm=512)
        x = _moe(x, mod[l], norm_ffn_w[l].reshape(1, d).astype(F32), rw_pad, rb_pad,
                 moe_w_gate[l].astype(BF16), moe_w_up[l].astype(BF16), moe_w_down[l].astype(BF16),
                 fnw, tm=1024, final_norm=(l == depth - 1))
    return x
```

```python
import functools
import math

import jax
import jax.numpy as jnp
from jax import lax
from jax.experimental import pallas as pl
from jax.experimental.pallas import tpu as pltpu

F32 = jnp.float32
BF16 = jnp.bfloat16
HIGHEST = lax.Precision.HIGHEST

D_MODEL = 1024
FOX_HEADS = 6
HEAD_DIM = 64
FOX_WIDTH = 384
S5_WIDTH = 256
S5_GROUPS = 16
S5_CH = 16
S5_STATE = 64
S5_NSTATE = S5_GROUPS * S5_STATE
HGRN_WIDTH = 384
N_EXPERTS = 16
N_GROUPS = 4
D_FF = 256
EPS = 1e-6
LANES = 128
NEG_BIG = -1e30
LOG2E = math.log2(math.e)
SOFTMAX_ROWS = 64
MOE_SUB = 256
MOE_CAP = 96

C_QKV = (0, 1152)
C_SU = (1152, 1408)
C_HQIG = (1408, 2560)
C_HF = (2560, 2944)
C_FF = (2944, 3072)
N_IN_PAD = 3072

NT_DIMS = (((1,), (1,)), ((), ()))

VMEM_LIMIT = 56 * 1024 * 1024


def _cparams(sem):
    return pltpu.CompilerParams(dimension_semantics=sem, vmem_limit_bytes=VMEM_LIMIT)


def _sigmoid(x):
    return 1.0 / (1.0 + jnp.exp(-x))


def _silu(x):
    hx = 0.5 * x
    return hx + hx * jnp.tanh(hx)


def _log_sigmoid(x):
    return jnp.minimum(x, 0.0) - jnp.log1p(jnp.exp(-jnp.abs(x)))


def _rms(x):
    return x * lax.rsqrt(jnp.mean(x * x, axis=-1, keepdims=True) + EPS)


def _ada_kernel(c_ref, w_ref, b_ref, o_ref):
    c = c_ref[...]
    o_ref[...] = jnp.dot(_silu(c), w_ref[...], precision=HIGHEST,
                         preferred_element_type=F32) + b_ref[...]


def _ada_mod(c, ada_w, ada_b):
    depth, d, n6 = ada_w.shape
    bsz = c.shape[0]
    nblk = n6 // d
    out = pl.pallas_call(
        _ada_kernel,
        grid=(depth, nblk),
        in_specs=[
            pl.BlockSpec((bsz, d), lambda l, j: (0, 0)),
            pl.BlockSpec((None, d, d), lambda l, j: (l, 0, j)),
            pl.BlockSpec((None, 1, d), lambda l, j: (l, 0, j)),
        ],
        out_specs=pl.BlockSpec((None, bsz, d), lambda l, j: (l, 0, j)),
        out_shape=jax.ShapeDtypeStruct((depth, bsz, n6), F32),
        compiler_params=_cparams(("arbitrary", "arbitrary")),
        name="ada_mod",
    )(c, ada_w, ada_b.reshape(depth, 1, n6))
    return out.reshape(depth, bsz, nblk, d)


def _inproj_kernel(x_ref, mod_ref, nw_ref, w_ref, fb_ref,
                   qkv_ref, su_ref, hqig_ref, hf_ref, cum_ref, carry_ref):
    i = pl.program_id(1)
    tm = x_ref.shape[0]

    @pl.when(i == 0)
    def _():
        carry_ref[...] = jnp.zeros_like(carry_ref)

    h = _rms(x_ref[...]) * nw_ref[...]
    h = h * (1.0 + mod_ref[1:2, :]) + mod_ref[0:1, :]
    hb = h.astype(BF16)

    def proj(c):
        return jnp.dot(hb, w_ref[:, c[0]:c[1]], preferred_element_type=F32)

    qkv_ref[...] = proj(C_QKV).astype(BF16)
    su = proj(C_SU)
    su_ref[0] = su[:, :LANES]
    su_ref[1] = su[:, LANES:]
    hqig_ref[...] = proj(C_HQIG).astype(BF16)
    hf_ref[...] = proj(C_HF)

    lf = _log_sigmoid(proj(C_FF) + fb_ref[...])
    lf_t = lf.T[0:8, :]
    r = lax.broadcasted_iota(jnp.int32, (tm, tm), 0)
    c = lax.broadcasted_iota(jnp.int32, (tm, tm), 1)
    tri_u = jnp.where(r <= c, 1.0, 0.0).astype(BF16)
    hi = lf_t.astype(BF16).astype(F32)
    r1 = lf_t - hi
    mid = r1.astype(BF16).astype(F32)
    pieces = jnp.concatenate([hi, mid, r1 - mid], axis=0).astype(BF16)
    parts = jnp.dot(pieces, tri_u, preferred_element_type=F32)
    cum = parts[0:8] + parts[8:16] + parts[16:24] + carry_ref[:, 0:1]
    cum_ref[...] = cum * LOG2E
    carry_ref[...] = jnp.broadcast_to(cum[:, tm - 1:tm], carry_ref.shape)


def _in_proj(x, mod_l, norm_w, w_pad, fb_pad, tm):
    bsz, seq, d = x.shape
    tm = min(tm, seq)
    return pl.pallas_call(
        _inproj_kernel,
        grid=(bsz, seq // tm),
        in_specs=[
            pl.BlockSpec((None, tm, d), lambda b, i: (b, i, 0)),
            pl.BlockSpec((None, 6, d), lambda b, i: (b, 0, 0)),
            pl.BlockSpec((1, d), lambda b, i: (0, 0)),
            pl.BlockSpec((d, N_IN_PAD), lambda b, i: (0, 0)),
            pl.BlockSpec((1, LANES), lambda b, i: (0, 0)),
        ],
        out_specs=[
            pl.BlockSpec((None, tm, 1152), lambda b, i: (b, i, 0)),
            pl.BlockSpec((2, None, tm, LANES), lambda b, i: (0, b, i, 0)),
            pl.BlockSpec((None, tm, 1152), lambda b, i: (b, i, 0)),
            pl.BlockSpec((None, tm, HGRN_WIDTH), lambda b, i: (b, i, 0)),
            pl.BlockSpec((None, 8, tm), lambda b, i: (b, 0, i)),
        ],
        out_shape=[
            jax.ShapeDtypeStruct((bsz, seq, 1152), BF16),
            jax.ShapeDtypeStruct((2, bsz, seq, LANES), F32),
            jax.ShapeDtypeStruct((bsz, seq, 1152), BF16),
            jax.ShapeDtypeStruct((bsz, seq, HGRN_WIDTH), F32),
            jax.ShapeDtypeStruct((bsz, 8, seq), F32),
        ],
        scratch_shapes=[pltpu.VMEM((8, LANES), F32)],
        compiler_params=_cparams(("arbitrary", "arbitrary")),
        name="in_proj",
    )(x, mod_l, norm_w, w_pad, fb_pad)


def _fox_kernel(q_ref, k_ref, v_ref, cum_ref, o_ref, s_sc, p_sc, al_sc, m_sc, l_sc, acc_sc, d_sc,
                *, tq, tk, n_steps, unroll):
    p = pl.program_id(1)
    seq = q_ref.shape[0]
    nq = seq // tq
    lane = lax.broadcasted_iota(jnp.int32, (tq, LANES), 1)
    first = lane < HEAD_DIM

    def advance(i, j):
        last = j == (i * tq) // tk
        return jnp.minimum(i + last.astype(jnp.int32), nq - 1), jnp.where(last, 0, j + 1)

    def stage_qk(i, j):
        q = q_ref[pl.ds(pl.multiple_of(i * tq, tq), tq), :]
        zero = jnp.zeros_like(q)
        q2 = jnp.concatenate([jnp.where(first, q, zero), jnp.where(first, zero, q)], axis=0)
        k0 = pl.multiple_of(j * tk, tk)
        s = lax.dot_general(q2, k_ref[pl.ds(k0, tk), :], NT_DIMS, preferred_element_type=F32)
        bias_a = cum_ref[pl.ds(2 * p, 1), pl.ds(k0, tk)]
        bias_b = cum_ref[pl.ds(2 * p + 1, 1), pl.ds(k0, tk)]
        is_last = j == (i * tq) // tk
        mask = d_sc[jnp.where(is_last, 1 + i - j * (tk // tq), 0)]
        s_sc[...] = jnp.concatenate([s[:tq] + (mask - bias_a), s[tq:] + (mask - bias_b)], axis=0)

    def stage_softmax(j):
        cap = jnp.where(j == 0, NEG_BIG, -NEG_BIG)
        for r0 in range(0, 2 * tq, SOFTMAX_ROWS):
            rs = slice(r0, r0 + SOFTMAX_ROWS)
            m_prev = jnp.minimum(m_sc[rs, :], cap)
            s = s_sc[rs, :]
            m_next = jnp.maximum(m_prev, jnp.max(s, axis=1, keepdims=True))
            pr = jnp.exp2(s - jnp.tile(m_next, (1, tk // LANES)))
            alpha = jnp.exp2(m_prev - m_next)
            l_sc[rs, :] = alpha * l_sc[rs, :] + jnp.sum(pr, axis=1, keepdims=True)
            m_sc[rs, :] = m_next
            p_sc[rs, :] = pr.astype(BF16)
            al_sc[rs, :] = alpha

    def stage_pv(i, j):
        k0 = pl.multiple_of(j * tk, tk)
        acc = acc_sc[...] * al_sc[...] + jnp.dot(p_sc[...], v_ref[pl.ds(k0, tk), :],
                                                  preferred_element_type=F32)
        acc_sc[...] = acc
        o = acc / l_sc[...]
        o_ref[pl.ds(pl.multiple_of(i * tq, tq), tq), :] = jnp.where(first, o[:tq], o[tq:])

    col_minus_row = (lax.broadcasted_iota(jnp.int32, (tq, tk), 1)
                     - lax.broadcasted_iota(jnp.int32, (tq, tk), 0))
    d_sc[0] = jnp.zeros((tq, tk), F32)
    for k in range(tk // tq):
        d_sc[1 + k] = jnp.where(col_minus_row <= k * tq, 0.0, NEG_BIG)
    s_sc[...] = jnp.zeros(s_sc.shape, F32)
    p_sc[...] = jnp.zeros(p_sc.shape, BF16)
    al_sc[...] = jnp.ones(al_sc.shape, F32)
    m_sc[...] = jnp.full(m_sc.shape, NEG_BIG, F32)
    l_sc[...] = jnp.ones(l_sc.shape, F32)
    acc_sc[...] = jnp.zeros(acc_sc.shape, F32)

    def body(t, carry):
        ia, ja, ib, jb, ic, jc = carry
        stage_pv(ic, jc)
        stage_softmax(jb)
        stage_qk(ia, ja)
        na, nja = advance(ia, ja)
        return na, nja, ia, ja, ib, jb

    zero = jnp.int32(0)
    lax.fori_loop(0, n_steps + 2, body, (zero, zero, zero, zero, zero, zero), unroll=unroll)


def _fox_attention(qkv, cum_t, tq, tk):
    bsz, seq, _ = qkv.shape
    tq = min(tq, seq)
    tk = min(max(tk, tq), seq)
    assert tk % tq == 0 and seq % tk == 0
    npair = FOX_HEADS // 2
    n_steps = sum((i * tq) // tk + 1 for i in range(seq // tq))
    kern = functools.partial(_fox_kernel, tq=tq, tk=tk, n_steps=n_steps,
                             unroll=2 if n_steps % 2 == 0 else 1)
    col = lambda off: pl.BlockSpec((None, seq, LANES), lambda b, p: (b, 0, off + p))
    return pl.pallas_call(
        kern,
        grid=(bsz, npair),
        in_specs=[col(0), col(npair), col(2 * npair),
                  pl.BlockSpec((None, 8, seq), lambda b, p: (b, 0, 0))],
        out_specs=col(0),
        out_shape=jax.ShapeDtypeStruct((bsz, seq, FOX_WIDTH), F32),
        scratch_shapes=[
            pltpu.VMEM((2 * tq, tk), F32),
            pltpu.VMEM((2 * tq, tk), BF16),
            pltpu.VMEM((2 * tq, LANES), F32),
            pltpu.VMEM((2 * tq, LANES), F32),
            pltpu.VMEM((2 * tq, LANES), F32),
            pltpu.VMEM((2 * tq, LANES), F32),
            pltpu.VMEM((1 + tk // tq, tq, tk), F32),
        ],
        compiler_params=_cparams(("arbitrary", "arbitrary")),
        name="fox_attn",
    )(qkv, qkv, qkv, cum_t)


def _gelu_tanh(x):
    return 0.5 * x * (1.0 + jnp.tanh(math.sqrt(2.0 / math.pi) * (x + 0.044715 * (x * x * x))))


def _s5_kernel(su_ref, bm_ref, ar_ref, ai_ref, cm_ref, d_ref, gw_ref, gb_ref, o_ref,
               utb, bu, ytb, st):
    i = pl.program_id(0)
    nb = su_ref.shape[1]
    ln = su_ref.shape[2]
    ns = S5_NSTATE

    @pl.when(i == 0)
    def _():
        st[...] = jnp.zeros_like(st)

    for k in range(2):
        for b in range(nb):
            utb[k, pl.ds(b, ln, stride=nb), :] = su_ref[k, b]
    u = jnp.concatenate([utb[0], utb[1]], axis=1)
    bu[...] = jnp.dot(u.astype(BF16), bm_ref[...], preferred_element_type=F32)

    ar = ar_ref[...]
    ai = ai_ref[...]

    def body(t, carry):
        re, im = carry
        r0 = pl.multiple_of(t * nb, nb)
        bur = bu[pl.ds(r0, nb), 0:ns]
        bui = bu[pl.ds(r0, nb), ns:2 * ns]
        nre = ar * re - ai * im + bur
        nim = ar * im + ai * re + bui
        bu[pl.ds(r0, nb), 0:ns] = nre
        bu[pl.ds(r0, nb), ns:2 * ns] = nim
        return nre, nim

    re, im = lax.fori_loop(0, ln, body, (st[:, 0:ns], st[:, ns:2 * ns]))
    st[:, 0:ns] = re
    st[:, ns:2 * ns] = im

    y = jnp.dot(bu[...].astype(BF16), cm_ref[...], preferred_element_type=F32) + d_ref[...] * u
    y = _gelu_tanh(y)
    gate = _sigmoid(jnp.dot(y.astype(BF16), gw_ref[...], preferred_element_type=F32) + gb_ref[...])
    out = y * gate
    ytb[0] = out[:, :LANES]
    ytb[1] = out[:, LANES:]
    for k in range(2):
        for b in range(nb):
            o_ref[b, :, k * LANES:(k + 1) * LANES] = ytb[k, pl.ds(b, ln, stride=nb), :]


def _s5_mixer(su, bmat, ar, ai, cmat, dvec, glu_w, glu_b, ln):
    _, bsz, seq, _ = su.shape
    ln = min(ln, seq)
    rows = bsz * ln
    const = lambda shape: pl.BlockSpec(shape, lambda i: (0,) * len(shape))
    return pl.pallas_call(
        _s5_kernel,
        grid=(seq // ln,),
        in_specs=[
            pl.BlockSpec((2, bsz, ln, LANES), lambda i: (0, 0, i, 0)),
            const((S5_WIDTH, 2 * S5_NSTATE)),
            const((bsz, S5_NSTATE)),
            const((bsz, S5_NSTATE)),
            const((2 * S5_NSTATE, S5_WIDTH)),
            const((1, S5_WIDTH)),
            const((S5_WIDTH, S5_WIDTH)),
            const((1, S5_WIDTH)),
        ],
        out_specs=pl.BlockSpec((bsz, ln, S5_WIDTH), lambda i: (0, i, 0)),
        out_shape=jax.ShapeDtypeStruct((bsz, seq, S5_WIDTH), F32),
        scratch_shapes=[
            pltpu.VMEM((2, rows, LANES), F32),
            pltpu.VMEM((rows, 2 * S5_NSTATE), F32),
            pltpu.VMEM((2, rows, LANES), F32),
            pltpu.VMEM((bsz, 2 * S5_NSTATE), F32),
        ],
        compiler_params=_cparams(("arbitrary",)),
        name="s5_mixer",
    )(su, bmat, ar, ai, cmat, dvec, glu_w, glu_b)


def _block_row(a, blk, r):
    t = a.shape[0]
    if blk >= 8:
        a3 = a.reshape(t // blk, blk, LANES)
        return jnp.broadcast_to(a3[:, r:r + 1, :], a3.shape).reshape(t, LANES)
    if blk == 4:
        a3 = a.reshape(t // 8, 8, LANES)
        sub = lax.broadcasted_iota(jnp.int32, a3.shape, 1)
        res = jnp.where(sub < 4, jnp.broadcast_to(a3[:, r:r + 1, :], a3.shape),
                        jnp.broadcast_to(a3[:, 4 + r:5 + r, :], a3.shape))
        return res.reshape(t, LANES)
    row = lax.broadcasted_iota(jnp.int32, a.shape, 0)
    res = a
    for q in range(blk):
        if q != r:
            res = jnp.where(row % blk == q, pltpu.roll(a, (q - r) % t, axis=0), res)
    return res


def _hgrn_kernel(hq_ref, hi_ref, hg_ref, hf_ref, lb_ref, omlb_ref, nw_ref, o_ref, lvl_sc, mask_sc, lo_sc,
                 *, tc):
    seq = hf_ref.shape[0]
    nlev = tc.bit_length() - 1
    lane = lax.broadcasted_iota(jnp.int32, (tc, LANES), 1)
    first = lane < HEAD_DIM
    row = lax.broadcasted_iota(jnp.int32, (tc, LANES), 0)
    rr = lax.broadcasted_iota(jnp.int32, (tc, tc), 0)
    cc = lax.broadcasted_iota(jnp.int32, (tc, tc), 1)
    diff = rr ^ cc
    bits = jnp.zeros((tc, tc), jnp.int32)
    for i in range(nlev):
        bits = bits + jnp.where((diff >> i) != 0, 1, 0)
    lvl_sc[...] = jnp.where(rr >= cc, bits, -1)
    for i in range(nlev):
        upper = ((row >> i) & 1) == 1
        mask_sc[i, 0] = jnp.where(jnp.logical_and(upper, first), 1.0, 0.0).astype(BF16)
        mask_sc[i, 1] = jnp.where(jnp.logical_and(upper, jnp.logical_not(first)), 1.0, 0.0).astype(BF16)
        mask_sc[i, 2] = jnp.where(upper, 0.0, 1.0).astype(BF16)
        lo_sc[i] = jnp.where(upper, 0.0, 1.0)
    sr = lax.broadcasted_iota(jnp.int32, (LANES, LANES), 0)
    sc = lax.broadcasted_iota(jnp.int32, (LANES, LANES), 1)
    blockdiag = (sr < HEAD_DIM) == (sc < HEAD_DIM)
    lb = lb_ref[...]
    omlb = omlb_ref[...]
    nw = nw_ref[...]

    def chunk(c, st):
        r0 = pl.multiple_of(c * tc, tc)
        z = hf_ref[pl.ds(r0, tc), :]
        q = _silu(hq_ref[pl.ds(r0, tc), :].astype(F32))
        v = hi_ref[pl.ds(r0, tc), :]
        g = hg_ref[pl.ds(r0, tc), :].astype(F32)
        e = jnp.exp(-jnp.abs(z))
        s_big = 1.0 / (1.0 + e)
        s_small = e * s_big
        pos = z >= 0.0
        f = lb + omlb * jnp.where(pos, s_big, s_small)
        kk = omlb * jnp.where(pos, s_small, s_big)

        def scores(q_a, q_b, km):
            return lax.dot_general(jnp.concatenate([q_a, q_b], axis=0), km, NT_DIMS,
                                   preferred_element_type=F32)

        s = scores(jnp.where(first, q, 0.0).astype(BF16), jnp.where(first, 0.0, q).astype(BF16),
                   kk.astype(BF16))
        on_diag = lvl_sc[...] == 0
        tot_a = jnp.where(on_diag, s[:tc], 0.0)
        tot_b = jnp.where(on_diag, s[tc:], 0.0)
        a_m = f
        b_m = jnp.ones_like(f)
        for i in range(nlev):
            m = 1 << i
            qa = (q * a_m).astype(BF16)
            s = scores(qa * mask_sc[i, 0], qa * mask_sc[i, 1], (kk * b_m).astype(BF16) * mask_sc[i, 2])
            sel = lvl_sc[...] == i + 1
            tot_a = jnp.where(sel, s[:tc], tot_a)
            tot_b = jnp.where(sel, s[tc:], tot_b)
            lower_total = _block_row(a_m, 2 * m, m - 1)
            upper_total = _block_row(a_m, 2 * m, 2 * m - 1)
            upper = lo_sc[i] < 0.5
            a_m = jnp.where(upper, a_m * lower_total, a_m)
            b_m = jnp.where(upper, b_m, b_m * upper_total)

        o = lax.dot_general((q * a_m).astype(BF16), st.astype(BF16), NT_DIMS, preferred_element_type=F32)
        pv = jnp.dot(jnp.concatenate([tot_a, tot_b], axis=0).astype(BF16), v, preferred_element_type=F32)
        o = o + jnp.where(first, pv[:tc], pv[tc:])

        upd = jnp.dot(v.astype(F32).T.astype(BF16), (kk * b_m).astype(BF16), preferred_element_type=F32)
        st_new = st * a_m[tc - 1:tc, :] + jnp.where(blockdiag, upd, 0.0)

        o2 = o * o
        s_a = jnp.sum(jnp.where(first, o2, 0.0), axis=-1, keepdims=True)
        s_b = jnp.sum(jnp.where(first, 0.0, o2), axis=-1, keepdims=True)
        ms = jnp.where(first, s_a, s_b) * (1.0 / HEAD_DIM)
        o_ref[pl.ds(r0, tc), :] = o * lax.rsqrt(ms + EPS) * nw * _silu(g)
        return st_new

    lax.fori_loop(0, seq // tc, chunk, jnp.zeros((LANES, LANES), F32), unroll=2)


def _hgrn_mixer(hqig, hf, lb, om_lb, nw2, tc):
    bsz, seq, _ = hf.shape
    tc = min(tc, seq)
    npair = HGRN_WIDTH // LANES
    kern = functools.partial(_hgrn_kernel, tc=tc)
    col = lambda off: pl.BlockSpec((None, seq, LANES), lambda b, p: (b, 0, off + p))
    par = pl.BlockSpec((1, LANES), lambda b, p: (0, p))
    return pl.pallas_call(
        kern,
        grid=(bsz, npair),
        in_specs=[col(0), col(npair), col(2 * npair), col(0), par, par,
                  pl.BlockSpec((1, LANES), lambda b, p: (0, 0))],
        out_specs=col(0),
        out_shape=jax.ShapeDtypeStruct((bsz, seq, HGRN_WIDTH), F32),
        scratch_shapes=[
            pltpu.VMEM((tc, tc), jnp.int32),
            pltpu.VMEM((tc.bit_length() - 1, 3, tc, LANES), BF16),
            pltpu.VMEM((tc.bit_length() - 1, tc, LANES), F32),
        ],
        compiler_params=_cparams(("arbitrary", "arbitrary")),
        name="hgrn_mixer",
    )(hqig, hqig, hqig, hf, lb, om_lb, nw2)


def _outproj_kernel(fox_ref, s5_ref, hg_ref, x_ref, mod_ref, nf_ref, ns_ref, w_ref, o_ref):
    of = (_rms(fox_ref[...]) * nf_ref[...]).astype(BF16)
    os5 = (_rms(s5_ref[...]) * ns_ref[...]).astype(BF16)
    oh = hg_ref[...].astype(BF16)
    a, b = FOX_WIDTH, FOX_WIDTH + S5_WIDTH
    mix = jnp.dot(of, w_ref[0:a, :], preferred_element_type=F32)
    mix = mix + jnp.dot(os5, w_ref[a:b, :], preferred_element_type=F32)
    mix = mix + jnp.dot(oh, w_ref[b:, :], preferred_element_type=F32)
    o_ref[...] = x_ref[...] + mod_ref[2:3, :] * mix


def _out_proj(o_fox, o_s5, o_hgrn, x, mod_l, nf, ns, w_out, tm):
    bsz, seq, d = x.shape
    tm = min(tm, seq)
    row = lambda w: pl.BlockSpec((None, tm, w), lambda b, i: (b, i, 0))
    return pl.pallas_call(
        _outproj_kernel,
        grid=(bsz, seq // tm),
        in_specs=[
            row(FOX_WIDTH), row(S5_WIDTH), row(HGRN_WIDTH), row(d),
            pl.BlockSpec((None, 6, d), lambda b, i: (b, 0, 0)),
            pl.BlockSpec((1, FOX_WIDTH), lambda b, i: (0, 0)),
            pl.BlockSpec((1, S5_WIDTH), lambda b, i: (0, 0)),
            pl.BlockSpec((d, d), lambda b, i: (0, 0)),
        ],
        out_specs=row(d),
        out_shape=jax.ShapeDtypeStruct((bsz, seq, d), F32),
        compiler_params=_cparams(("arbitrary", "arbitrary")),
        name="out_proj",
    )(o_fox, o_s5, o_hgrn, x, mod_l, nf, ns, w_out)


def _routing(logits_t):
    mx = jnp.max(logits_t, axis=0, keepdims=True)
    ex = jnp.exp(logits_t - mx)
    probs = ex / jnp.sum(ex, axis=0, keepdims=True)
    p = [probs[e:e + 1, :] for e in range(N_EXPERTS)]
    epg = N_EXPERTS // N_GROUPS
    scores = []
    for g in range(N_GROUPS):
        a, b, c, d = p[epg * g:epg * g + epg]
        hi1, lo1 = jnp.maximum(a, b), jnp.minimum(a, b)
        hi2, lo2 = jnp.maximum(c, d), jnp.minimum(c, d)
        top1 = jnp.maximum(hi1, hi2)
        top2 = jnp.maximum(jnp.minimum(hi1, hi2), jnp.maximum(lo1, lo2))
        scores.append(top1 + top2)
    best = jnp.zeros_like(scores[0], dtype=jnp.int32)
    bs = scores[0]
    for g in range(1, N_GROUPS):
        upd = scores[g] > bs
        best = jnp.where(upd, g, best)
        bs = jnp.where(upd, scores[g], bs)
    ig = []
    for j in range(epg):
        val = p[j]
        for g in range(1, N_GROUPS):
            val = jnp.where(best == g, p[epg * g + j], val)
        ig.append(val)
    i1 = jnp.zeros_like(best)
    w1 = ig[0]
    for j in range(1, epg):
        upd = ig[j] > w1
        i1 = jnp.where(upd, j, i1)
        w1 = jnp.where(upd, ig[j], w1)
    i2 = jnp.zeros_like(best)
    w2 = jnp.full_like(w1, -1.0)
    for j in range(epg):
        upd = jnp.logical_and(i1 != j, ig[j] > w2)
        i2 = jnp.where(upd, j, i2)
        w2 = jnp.where(upd, ig[j], w2)
    den = w1 + w2
    tw1 = w1 / den
    tw2 = w2 / den
    e1 = best * epg + i1
    e2 = best * epg + i2
    return [jnp.where(e1 == e, tw1, 0.0) + jnp.where(e2 == e, tw2, 0.0) for e in range(N_EXPERTS)], best


def _moe_kernel(x_ref, mod_ref, nw_ref, rw_ref, rb_ref, wg_ref, wu_ref, wd_ref, fnw_ref,
                o_ref, h_sc, comb_sc, combt_sc, hs_sc, ws_sc, ys_sc, acc_sc, cnt_sm, *, final_norm):
    grp = pl.program_id(2)
    tm, d = x_ref.shape
    epg = N_EXPERTS // N_GROUPS
    sub, cap = MOE_SUB, MOE_CAP
    ns = tm // sub
    gc = N_GROUPS * cap
    row_oh, row_slot = N_EXPERTS, N_EXPERTS + 8

    @pl.when(grp == 0)
    def _():
        h = _rms(x_ref[...]) * nw_ref[...]
        h = h * (1.0 + mod_ref[4:5, :]) + mod_ref[3:4, :]
        hb = h.astype(BF16)
        h_sc[...] = hb
        h_lo = (h - hb.astype(F32)).astype(BF16)
        hi_part = jnp.dot(hb, rw_ref[...], preferred_element_type=F32)
        lo_part = jnp.dot(h_lo, rw_ref[:, 0:LANES], preferred_element_type=F32)
        logits = hi_part[:, 0:LANES] + hi_part[:, LANES:] + lo_part + rb_ref[...]
        rows, best = _routing(logits.T[0:N_EXPERTS, :])
        combt_sc[...] = jnp.zeros_like(combt_sc)
        for j in range(N_EXPERTS):
            combt_sc[j:j + 1, :] = rows[j]
        for g in range(N_GROUPS):
            combt_sc[row_oh + g:row_oh + g + 1, :] = jnp.where(best == g, 1.0, 0.0)
        oh8 = combt_sc[row_oh:row_oh + 8, :]
        r = lax.broadcasted_iota(jnp.int32, (sub, sub), 0)
        c = lax.broadcasted_iota(jnp.int32, (sub, sub), 1)
        tri_u = jnp.where(r <= c, 1.0, 0.0).astype(BF16)
        incl = [jnp.dot(oh8[:, s * sub:(s + 1) * sub].astype(BF16), tri_u, preferred_element_type=F32)
                for s in range(ns)]
        cmax = incl[0][:, sub - 1:sub]
        for s in range(1, ns):
            cmax = jnp.maximum(cmax, incl[s][:, sub - 1:sub])
        for g in range(N_GROUPS):
            cnt_sm[g] = jnp.max(cmax[g:g + 1, :]).astype(jnp.int32)
        over = jnp.where(cmax > cap, 1.0, 0.0)
        best_f = best.astype(F32)
        for s in range(ns):
            cols = slice(s * sub, (s + 1) * sub)
            oh_s = oh8[:, cols]
            rank = jnp.sum(oh_s * incl[s], axis=0, keepdims=True) - 1.0
            dense = jnp.sum(oh_s * over, axis=0, keepdims=True)
            combt_sc[row_slot:row_slot + 1, cols] = jnp.where(dense > 0.5, -1.0, best_f[:, cols] * cap + rank)
        comb_sc[...] = combt_sc[...].T
        for s in range(ns):
            toks = slice(s * sub, (s + 1) * sub)
            slot_row = combt_sc[row_slot:row_slot + 1, toks].astype(jnp.int32)
            ri = lax.broadcasted_iota(jnp.int32, (gc, sub), 0)
            p = jnp.where(ri == slot_row, 1.0, 0.0).astype(BF16)
            hs_all = jnp.dot(p, h_sc[toks, :], preferred_element_type=F32).astype(BF16)
            cs = comb_sc[toks, :]
            c_hi = cs.astype(BF16)
            c_lo = (cs - c_hi.astype(F32)).astype(BF16)
            ws_all = jnp.dot(p, jnp.concatenate([c_hi, c_lo], axis=1), preferred_element_type=F32)
            ws_all = ws_all[:, 0:LANES] + ws_all[:, LANES:]
            for g in range(N_GROUPS):
                hs_sc[g, s * cap:(s + 1) * cap, :] = hs_all[g * cap:(g + 1) * cap, :]
                ws_sc[g, s * cap:(s + 1) * cap, :] = ws_all[g * cap:(g + 1) * cap, :]
        acc_sc[...] = jnp.zeros_like(acc_sc)

    def experts(h_rows, w_rows):
        lane = lax.broadcasted_iota(jnp.int32, w_rows.shape, 1)
        hid = []
        for j in range(epg):
            gate = jnp.dot(h_rows, wg_ref[j], preferred_element_type=F32)
            up = jnp.dot(h_rows, wu_ref[j], preferred_element_type=F32)
            w_e = jnp.sum(jnp.where(lane == grp * epg + j, w_rows, 0.0), axis=-1, keepdims=True)
            hid.append((_silu(gate) * up * w_e).astype(BF16))
        wd = wd_ref[...].reshape(epg * D_FF, d)
        return jnp.dot(jnp.concatenate(hid, axis=1), wd, preferred_element_type=F32)

    cnt = cnt_sm[grp]

    @pl.when(cnt <= cap)
    def _():
        ys = experts(hs_sc[grp], ws_sc[grp]).astype(BF16)
        for s in range(ns):
            ys_sc[s, grp] = ys[s * cap:(s + 1) * cap, :]

    @pl.when(cnt > cap)
    def _():
        acc_sc[...] += experts(h_sc[...], comb_sc[...])
        for s in range(ns):
            ys_sc[s, grp] = jnp.zeros((cap, d), BF16)

    @pl.when(grp == N_GROUPS - 1)
    def _():
        for s in range(ns):
            toks = slice(s * sub, (s + 1) * sub)
            slot_col = comb_sc[toks, row_slot:row_slot + 1].astype(jnp.int32)
            li = lax.broadcasted_iota(jnp.int32, (sub, gc), 1)
            pt = jnp.where(li == slot_col, 1.0, 0.0).astype(BF16)
            y = jnp.dot(pt, ys_sc[s].reshape(gc, d), preferred_element_type=F32) + acc_sc[toks, :]
            y = x_ref[toks, :] + mod_ref[5:6, :] * y
            if final_norm:
                y = _rms(y) * fnw_ref[...]
            o_ref[toks, :] = y


def _moe(x, mod_l, norm_w, rw_pad, rb_pad, wg, wu, wd, fnw, tm, final_norm):
    bsz, seq, d = x.shape
    tm = min(tm, seq)
    assert tm % MOE_SUB == 0
    ns = tm // MOE_SUB
    epg = N_EXPERTS // N_GROUPS
    kern = functools.partial(_moe_kernel, final_norm=final_norm)
    return pl.pallas_call(
        kern,
        grid=(bsz, seq // tm, N_GROUPS),
        in_specs=[
            pl.BlockSpec((None, tm, d), lambda b, i, e: (b, i, 0)),
            pl.BlockSpec((None, 6, d), lambda b, i, e: (b, 0, 0)),
            pl.BlockSpec((1, d), lambda b, i, e: (0, 0)),
            pl.BlockSpec((d, 2 * LANES), lambda b, i, e: (0, 0)),
            pl.BlockSpec((1, LANES), lambda b, i, e: (0, 0)),
            pl.BlockSpec((epg, d, D_FF), lambda b, i, e: (e, 0, 0)),
            pl.BlockSpec((epg, d, D_FF), lambda b, i, e: (e, 0, 0)),
            pl.BlockSpec((epg, D_FF, d), lambda b, i, e: (e, 0, 0)),
            pl.BlockSpec((1, d), lambda b, i, e: (0, 0)),
        ],
        out_specs=pl.BlockSpec((None, tm, d), lambda b, i, e: (b, i, 0)),
        out_shape=jax.ShapeDtypeStruct((bsz, seq, d), F32),
        scratch_shapes=[
            pltpu.VMEM((tm, d), BF16),
            pltpu.VMEM((tm, LANES), F32),
            pltpu.VMEM((LANES, tm), F32),
            pltpu.VMEM((N_GROUPS, ns * MOE_CAP, d), BF16),
            pltpu.VMEM((N_GROUPS, ns * MOE_CAP, LANES), F32),
            pltpu.VMEM((ns, N_GROUPS, MOE_CAP, d), BF16),
            pltpu.VMEM((tm, d), F32),
            pltpu.SMEM((N_GROUPS,), jnp.int32),
        ],
        compiler_params=_cparams(("arbitrary", "arbitrary", "arbitrary")),
        name="moe",
    )(x, mod_l, norm_w, rw_pad, rb_pad, wg, wu, wd, fnw)


def _prep_w_in(w_in_l, f_bias_l):
    a = FOX_WIDTH
    o_ff = 3 * a
    o_su = o_ff + FOX_HEADS
    o_hq = o_su + S5_WIDTH
    hw = HGRN_WIDTH
    fq = w_in_l[:, 0:a] * (HEAD_DIM ** -0.5 * LOG2E)
    fk, fv = w_in_l[:, a:2 * a], w_in_l[:, 2 * a:3 * a]
    ff = jnp.pad(w_in_l[:, o_ff:o_su], ((0, 0), (0, LANES - FOX_HEADS)))
    su = w_in_l[:, o_su:o_hq]
    hq = w_in_l[:, o_hq:o_hq + hw]
    hf = w_in_l[:, o_hq + hw:o_hq + 2 * hw]
    hi = w_in_l[:, o_hq + 2 * hw:o_hq + 3 * hw]
    hg = w_in_l[:, o_hq + 3 * hw:o_hq + 4 * hw]
    w = jnp.concatenate([fq, fk, fv, su, hq, hi, hg, hf, ff], axis=1).astype(BF16)
    fb = jnp.pad(f_bias_l.astype(F32), (0, LANES - FOX_HEADS)).reshape(1, LANES)
    return w, fb


def _prep_s5(a_re, a_im, b_re, b_im, c_re, c_im, log_dt, bsz):
    lam = lax.complex(a_re.astype(F32), a_im.astype(F32))
    dt = jnp.exp(log_dt.astype(F32))[:, None]
    a_bar = jnp.exp(lam * dt)
    b_bar = ((a_bar - 1.0) / lam)[..., None] * lax.complex(b_re.astype(F32), b_im.astype(F32))
    eye = jnp.eye(S5_GROUPS, dtype=F32)
    bm_re = jnp.einsum('gpc,gh->gchp', jnp.real(b_bar), eye).reshape(S5_WIDTH, S5_NSTATE)
    bm_im = jnp.einsum('gpc,gh->gchp', jnp.imag(b_bar), eye).reshape(S5_WIDTH, S5_NSTATE)
    bmat = jnp.concatenate([bm_re, bm_im], axis=1).astype(BF16)
    cm_re = jnp.einsum('gcp,gh->gphc', c_re.astype(F32), eye).reshape(S5_NSTATE, S5_WIDTH)
    cm_im = jnp.einsum('gcp,gh->gphc', c_im.astype(F32), eye).reshape(S5_NSTATE, S5_WIDTH)
    cmat = jnp.concatenate([cm_re, -cm_im], axis=0).astype(BF16)
    ar = jnp.broadcast_to(jnp.real(a_bar).reshape(1, S5_NSTATE), (bsz, S5_NSTATE))
    ai = jnp.broadcast_to(jnp.imag(a_bar).reshape(1, S5_NSTATE), (bsz, S5_NSTATE))
    return bmat, ar, ai, cmat


def kernel(x, c, ada_w, ada_b, norm_mix_w, norm_ffn_w, w_in, fox_f_bias, s5_a_re, s5_a_im, s5_b_re, s5_b_im, s5_c_re, s5_c_im, s5_d, s5_log_dt, s5_glu_w, s5_glu_b, hgrn_lb_logits, hgrn_norm_w, branch_norm_fox, branch_norm_s5, w_out, router_w, router_b, moe_w_gate, moe_w_up, moe_w_down, final_norm_w):
    bsz, seq, d = x.shape
    depth = w_in.shape[0]
    lb_cum = jnp.cumsum(jax.nn.softmax(hgrn_lb_logits.astype(F32), axis=0), axis=0)
    lower = lb_cum - lb_cum[0:1]
    om_lb = 1.0 - lower

    mod = _ada_mod(c, ada_w, ada_b)
    rw32 = jnp.pad(router_w.astype(F32), ((0, 0), (0, LANES - N_EXPERTS)))
    rw_hi = rw32.astype(BF16)
    rw_pad = jnp.concatenate([rw_hi, (rw32 - rw_hi.astype(F32)).astype(BF16)], axis=1)
    rb_pad = jnp.pad(router_b.astype(F32), (0, LANES - N_EXPERTS)).reshape(1, LANES)
    fnw = final_norm_w.reshape(1, d).astype(F32)

    for l in range(depth):
        w_pad, fb_pad = _prep_w_in(w_in[l], fox_f_bias[l])
        qkv, su, hqig, hf, cum_t = _in_proj(x, mod[l], norm_mix_w[l].reshape(1, d), w_pad, fb_pad, tm=1024)
        o_fox = _fox_attention(qkv, cum_t, tq=256, tk=512)
        bmat, ar, ai, cmat = _prep_s5(s5_a_re[l], s5_a_im[l], s5_b_re[l], s5_b_im[l],
                                      s5_c_re[l], s5_c_im[l], s5_log_dt[l], bsz)
        o_s5 = _s5_mixer(su, bmat, ar, ai, cmat, s5_d[l].reshape(1, S5_WIDTH).astype(F32),
                         s5_glu_w[l].astype(BF16), s5_glu_b[l].reshape(1, S5_WIDTH).astype(F32), ln=128)
        nw2 = jnp.tile(hgrn_norm_w[l].astype(F32), 2).reshape(1, LANES)
        o_hgrn = _hgrn_mixer(hqig, hf, lower[l].reshape(1, -1), om_lb[l].reshape(1, -1), nw2, tc=128)
        x = _out_proj(o_fox, o_s5, o_hgrn, x, mod[l],
                      branch_norm_fox[l].reshape(1, -1).astype(F32),
                      branch_norm_s5[l].reshape(1, -1).astype(F32),
                      w_out[l].astype(BF16), tm=512)
        x = _moe(x, mod[l], norm_ffn_w[l].reshape(1, d).astype(F32), rw_pad, rb_pad,
                 moe_w_gate[l].astype(BF16), moe_w_up[l].astype(BF16), moe_w_down[l].astype(BF16),
                 fnw, tm=1024, final_norm=(l == depth - 1))
    return x
```

```python
import functools
import math

import jax
import jax.numpy as jnp
from jax import lax
from jax.experimental import pallas as pl
from jax.experimental.pallas import tpu as pltpu

F32 = jnp.float32
BF16 = jnp.bfloat16
HIGHEST = lax.Precision.HIGHEST

D_MODEL = 1024
FOX_HEADS = 6
HEAD_DIM = 64
FOX_WIDTH = 384
S5_WIDTH = 256
S5_GROUPS = 16
S5_CH = 16
S5_STATE = 64
S5_NSTATE = S5_GROUPS * S5_STATE
HGRN_WIDTH = 384
N_EXPERTS = 16
N_GROUPS = 4
D_FF = 256
EPS = 1e-6
LANES = 128
NEG_BIG = -1e30
LOG2E = math.log2(math.e)
SOFTMAX_ROWS = 64
MOE_SUB = 256
MOE_CAP = 96

C_QKV = (0, 1152)
C_SU = (1152, 1408)
C_HQIG = (1408, 2560)
C_HF = (2560, 2944)
C_FF = (2944, 3072)
N_IN_PAD = 3072

NT_DIMS = (((1,), (1,)), ((), ()))

VMEM_LIMIT = 56 * 1024 * 1024


def _cparams(sem):
    return pltpu.CompilerParams(dimension_semantics=sem, vmem_limit_bytes=VMEM_LIMIT)


def _sigmoid(x):
    return 1.0 / (1.0 + jnp.exp(-x))


def _silu(x):
    hx = 0.5 * x
    return hx + hx * jnp.tanh(hx)


def _log_sigmoid(x):
    return jnp.minimum(x, 0.0) - jnp.log1p(jnp.exp(-jnp.abs(x)))


def _rms(x):
    return x * lax.rsqrt(jnp.mean(x * x, axis=-1, keepdims=True) + EPS)


def _ada_kernel(c_ref, w_ref, b_ref, o_ref):
    c = c_ref[...]
    o_ref[...] = jnp.dot(_silu(c), w_ref[...], precision=HIGHEST,
                         preferred_element_type=F32) + b_ref[...]


def _ada_mod(c, ada_w, ada_b):
    depth, d, n6 = ada_w.shape
    bsz = c.shape[0]
    nblk = n6 // d
    out = pl.pallas_call(
        _ada_kernel,
        grid=(depth, nblk),
        in_specs=[
            pl.BlockSpec((bsz, d), lambda l, j: (0, 0)),
            pl.BlockSpec((None, d, d), lambda l, j: (l, 0, j)),
            pl.BlockSpec((None, 1, d), lambda l, j: (l, 0, j)),
        ],
        out_specs=pl.BlockSpec((None, bsz, d), lambda l, j: (l, 0, j)),
        out_shape=jax.ShapeDtypeStruct((depth, bsz, n6), F32),
        compiler_params=_cparams(("arbitrary", "arbitrary")),
        name="ada_mod",
    )(c, ada_w, ada_b.reshape(depth, 1, n6))
    return out.reshape(depth, bsz, nblk, d)


def _inproj_kernel(x_ref, mod_ref, nw_ref, w_ref, fb_ref,
                   qkv_ref, su_ref, hqig_ref, hf_ref, cum_ref, carry_ref):
    i = pl.program_id(1)
    tm = x_ref.shape[0]

    @pl.when(i == 0)
    def _():
        carry_ref[...] = jnp.zeros_like(carry_ref)

    h = _rms(x_ref[...]) * nw_ref[...]
    h = h * (1.0 + mod_ref[1:2, :]) + mod_ref[0:1, :]
    hb = h.astype(BF16)

    def proj(c):
        return jnp.dot(hb, w_ref[:, c[0]:c[1]], preferred_element_type=F32)

    qkv_ref[...] = proj(C_QKV).astype(BF16)
    su = proj(C_SU)
    su_ref[0] = su[:, :LANES]
    su_ref[1] = su[:, LANES:]
    hqig_ref[...] = proj(C_HQIG).astype(BF16)
    hf_ref[...] = proj(C_HF)

    lf = _log_sigmoid(proj(C_FF) + fb_ref[...])
    lf_t = lf.T[0:8, :]
    r = lax.broadcasted_iota(jnp.int32, (tm, tm), 0)
    c = lax.broadcasted_iota(jnp.int32, (tm, tm), 1)
    tri_u = jnp.where(r <= c, 1.0, 0.0).astype(BF16)
    hi = lf_t.astype(BF16).astype(F32)
    r1 = lf_t - hi
    mid = r1.astype(BF16).astype(F32)
    pieces = jnp.concatenate([hi, mid, r1 - mid], axis=0).astype(BF16)
    parts = jnp.dot(pieces, tri_u, preferred_element_type=F32)
    cum = parts[0:8] + parts[8:16] + parts[16:24] + carry_ref[:, 0:1]
    cum_ref[...] = cum * LOG2E
    carry_ref[...] = jnp.broadcast_to(cum[:, tm - 1:tm], carry_ref.shape)


def _in_proj(x, mod_l, norm_w, w_pad, fb_pad, tm):
    bsz, seq, d = x.shape
    tm = min(tm, seq)
    return pl.pallas_call(
        _inproj_kernel,
        grid=(bsz, seq // tm),
        in_specs=[
            pl.BlockSpec((None, tm, d), lambda b, i: (b, i, 0)),
            pl.BlockSpec((None, 6, d), lambda b, i: (b, 0, 0)),
            pl.BlockSpec((1, d), lambda b, i: (0, 0)),
            pl.BlockSpec((d, N_IN_PAD), lambda b, i: (0, 0)),
            pl.BlockSpec((1, LANES), lambda b, i: (0, 0)),
        ],
        out_specs=[
            pl.BlockSpec((None, tm, 1152), lambda b, i: (b, i, 0)),
            pl.BlockSpec((2, None, tm, LANES), lambda b, i: (0, b, i, 0)),
            pl.BlockSpec((None, tm, 1152), lambda b, i: (b, i, 0)),
            pl.BlockSpec((None, tm, HGRN_WIDTH), lambda b, i: (b, i, 0)),
            pl.BlockSpec((None, 8, tm), lambda b, i: (b, 0, i)),
        ],
        out_shape=[
            jax.ShapeDtypeStruct((bsz, seq, 1152), BF16),
            jax.ShapeDtypeStruct((2, bsz, seq, LANES), F32),
            jax.ShapeDtypeStruct((bsz, seq, 1152), BF16),
            jax.ShapeDtypeStruct((bsz, seq, HGRN_WIDTH), F32),
            jax.ShapeDtypeStruct((bsz, 8, seq), F32),
        ],
        scratch_shapes=[pltpu.VMEM((8, LANES), F32)],
        compiler_params=_cparams(("arbitrary", "arbitrary")),
        name="in_proj",
    )(x, mod_l, norm_w, w_pad, fb_pad)


def _fox_kernel(q_ref, k_ref, v_ref, cum_ref, o_ref, s_sc, p_sc, al_sc, m_sc, l_sc, acc_sc, d_sc,
                *, tq, tk, n_steps, unroll):
    p = pl.program_id(1)
    seq = q_ref.shape[0]
    nq = seq // tq
    lane = lax.broadcasted_iota(jnp.int32, (tq, LANES), 1)
    first = lane < HEAD_DIM

    def advance(i, j):
        last = j == (i * tq) // tk
        return jnp.minimum(i + last.astype(jnp.int32), nq - 1), jnp.where(last, 0, j + 1)

    def stage_qk(i, j):
        q = q_ref[pl.ds(pl.multiple_of(i * tq, tq), tq), :]
        zero = jnp.zeros_like(q)
        q2 = jnp.concatenate([jnp.where(first, q, zero), jnp.where(first, zero, q)], axis=0)
        k0 = pl.multiple_of(j * tk, tk)
        s = lax.dot_general(q2, k_ref[pl.ds(k0, tk), :], NT_DIMS, preferred_element_type=F32)
        bias_a = cum_ref[pl.ds(2 * p, 1), pl.ds(k0, tk)]
        bias_b = cum_ref[pl.ds(2 * p + 1, 1), pl.ds(k0, tk)]
        is_last = j == (i * tq) // tk
        mask = d_sc[jnp.where(is_last, 1 + i - j * (tk // tq), 0)]
        s_sc[...] = jnp.concatenate([s[:tq] + (mask - bias_a), s[tq:] + (mask - bias_b)], axis=0)

    def stage_softmax(j):
        cap = jnp.where(j == 0, NEG_BIG, -NEG_BIG)
        for r0 in range(0, 2 * tq, SOFTMAX_ROWS):
            rs = slice(r0, r0 + SOFTMAX_ROWS)
            m_prev = jnp.minimum(m_sc[rs, :], cap)
            s = s_sc[rs, :]
            m_next = jnp.maximum(m_prev, jnp.max(s, axis=1, keepdims=True))
            pr = jnp.exp2(s - jnp.tile(m_next, (1, tk // LANES)))
            alpha = jnp.exp2(m_prev - m_next)
            l_sc[rs, :] = alpha * l_sc[rs, :] + jnp.sum(pr, axis=1, keepdims=True)
            m_sc[rs, :] = m_next
            p_sc[rs, :] = pr.astype(BF16)
            al_sc[rs, :] = alpha

    def stage_pv(i, j):
        k0 = pl.multiple_of(j * tk, tk)
        acc = acc_sc[...] * al_sc[...] + jnp.dot(p_sc[...], v_ref[pl.ds(k0, tk), :],
                                                  preferred_element_type=F32)
        acc_sc[...] = acc
        o = acc / l_sc[...]
        o_ref[pl.ds(pl.multiple_of(i * tq, tq), tq), :] = jnp.where(first, o[:tq], o[tq:]).astype(BF16)

    col_minus_row = (lax.broadcasted_iota(jnp.int32, (tq, tk), 1)
                     - lax.broadcasted_iota(jnp.int32, (tq, tk), 0))
    d_sc[0] = jnp.zeros((tq, tk), F32)
    for k in range(tk // tq):
        d_sc[1 + k] = jnp.where(col_minus_row <= k * tq, 0.0, NEG_BIG)
    s_sc[...] = jnp.zeros(s_sc.shape, F32)
    p_sc[...] = jnp.zeros(p_sc.shape, BF16)
    al_sc[...] = jnp.ones(al_sc.shape, F32)
    m_sc[...] = jnp.full(m_sc.shape, NEG_BIG, F32)
    l_sc[...] = jnp.ones(l_sc.shape, F32)
    acc_sc[...] = jnp.zeros(acc_sc.shape, F32)

    def body(t, carry):
        ia, ja, ib, jb, ic, jc = carry
        stage_pv(ic, jc)
        stage_softmax(jb)
        stage_qk(ia, ja)
        na, nja = advance(ia, ja)
        return na, nja, ia, ja, ib, jb

    zero = jnp.int32(0)
    lax.fori_loop(0, n_steps + 2, body, (zero, zero, zero, zero, zero, zero), unroll=unroll)


def _fox_attention(qkv, cum_t, tq, tk):
    bsz, seq, _ = qkv.shape
    tq = min(tq, seq)
    tk = min(max(tk, tq), seq)
    assert tk % tq == 0 and seq % tk == 0
    npair = FOX_HEADS // 2
    n_steps = sum((i * tq) // tk + 1 for i in range(seq // tq))
    kern = functools.partial(_fox_kernel, tq=tq, tk=tk, n_steps=n_steps,
                             unroll=2 if n_steps % 2 == 0 else 1)
    col = lambda off: pl.BlockSpec((None, seq, LANES), lambda b, p: (b, 0, off + p))
    return pl.pallas_call(
        kern,
        grid=(bsz, npair),
        in_specs=[col(0), col(npair), col(2 * npair),
                  pl.BlockSpec((None, 8, seq), lambda b, p: (b, 0, 0))],
        out_specs=col(0),
        out_shape=jax.ShapeDtypeStruct((bsz, seq, FOX_WIDTH), BF16),
        scratch_shapes=[
            pltpu.VMEM((2 * tq, tk), F32),
            pltpu.VMEM((2 * tq, tk), BF16),
            pltpu.VMEM((2 * tq, LANES), F32),
            pltpu.VMEM((2 * tq, LANES), F32),
            pltpu.VMEM((2 * tq, LANES), F32),
            pltpu.VMEM((2 * tq, LANES), F32),
            pltpu.VMEM((1 + tk // tq, tq, tk), F32),
        ],
        compiler_params=_cparams(("arbitrary", "arbitrary")),
        name="fox_attn",
    )(qkv, qkv, qkv, cum_t)


def _gelu_tanh(x):
    return 0.5 * x * (1.0 + jnp.tanh(math.sqrt(2.0 / math.pi) * (x + 0.044715 * (x * x * x))))


def _s5_kernel(su_ref, bm_ref, ar_ref, ai_ref, cm_ref, d_ref, gw_ref, gb_ref, o_ref,
               utb, bu, ytb, st):
    i = pl.program_id(0)
    nb = su_ref.shape[1]
    ln = su_ref.shape[2]
    ns = S5_NSTATE

    @pl.when(i == 0)
    def _():
        st[...] = jnp.zeros_like(st)

    for k in range(2):
        for b in range(nb):
            utb[k, pl.ds(b, ln, stride=nb), :] = su_ref[k, b]
    u = jnp.concatenate([utb[0], utb[1]], axis=1)
    bu[...] = jnp.dot(u.astype(BF16), bm_ref[...], preferred_element_type=F32)

    ar = ar_ref[...]
    ai = ai_ref[...]

    def body(t, carry):
        re, im = carry
        r0 = pl.multiple_of(t * nb, nb)
        bur = bu[pl.ds(r0, nb), 0:ns]
        bui = bu[pl.ds(r0, nb), ns:2 * ns]
        nre = ar * re - ai * im + bur
        nim = ar * im + ai * re + bui
        bu[pl.ds(r0, nb), 0:ns] = nre
        bu[pl.ds(r0, nb), ns:2 * ns] = nim
        return nre, nim

    re, im = lax.fori_loop(0, ln, body, (st[:, 0:ns], st[:, ns:2 * ns]))
    st[:, 0:ns] = re
    st[:, ns:2 * ns] = im

    y = jnp.dot(bu[...].astype(BF16), cm_ref[...], preferred_element_type=F32) + d_ref[...] * u
    y = _gelu_tanh(y)
    gate = _sigmoid(jnp.dot(y.astype(BF16), gw_ref[...], preferred_element_type=F32) + gb_ref[...])
    out = y * gate
    ytb[0] = out[:, :LANES]
    ytb[1] = out[:, LANES:]
    for k in range(2):
        for b in range(nb):
            o_ref[b, :, k * LANES:(k + 1) * LANES] = ytb[k, pl.ds(b, ln, stride=nb), :].astype(BF16)


def _s5_mixer(su, bmat, ar, ai, cmat, dvec, glu_w, glu_b, ln):
    _, bsz, seq, _ = su.shape
    ln = min(ln, seq)
    rows = bsz * ln
    const = lambda shape: pl.BlockSpec(shape, lambda i: (0,) * len(shape))
    return pl.pallas_call(
        _s5_kernel,
        grid=(seq // ln,),
        in_specs=[
            pl.BlockSpec((2, bsz, ln, LANES), lambda i: (0, 0, i, 0)),
            const((S5_WIDTH, 2 * S5_NSTATE)),
            const((bsz, S5_NSTATE)),
            const((bsz, S5_NSTATE)),
            const((2 * S5_NSTATE, S5_WIDTH)),
            const((1, S5_WIDTH)),
            const((S5_WIDTH, S5_WIDTH)),
            const((1, S5_WIDTH)),
        ],
        out_specs=pl.BlockSpec((bsz, ln, S5_WIDTH), lambda i: (0, i, 0)),
        out_shape=jax.ShapeDtypeStruct((bsz, seq, S5_WIDTH), BF16),
        scratch_shapes=[
            pltpu.VMEM((2, rows, LANES), F32),
            pltpu.VMEM((rows, 2 * S5_NSTATE), F32),
            pltpu.VMEM((2, rows, LANES), F32),
            pltpu.VMEM((bsz, 2 * S5_NSTATE), F32),
        ],
        compiler_params=_cparams(("arbitrary",)),
        name="s5_mixer",
    )(su, bmat, ar, ai, cmat, dvec, glu_w, glu_b)


def _block_row(a, blk, r):
    t = a.shape[0]
    if blk >= 8:
        a3 = a.reshape(t // blk, blk, LANES)
        return jnp.broadcast_to(a3[:, r:r + 1, :], a3.shape).reshape(t, LANES)
    if blk == 4:
        a3 = a.reshape(t // 8, 8, LANES)
        sub = lax.broadcasted_iota(jnp.int32, a3.shape, 1)
        res = jnp.where(sub < 4, jnp.broadcast_to(a3[:, r:r + 1, :], a3.shape),
                        jnp.broadcast_to(a3[:, 4 + r:5 + r, :], a3.shape))
        return res.reshape(t, LANES)
    row = lax.broadcasted_iota(jnp.int32, a.shape, 0)
    res = a
    for q in range(blk):
        if q != r:
            res = jnp.where(row % blk == q, pltpu.roll(a, (q - r) % t, axis=0), res)
    return res


def _hgrn_kernel(hq_ref, hi_ref, hg_ref, hf_ref, lb_ref, omlb_ref, nw_ref, o_ref, lvl_sc, mask_sc, lo_sc,
                 *, tc):
    seq = hf_ref.shape[0]
    nlev = tc.bit_length() - 1
    lane = lax.broadcasted_iota(jnp.int32, (tc, LANES), 1)
    first = lane < HEAD_DIM
    row = lax.broadcasted_iota(jnp.int32, (tc, LANES), 0)
    rr = lax.broadcasted_iota(jnp.int32, (tc, tc), 0)
    cc = lax.broadcasted_iota(jnp.int32, (tc, tc), 1)
    diff = rr ^ cc
    bits = jnp.zeros((tc, tc), jnp.int32)
    for i in range(nlev):
        bits = bits + jnp.where((diff >> i) != 0, 1, 0)
    lvl_sc[...] = jnp.where(rr >= cc, bits, -1)
    for i in range(nlev):
        upper = ((row >> i) & 1) == 1
        mask_sc[i, 0] = jnp.where(jnp.logical_and(upper, first), 1.0, 0.0).astype(BF16)
        mask_sc[i, 1] = jnp.where(jnp.logical_and(upper, jnp.logical_not(first)), 1.0, 0.0).astype(BF16)
        mask_sc[i, 2] = jnp.where(upper, 0.0, 1.0).astype(BF16)
        lo_sc[i] = jnp.where(upper, 0.0, 1.0)
    sr = lax.broadcasted_iota(jnp.int32, (LANES, LANES), 0)
    sc = lax.broadcasted_iota(jnp.int32, (LANES, LANES), 1)
    blockdiag = (sr < HEAD_DIM) == (sc < HEAD_DIM)
    lb = lb_ref[...]
    omlb = omlb_ref[...]
    nw = nw_ref[...]

    def chunk(c, st):
        r0 = pl.multiple_of(c * tc, tc)
        z = hf_ref[pl.ds(r0, tc), :]
        q = _silu(hq_ref[pl.ds(r0, tc), :].astype(F32))
        v = hi_ref[pl.ds(r0, tc), :]
        g = hg_ref[pl.ds(r0, tc), :].astype(F32)
        e = jnp.exp(-jnp.abs(z))
        s_big = 1.0 / (1.0 + e)
        s_small = e * s_big
        pos = z >= 0.0
        f = lb + omlb * jnp.where(pos, s_big, s_small)
        kk = omlb * jnp.where(pos, s_small, s_big)

        def scores(q_a, q_b, km):
            return lax.dot_general(jnp.concatenate([q_a, q_b], axis=0), km, NT_DIMS,
                                   preferred_element_type=F32)

        s = scores(jnp.where(first, q, 0.0).astype(BF16), jnp.where(first, 0.0, q).astype(BF16),
                   kk.astype(BF16))
        on_diag = lvl_sc[...] == 0
        tot_a = jnp.where(on_diag, s[:tc], 0.0)
        tot_b = jnp.where(on_diag, s[tc:], 0.0)
        a_m = f
        b_m = jnp.ones_like(f)
        for i in range(nlev):
            m = 1 << i
            qa = (q * a_m).astype(BF16)
            s = scores(qa * mask_sc[i, 0], qa * mask_sc[i, 1], (kk * b_m).astype(BF16) * mask_sc[i, 2])
            sel = lvl_sc[...] == i + 1
            tot_a = jnp.where(sel, s[:tc], tot_a)
            tot_b = jnp.where(sel, s[tc:], tot_b)
            lower_total = _block_row(a_m, 2 * m, m - 1)
            upper_total = _block_row(a_m, 2 * m, 2 * m - 1)
            upper = lo_sc[i] < 0.5
            a_m = jnp.where(upper, a_m * lower_total, a_m)
            b_m = jnp.where(upper, b_m, b_m * upper_total)

        o = lax.dot_general((q * a_m).astype(BF16), st.astype(BF16), NT_DIMS, preferred_element_type=F32)
        pv = jnp.dot(jnp.concatenate([tot_a, tot_b], axis=0).astype(BF16), v, preferred_element_type=F32)
        o = o + jnp.where(first, pv[:tc], pv[tc:])

        upd = jnp.dot(v.astype(F32).T.astype(BF16), (kk * b_m).astype(BF16), preferred_element_type=F32)
        st_new = st * a_m[tc - 1:tc, :] + jnp.where(blockdiag, upd, 0.0)

        o2 = o * o
        s_a = jnp.sum(jnp.where(first, o2, 0.0), axis=-1, keepdims=True)
        s_b = jnp.sum(jnp.where(first, 0.0, o2), axis=-1, keepdims=True)
        ms = jnp.where(first, s_a, s_b) * (1.0 / HEAD_DIM)
        o_ref[pl.ds(r0, tc), :] = (o * lax.rsqrt(ms + EPS) * nw * _silu(g)).astype(BF16)
        return st_new

    lax.fori_loop(0, seq // tc, chunk, jnp.zeros((LANES, LANES), F32), unroll=2)


def _hgrn_mixer(hqig, hf, lb, om_lb, nw2, tc):
    bsz, seq, _ = hf.shape
    tc = min(tc, seq)
    npair = HGRN_WIDTH // LANES
    kern = functools.partial(_hgrn_kernel, tc=tc)
    col = lambda off: pl.BlockSpec((None, seq, LANES), lambda b, p: (b, 0, off + p))
    par = pl.BlockSpec((1, LANES), lambda b, p: (0, p))
    return pl.pallas_call(
        kern,
        grid=(bsz, npair),
        in_specs=[col(0), col(npair), col(2 * npair), col(0), par, par,
                  pl.BlockSpec((1, LANES), lambda b, p: (0, 0))],
        out_specs=col(0),
        out_shape=jax.ShapeDtypeStruct((bsz, seq, HGRN_WIDTH), BF16),
        scratch_shapes=[
            pltpu.VMEM((tc, tc), jnp.int32),
            pltpu.VMEM((tc.bit_length() - 1, 3, tc, LANES), BF16),
            pltpu.VMEM((tc.bit_length() - 1, tc, LANES), F32),
        ],
        compiler_params=_cparams(("arbitrary", "arbitrary")),
        name="hgrn_mixer",
    )(hqig, hqig, hqig, hf, lb, om_lb, nw2)


def _routing(logits_t):
    mx = jnp.max(logits_t, axis=0, keepdims=True)
    ex = jnp.exp(logits_t - mx)
    probs = ex / jnp.sum(ex, axis=0, keepdims=True)
    p = [probs[e:e + 1, :] for e in range(N_EXPERTS)]
    epg = N_EXPERTS // N_GROUPS
    scores = []
    for g in range(N_GROUPS):
        a, b, c, d = p[epg * g:epg * g + epg]
        hi1, lo1 = jnp.maximum(a, b), jnp.minimum(a, b)
        hi2, lo2 = jnp.maximum(c, d), jnp.minimum(c, d)
        top1 = jnp.maximum(hi1, hi2)
        top2 = jnp.maximum(jnp.minimum(hi1, hi2), jnp.maximum(lo1, lo2))
        scores.append(top1 + top2)
    best = jnp.zeros_like(scores[0], dtype=jnp.int32)
    bs = scores[0]
    for g in range(1, N_GROUPS):
        upd = scores[g] > bs
        best = jnp.where(upd, g, best)
        bs = jnp.where(upd, scores[g], bs)
    ig = []
    for j in range(epg):
        val = p[j]
        for g in range(1, N_GROUPS):
            val = jnp.where(best == g, p[epg * g + j], val)
        ig.append(val)
    i1 = jnp.zeros_like(best)
    w1 = ig[0]
    for j in range(1, epg):
        upd = ig[j] > w1
        i1 = jnp.where(upd, j, i1)
        w1 = jnp.where(upd, ig[j], w1)
    i2 = jnp.zeros_like(best)
    w2 = jnp.full_like(w1, -1.0)
    for j in range(epg):
        upd = jnp.logical_and(i1 != j, ig[j] > w2)
        i2 = jnp.where(upd, j, i2)
        w2 = jnp.where(upd, ig[j], w2)
    den = w1 + w2
    tw1 = w1 / den
    tw2 = w2 / den
    e1 = best * epg + i1
    e2 = best * epg + i2
    return [jnp.where(e1 == e, tw1, 0.0) + jnp.where(e2 == e, tw2, 0.0) for e in range(N_EXPERTS)], best


def _moe_kernel(fox_ref, s5_ref, hg_ref, x_ref, mod_ref, nf_ref, ns_ref, wo_ref,
                nw_ref, rw_ref, rb_ref, wg_ref, wu_ref, wd_ref, fnw_ref,
                o_ref, h_sc, comb_sc, combt_sc, hs_sc, ws_sc, ys_sc, acc_sc, cnt_sm, *, final_norm):
    grp = pl.program_id(2)
    tm, d = x_ref.shape
    epg = N_EXPERTS // N_GROUPS
    sub, cap = MOE_SUB, MOE_CAP
    ns = tm // sub
    gc = N_GROUPS * cap
    row_oh, row_slot = N_EXPERTS, N_EXPERTS + 8

    @pl.when(grp == 0)
    def _():
        a, b = FOX_WIDTH, FOX_WIDTH + S5_WIDTH
        for s in range(ns):
            toks = slice(s * sub, (s + 1) * sub)
            of = (_rms(fox_ref[toks, :].astype(F32)) * nf_ref[...]).astype(BF16)
            os5 = (_rms(s5_ref[toks, :].astype(F32)) * ns_ref[...]).astype(BF16)
            mix = jnp.dot(of, wo_ref[0:a, :], preferred_element_type=F32)
            mix = mix + jnp.dot(os5, wo_ref[a:b, :], preferred_element_type=F32)
            mix = mix + jnp.dot(hg_ref[toks, :], wo_ref[b:, :], preferred_element_type=F32)
            o_ref[toks, :] = x_ref[toks, :] + mod_ref[2:3, :] * mix
        h = _rms(o_ref[...]) * nw_ref[...]
        h = h * (1.0 + mod_ref[4:5, :]) + mod_ref[3:4, :]
        hb = h.astype(BF16)
        h_sc[...] = hb
        h_lo = (h - hb.astype(F32)).astype(BF16)
        hi_part = jnp.dot(hb, rw_ref[...], preferred_element_type=F32)
        lo_part = jnp.dot(h_lo, rw_ref[:, 0:LANES], preferred_element_type=F32)
        logits = hi_part[:, 0:LANES] + hi_part[:, LANES:] + lo_part + rb_ref[...]
        rows, best = _routing(logits.T[0:N_EXPERTS, :])
        combt_sc[...] = jnp.zeros_like(combt_sc)
        for j in range(N_EXPERTS):
            combt_sc[j:j + 1, :] = rows[j]
        for g in range(N_GROUPS):
            combt_sc[row_oh + g:row_oh + g + 1, :] = jnp.where(best == g, 1.0, 0.0)
        oh8 = combt_sc[row_oh:row_oh + 8, :]
        r = lax.broadcasted_iota(jnp.int32, (sub, sub), 0)
        c = lax.broadcasted_iota(jnp.int32, (sub, sub), 1)
        tri_u = jnp.where(r <= c, 1.0, 0.0).astype(BF16)
        incl = [jnp.dot(oh8[:, s * sub:(s + 1) * sub].astype(BF16), tri_u, preferred_element_type=F32)
                for s in range(ns)]
        cmax = incl[0][:, sub - 1:sub]
        for s in range(1, ns):
            cmax = jnp.maximum(cmax, incl[s][:, sub - 1:sub])
        for g in range(N_GROUPS):
            cnt_sm[g] = jnp.max(cmax[g:g + 1, :]).astype(jnp.int32)
        over = jnp.where(cmax > cap, 1.0, 0.0)
        best_f = best.astype(F32)
        for s in range(ns):
            cols = slice(s * sub, (s + 1) * sub)
            oh_s = oh8[:, cols]
            rank = jnp.sum(oh_s * incl[s], axis=0, keepdims=True) - 1.0
            dense = jnp.sum(oh_s * over, axis=0, keepdims=True)
            combt_sc[row_slot:row_slot + 1, cols] = jnp.where(dense > 0.5, -1.0, best_f[:, cols] * cap + rank)
        comb_sc[...] = combt_sc[...].T
        for s in range(ns):
            toks = slice(s * sub, (s + 1) * sub)
            slot_row = combt_sc[row_slot:row_slot + 1, toks].astype(jnp.int32)
            ri = lax.broadcasted_iota(jnp.int32, (gc, sub), 0)
            p = jnp.where(ri == slot_row, 1.0, 0.0).astype(BF16)
            hs_all = jnp.dot(p, h_sc[toks, :], preferred_element_type=F32).astype(BF16)
            cs = comb_sc[toks, :]
            c_hi = cs.astype(BF16)
            c_lo = (cs - c_hi.astype(F32)).astype(BF16)
            ws_all = jnp.dot(p, jnp.concatenate([c_hi, c_lo], axis=1), preferred_element_type=F32)
            ws_all = ws_all[:, 0:LANES] + ws_all[:, LANES:]
            for g in range(N_GROUPS):
                hs_sc[g, s * cap:(s + 1) * cap, :] = hs_all[g * cap:(g + 1) * cap, :]
                ws_sc[g, s * cap:(s + 1) * cap, :] = ws_all[g * cap:(g + 1) * cap, :]
        acc_sc[...] = jnp.zeros_like(acc_sc)

    def experts(h_rows, w_rows):
        lane = lax.broadcasted_iota(jnp.int32, w_rows.shape, 1)
        hid = []
        for j in range(epg):
            gate = jnp.dot(h_rows, wg_ref[j], preferred_element_type=F32)
            up = jnp.dot(h_rows, wu_ref[j], preferred_element_type=F32)
            w_e = jnp.sum(jnp.where(lane == grp * epg + j, w_rows, 0.0), axis=-1, keepdims=True)
            hid.append((_silu(gate) * up * w_e).astype(BF16))
        wd = wd_ref[...].reshape(epg * D_FF, d)
        return jnp.dot(jnp.concatenate(hid, axis=1), wd, preferred_element_type=F32)

    cnt = cnt_sm[grp]

    @pl.when(cnt <= cap)
    def _():
        ys = experts(hs_sc[grp], ws_sc[grp]).astype(BF16)
        for s in range(ns):
            ys_sc[s, grp] = ys[s * cap:(s + 1) * cap, :]

    @pl.when(cnt > cap)
    def _():
        acc_sc[...] += experts(h_sc[...], comb_sc[...])
        for s in range(ns):
            ys_sc[s, grp] = jnp.zeros((cap, d), BF16)

    @pl.when(grp == N_GROUPS - 1)
    def _():
        for s in range(ns):
            toks = slice(s * sub, (s + 1) * sub)
            slot_col = comb_sc[toks, row_slot:row_slot + 1].astype(jnp.int32)
            li = lax.broadcasted_iota(jnp.int32, (sub, gc), 1)
            pt = jnp.where(li == slot_col, 1.0, 0.0).astype(BF16)
            y = jnp.dot(pt, ys_sc[s].reshape(gc, d), preferred_element_type=F32) + acc_sc[toks, :]
            y = o_ref[toks, :] + mod_ref[5:6, :] * y
            if final_norm:
                y = _rms(y) * fnw_ref[...]
            o_ref[toks, :] = y


def _outproj_moe(o_fox, o_s5, o_hgrn, x, mod_l, nf, ns_w, w_out, norm_w, rw_pad, rb_pad, wg, wu, wd, fnw,
                 tm, final_norm):
    bsz, seq, d = x.shape
    tm = min(tm, seq)
    assert tm % MOE_SUB == 0
    ns = tm // MOE_SUB
    epg = N_EXPERTS // N_GROUPS
    kern = functools.partial(_moe_kernel, final_norm=final_norm)
    row = lambda w: pl.BlockSpec((None, tm, w), lambda b, i, e: (b, i, 0))
    return pl.pallas_call(
        kern,
        grid=(bsz, seq // tm, N_GROUPS),
        in_specs=[
            row(FOX_WIDTH), row(S5_WIDTH), row(HGRN_WIDTH), row(d),
            pl.BlockSpec((None, 6, d), lambda b, i, e: (b, 0, 0)),
            pl.BlockSpec((1, FOX_WIDTH), lambda b, i, e: (0, 0)),
            pl.BlockSpec((1, S5_WIDTH), lambda b, i, e: (0, 0)),
            pl.BlockSpec((d, d), lambda b, i, e: (0, 0)),
            pl.BlockSpec((1, d), lambda b, i, e: (0, 0)),
            pl.BlockSpec((d, 2 * LANES), lambda b, i, e: (0, 0)),
            pl.BlockSpec((1, LANES), lambda b, i, e: (0, 0)),
            pl.BlockSpec((epg, d, D_FF), lambda b, i, e: (e, 0, 0)),
            pl.BlockSpec((epg, d, D_FF), lambda b, i, e: (e, 0, 0)),
            pl.BlockSpec((epg, D_FF, d), lambda b, i, e: (e, 0, 0)),
            pl.BlockSpec((1, d), lambda b, i, e: (0, 0)),
        ],
        out_specs=pl.BlockSpec((None, tm, d), lambda b, i, e: (b, i, 0)),
        out_shape=jax.ShapeDtypeStruct((bsz, seq, d), F32),
        scratch_shapes=[
            pltpu.VMEM((tm, d), BF16),
            pltpu.VMEM((tm, LANES), F32),
            pltpu.VMEM((LANES, tm), F32),
            pltpu.VMEM((N_GROUPS, ns * MOE_CAP, d), BF16),
            pltpu.VMEM((N_GROUPS, ns * MOE_CAP, LANES), F32),
            pltpu.VMEM((ns, N_GROUPS, MOE_CAP, d), BF16),
            pltpu.VMEM((tm, d), F32),
            pltpu.SMEM((N_GROUPS,), jnp.int32),
        ],
        compiler_params=_cparams(("arbitrary", "arbitrary", "arbitrary")),
        name="moe",
    )(o_fox, o_s5, o_hgrn, x, mod_l, nf, ns_w, w_out, norm_w, rw_pad, rb_pad, wg, wu, wd, fnw)


def _prep_w_in(w_in_l, f_bias_l):
    a = FOX_WIDTH
    o_ff = 3 * a
    o_su = o_ff + FOX_HEADS
    o_hq = o_su + S5_WIDTH
    hw = HGRN_WIDTH
    fq = w_in_l[:, 0:a] * (HEAD_DIM ** -0.5 * LOG2E)
    fk, fv = w_in_l[:, a:2 * a], w_in_l[:, 2 * a:3 * a]
    ff = jnp.pad(w_in_l[:, o_ff:o_su], ((0, 0), (0, LANES - FOX_HEADS)))
    su = w_in_l[:, o_su:o_hq]
    hq = w_in_l[:, o_hq:o_hq + hw]
    hf = w_in_l[:, o_hq + hw:o_hq + 2 * hw]
    hi = w_in_l[:, o_hq + 2 * hw:o_hq + 3 * hw]
    hg = w_in_l[:, o_hq + 3 * hw:o_hq + 4 * hw]
    w = jnp.concatenate([fq, fk, fv, su, hq, hi, hg, hf, ff], axis=1).astype(BF16)
    fb = jnp.pad(f_bias_l.astype(F32), (0, LANES - FOX_HEADS)).reshape(1, LANES)
    return w, fb


def _prep_s5(a_re, a_im, b_re, b_im, c_re, c_im, log_dt, bsz):
    lam = lax.complex(a_re.astype(F32), a_im.astype(F32))
    dt = jnp.exp(log_dt.astype(F32))[:, None]
    a_bar = jnp.exp(lam * dt)
    b_bar = ((a_bar - 1.0) / lam)[..., None] * lax.complex(b_re.astype(F32), b_im.astype(F32))
    eye = jnp.eye(S5_GROUPS, dtype=F32)
    bm_re = jnp.einsum('gpc,gh->gchp', jnp.real(b_bar), eye).reshape(S5_WIDTH, S5_NSTATE)
    bm_im = jnp.einsum('gpc,gh->gchp', jnp.imag(b_bar), eye).reshape(S5_WIDTH, S5_NSTATE)
    bmat = jnp.concatenate([bm_re, bm_im], axis=1).astype(BF16)
    cm_re = jnp.einsum('gcp,gh->gphc', c_re.astype(F32), eye).reshape(S5_NSTATE, S5_WIDTH)
    cm_im = jnp.einsum('gcp,gh->gphc', c_im.astype(F32), eye).reshape(S5_NSTATE, S5_WIDTH)
    cmat = jnp.concatenate([cm_re, -cm_im], axis=0).astype(BF16)
    ar = jnp.broadcast_to(jnp.real(a_bar).reshape(1, S5_NSTATE), (bsz, S5_NSTATE))
    ai = jnp.broadcast_to(jnp.imag(a_bar).reshape(1, S5_NSTATE), (bsz, S5_NSTATE))
    return bmat, ar, ai, cmat


def kernel(x, c, ada_w, ada_b, norm_mix_w, norm_ffn_w, w_in, fox_f_bias, s5_a_re, s5_a_im, s5_b_re, s5_b_im, s5_c_re, s5_c_im, s5_d, s5_log_dt, s5_glu_w, s5_glu_b, hgrn_lb_logits, hgrn_norm_w, branch_norm_fox, branch_norm_s5, w_out, router_w, router_b, moe_w_gate, moe_w_up, moe_w_down, final_norm_w):
    bsz, seq, d = x.shape
    depth = w_in.shape[0]
    lb_cum = jnp.cumsum(jax.nn.softmax(hgrn_lb_logits.astype(F32), axis=0), axis=0)
    lower = lb_cum - lb_cum[0:1]
    om_lb = 1.0 - lower

    mod = _ada_mod(c, ada_w, ada_b)
    rw32 = jnp.pad(router_w.astype(F32), ((0, 0), (0, LANES - N_EXPERTS)))
    rw_hi = rw32.astype(BF16)
    rw_pad = jnp.concatenate([rw_hi, (rw32 - rw_hi.astype(F32)).astype(BF16)], axis=1)
    rb_pad = jnp.pad(router_b.astype(F32), (0, LANES - N_EXPERTS)).reshape(1, LANES)
    fnw = final_norm_w.reshape(1, d).astype(F32)

    for l in range(depth):
        w_pad, fb_pad = _prep_w_in(w_in[l], fox_f_bias[l])
        qkv, su, hqig, hf, cum_t = _in_proj(x, mod[l], norm_mix_w[l].reshape(1, d), w_pad, fb_pad, tm=1024)
        o_fox = _fox_attention(qkv, cum_t, tq=256, tk=512)
        bmat, ar, ai, cmat = _prep_s5(s5_a_re[l], s5_a_im[l], s5_b_re[l], s5_b_im[l],
                                      s5_c_re[l], s5_c_im[l], s5_log_dt[l], bsz)
        o_s5 = _s5_mixer(su, bmat, ar, ai, cmat, s5_d[l].reshape(1, S5_WIDTH).astype(F32),
                         s5_glu_w[l].astype(BF16), s5_glu_b[l].reshape(1, S5_WIDTH).astype(F32), ln=128)
        nw2 = jnp.tile(hgrn_norm_w[l].astype(F32), 2).reshape(1, LANES)
        o_hgrn = _hgrn_mixer(hqig, hf, lower[l].reshape(1, -1), om_lb[l].reshape(1, -1), nw2, tc=128)
        x = _outproj_moe(o_fox, o_s5, o_hgrn, x, mod[l],
                         branch_norm_fox[l].reshape(1, -1).astype(F32),
                         branch_norm_s5[l].reshape(1, -1).astype(F32),
                         w_out[l].astype(BF16),
                         norm_ffn_w[l].reshape(1, d).astype(F32), rw_pad, rb_pad,
                         moe_w_gate[l].astype(BF16), moe_w_up[l].astype(BF16), moe_w_down[l].astype(BF16),
                         fnw, tm=1024, final_norm=(l == depth - 1))
    return x
```

```python
import functools
import math

import jax
import jax.numpy as jnp
from jax import lax
from jax.experimental import pallas as pl
from jax.experimental.pallas import tpu as pltpu

F32 = jnp.float32
BF16 = jnp.bfloat16
HIGHEST = lax.Precision.HIGHEST

D_MODEL = 1024
FOX_HEADS = 6
HEAD_DIM = 64
FOX_WIDTH = 384
S5_WIDTH = 256
S5_GROUPS = 16
S5_CH = 16
S5_STATE = 64
S5_NSTATE = S5_GROUPS * S5_STATE
HGRN_WIDTH = 384
N_EXPERTS = 16
N_GROUPS = 4
D_FF = 256
EPS = 1e-6
LANES = 128
NEG_BIG = -1e30
LOG2E = math.log2(math.e)
SOFTMAX_ROWS = 64
S5_PIECE = 64
MOE_SUB = 256
MOE_CAP = 96

C_QKV = (0, 1152)
C_SU = (1152, 1408)
C_HQIG = (1408, 2560)
C_HF = (2560, 2944)
C_FF = (2944, 3072)
N_IN_PAD = 3072

NT_DIMS = (((1,), (1,)), ((), ()))

VMEM_LIMIT = 56 * 1024 * 1024


def _cparams(sem):
    return pltpu.CompilerParams(dimension_semantics=sem, vmem_limit_bytes=VMEM_LIMIT)


def _sigmoid(x):
    return 1.0 / (1.0 + jnp.exp(-x))


def _silu(x):
    hx = 0.5 * x
    return hx + hx * jnp.tanh(hx)


def _log_sigmoid(x):
    return jnp.minimum(x, 0.0) - jnp.log1p(jnp.exp(-jnp.abs(x)))


def _rms(x):
    return x * lax.rsqrt(jnp.mean(x * x, axis=-1, keepdims=True) + EPS)


def _ada_kernel(c_ref, w_ref, b_ref, o_ref):
    c = c_ref[...]
    o_ref[...] = jnp.dot(_silu(c), w_ref[...], precision=HIGHEST,
                         preferred_element_type=F32) + b_ref[...]


def _ada_mod(c, ada_w, ada_b):
    depth, d, n6 = ada_w.shape
    bsz = c.shape[0]
    nblk = n6 // d
    out = pl.pallas_call(
        _ada_kernel,
        grid=(depth, nblk),
        in_specs=[
            pl.BlockSpec((bsz, d), lambda l, j: (0, 0)),
            pl.BlockSpec((None, d, d), lambda l, j: (l, 0, j)),
            pl.BlockSpec((None, 1, d), lambda l, j: (l, 0, j)),
        ],
        out_specs=pl.BlockSpec((None, bsz, d), lambda l, j: (l, 0, j)),
        out_shape=jax.ShapeDtypeStruct((depth, bsz, n6), F32),
        compiler_params=_cparams(("arbitrary", "arbitrary")),
        name="ada_mod",
    )(c, ada_w, ada_b.reshape(depth, 1, n6))
    return out.reshape(depth, bsz, nblk, d)


def _inproj_kernel(x_ref, mod_ref, nw_ref, w_ref, fb_ref,
                   qkv_ref, su_ref, hqig_ref, hf_ref, cum_ref, carry_ref):
    i = pl.program_id(1)
    tm = x_ref.shape[0]

    @pl.when(i == 0)
    def _():
        carry_ref[...] = jnp.zeros_like(carry_ref)

    h = _rms(x_ref[...]) * nw_ref[...]
    h = h * (1.0 + mod_ref[1:2, :]) + mod_ref[0:1, :]
    hb = h.astype(BF16)

    def proj(c):
        return jnp.dot(hb, w_ref[:, c[0]:c[1]], preferred_element_type=F32)

    qkv_ref[...] = proj(C_QKV).astype(BF16)
    su = proj(C_SU)
    su_ref[0] = su[:, :LANES]
    su_ref[1] = su[:, LANES:]
    hqig_ref[...] = proj(C_HQIG).astype(BF16)
    hf_ref[...] = proj(C_HF)

    lf = _log_sigmoid(proj(C_FF) + fb_ref[...])
    lf_t = lf.T[0:8, :]
    r = lax.broadcasted_iota(jnp.int32, (tm, tm), 0)
    c = lax.broadcasted_iota(jnp.int32, (tm, tm), 1)
    tri_u = jnp.where(r <= c, 1.0, 0.0).astype(BF16)
    hi = lf_t.astype(BF16).astype(F32)
    r1 = lf_t - hi
    mid = r1.astype(BF16).astype(F32)
    pieces = jnp.concatenate([hi, mid, r1 - mid], axis=0).astype(BF16)
    parts = jnp.dot(pieces, tri_u, preferred_element_type=F32)
    cum = parts[0:8] + parts[8:16] + parts[16:24] + carry_ref[:, 0:1]
    cum_ref[...] = cum * LOG2E
    carry_ref[...] = jnp.broadcast_to(cum[:, tm - 1:tm], carry_ref.shape)


def _in_proj(x, mod_l, norm_w, w_pad, fb_pad, tm):
    bsz, seq, d = x.shape
    tm = min(tm, seq)
    return pl.pallas_call(
        _inproj_kernel,
        grid=(bsz, seq // tm),
        in_specs=[
            pl.BlockSpec((None, tm, d), lambda b, i: (b, i, 0)),
            pl.BlockSpec((None, 6, d), lambda b, i: (b, 0, 0)),
            pl.BlockSpec((1, d), lambda b, i: (0, 0)),
            pl.BlockSpec((d, N_IN_PAD), lambda b, i: (0, 0)),
            pl.BlockSpec((1, LANES), lambda b, i: (0, 0)),
        ],
        out_specs=[
            pl.BlockSpec((None, tm, 1152), lambda b, i: (b, i, 0)),
            pl.BlockSpec((2, None, tm, LANES), lambda b, i: (0, b, i, 0)),
            pl.BlockSpec((None, tm, 1152), lambda b, i: (b, i, 0)),
            pl.BlockSpec((None, tm, HGRN_WIDTH), lambda b, i: (b, i, 0)),
            pl.BlockSpec((None, 8, tm), lambda b, i: (b, 0, i)),
        ],
        out_shape=[
            jax.ShapeDtypeStruct((bsz, seq, 1152), BF16),
            jax.ShapeDtypeStruct((2, bsz, seq, LANES), F32),
            jax.ShapeDtypeStruct((bsz, seq, 1152), BF16),
            jax.ShapeDtypeStruct((bsz, seq, HGRN_WIDTH), F32),
            jax.ShapeDtypeStruct((bsz, 8, seq), F32),
        ],
        scratch_shapes=[pltpu.VMEM((8, LANES), F32)],
        compiler_params=_cparams(("arbitrary", "arbitrary")),
        name="in_proj",
    )(x, mod_l, norm_w, w_pad, fb_pad)


def _fox_kernel(q_ref, k_ref, v_ref, cum_ref, o_ref, s_sc, p_sc, al_sc, m_sc, l_sc, acc_sc, d_sc,
                *, tq, tk, n_steps, unroll):
    p = pl.program_id(1)
    seq = q_ref.shape[0]
    nq = seq // tq
    lane = lax.broadcasted_iota(jnp.int32, (tq, LANES), 1)
    first = lane < HEAD_DIM

    def advance(i, j):
        last = j == (i * tq) // tk
        return jnp.minimum(i + last.astype(jnp.int32), nq - 1), jnp.where(last, 0, j + 1)

    def stage_qk(i, j):
        q = q_ref[pl.ds(pl.multiple_of(i * tq, tq), tq), :]
        zero = jnp.zeros_like(q)
        q2 = jnp.concatenate([jnp.where(first, q, zero), jnp.where(first, zero, q)], axis=0)
        k0 = pl.multiple_of(j * tk, tk)
        s = lax.dot_general(q2, k_ref[pl.ds(k0, tk), :], NT_DIMS, preferred_element_type=F32)
        bias_a = cum_ref[pl.ds(2 * p, 1), pl.ds(k0, tk)]
        bias_b = cum_ref[pl.ds(2 * p + 1, 1), pl.ds(k0, tk)]
        is_last = j == (i * tq) // tk
        mask = d_sc[jnp.where(is_last, 1 + i - j * (tk // tq), 0)]
        s_sc[...] = jnp.concatenate([s[:tq] + (mask - bias_a), s[tq:] + (mask - bias_b)], axis=0)

    def stage_softmax(j):
        cap = jnp.where(j == 0, NEG_BIG, -NEG_BIG)
        for r0 in range(0, 2 * tq, SOFTMAX_ROWS):
            rs = slice(r0, r0 + SOFTMAX_ROWS)
            m_prev = jnp.minimum(m_sc[rs, :], cap)
            s = s_sc[rs, :]
            m_next = jnp.maximum(m_prev, jnp.max(s, axis=1, keepdims=True))
            pr = jnp.exp2(s - jnp.tile(m_next, (1, tk // LANES)))
            alpha = jnp.exp2(m_prev - m_next)
            l_sc[rs, :] = alpha * l_sc[rs, :] + jnp.sum(pr, axis=1, keepdims=True)
            m_sc[rs, :] = m_next
            p_sc[rs, :] = pr.astype(BF16)
            al_sc[rs, :] = alpha

    def stage_pv(i, j):
        k0 = pl.multiple_of(j * tk, tk)
        acc = acc_sc[...] * al_sc[...] + jnp.dot(p_sc[...], v_ref[pl.ds(k0, tk), :],
                                                  preferred_element_type=F32)
        acc_sc[...] = acc
        o = acc / l_sc[...]
        o_ref[pl.ds(pl.multiple_of(i * tq, tq), tq), :] = jnp.where(first, o[:tq], o[tq:]).astype(BF16)

    col_minus_row = (lax.broadcasted_iota(jnp.int32, (tq, tk), 1)
                     - lax.broadcasted_iota(jnp.int32, (tq, tk), 0))
    d_sc[0] = jnp.zeros((tq, tk), F32)
    for k in range(tk // tq):
        d_sc[1 + k] = jnp.where(col_minus_row <= k * tq, 0.0, NEG_BIG)
    s_sc[...] = jnp.zeros(s_sc.shape, F32)
    p_sc[...] = jnp.zeros(p_sc.shape, BF16)
    al_sc[...] = jnp.ones(al_sc.shape, F32)
    m_sc[...] = jnp.full(m_sc.shape, NEG_BIG, F32)
    l_sc[...] = jnp.ones(l_sc.shape, F32)
    acc_sc[...] = jnp.zeros(acc_sc.shape, F32)

    def body(t, carry):
        ia, ja, ib, jb, ic, jc = carry
        stage_pv(ic, jc)
        stage_softmax(jb)
        stage_qk(ia, ja)
        na, nja = advance(ia, ja)
        return na, nja, ia, ja, ib, jb

    zero = jnp.int32(0)
    lax.fori_loop(0, n_steps + 2, body, (zero, zero, zero, zero, zero, zero), unroll=unroll)


def _fox_attention(qkv, cum_t, tq, tk):
    bsz, seq, _ = qkv.shape
    tq = min(tq, seq)
    tk = min(max(tk, tq), seq)
    assert tk % tq == 0 and seq % tk == 0
    npair = FOX_HEADS // 2
    n_steps = sum((i * tq) // tk + 1 for i in range(seq // tq))
    kern = functools.partial(_fox_kernel, tq=tq, tk=tk, n_steps=n_steps,
                             unroll=2 if n_steps % 2 == 0 else 1)
    col = lambda off: pl.BlockSpec((None, seq, LANES), lambda b, p: (b, 0, off + p))
    return pl.pallas_call(
        kern,
        grid=(bsz, npair),
        in_specs=[col(0), col(npair), col(2 * npair),
                  pl.BlockSpec((None, 8, seq), lambda b, p: (b, 0, 0))],
        out_specs=col(0),
        out_shape=jax.ShapeDtypeStruct((bsz, seq, FOX_WIDTH), BF16),
        scratch_shapes=[
            pltpu.VMEM((2 * tq, tk), F32),
            pltpu.VMEM((2 * tq, tk), BF16),
            pltpu.VMEM((2 * tq, LANES), F32),
            pltpu.VMEM((2 * tq, LANES), F32),
            pltpu.VMEM((2 * tq, LANES), F32),
            pltpu.VMEM((2 * tq, LANES), F32),
            pltpu.VMEM((1 + tk // tq, tq, tk), F32),
        ],
        compiler_params=_cparams(("arbitrary", "arbitrary")),
        name="fox_attn",
    )(qkv, qkv, qkv, cum_t)


def _gelu_tanh(x):
    return 0.5 * x * (1.0 + jnp.tanh(math.sqrt(2.0 / math.pi) * (x + 0.044715 * (x * x * x))))


def _s5_kernel(su_ref, bm_ref, ar_ref, ai_ref, cm_ref, d_ref, gw_ref, gb_ref, o_ref,
               utb, bu, ytb, st):
    i = pl.program_id(0)
    nb = su_ref.shape[1]
    ln = su_ref.shape[2]
    ns = S5_NSTATE

    @pl.when(i == 0)
    def _():
        st[...] = jnp.zeros_like(st)

    for k in range(2):
        for b in range(nb):
            utb[k, pl.ds(b, ln, stride=nb), :] = su_ref[k, b]
    u = jnp.concatenate([utb[0], utb[1]], axis=1)
    ub = u.astype(BF16)
    ar = ar_ref[...]
    ai = ai_ref[...]
    re, im = st[:, 0:ns], st[:, ns:2 * ns]
    npiece = max(ln // S5_PIECE, 1)
    prow = (ln // npiece) * nb
    for c in range(npiece):
        rows_c = slice(c * prow, (c + 1) * prow)
        bu[rows_c, :] = jnp.dot(ub[rows_c, :], bm_ref[...], preferred_element_type=F32)
    ys = []
    for c in range(npiece):
        for t in range(c * prow // nb, (c + 1) * prow // nb):
            r0 = t * nb
            nre = ar * re - ai * im + bu[r0:r0 + nb, 0:ns]
            nim = ar * im + ai * re + bu[r0:r0 + nb, ns:2 * ns]
            bu[r0:r0 + nb, 0:ns] = nre
            bu[r0:r0 + nb, ns:2 * ns] = nim
            re, im = nre, nim
        rows_c = slice(c * prow, (c + 1) * prow)
        ys.append(jnp.dot(bu[rows_c, :].astype(BF16), cm_ref[...], preferred_element_type=F32))
    st[:, 0:ns] = re
    st[:, ns:2 * ns] = im

    y = jnp.concatenate(ys, axis=0) + d_ref[...] * u
    y = _gelu_tanh(y)
    gate = _sigmoid(jnp.dot(y.astype(BF16), gw_ref[...], preferred_element_type=F32) + gb_ref[...])
    out = y * gate
    ytb[0] = out[:, :LANES]
    ytb[1] = out[:, LANES:]
    for k in range(2):
        for b in range(nb):
            o_ref[b, :, k * LANES:(k + 1) * LANES] = ytb[k, pl.ds(b, ln, stride=nb), :].astype(BF16)


def _s5_mixer(su, bmat, ar, ai, cmat, dvec, glu_w, glu_b, ln):
    _, bsz, seq, _ = su.shape
    ln = min(ln, seq)
    rows = bsz * ln
    const = lambda shape: pl.BlockSpec(shape, lambda i: (0,) * len(shape))
    return pl.pallas_call(
        _s5_kernel,
        grid=(seq // ln,),
        in_specs=[
            pl.BlockSpec((2, bsz, ln, LANES), lambda i: (0, 0, i, 0)),
            const((S5_WIDTH, 2 * S5_NSTATE)),
            const((bsz, S5_NSTATE)),
            const((bsz, S5_NSTATE)),
            const((2 * S5_NSTATE, S5_WIDTH)),
            const((1, S5_WIDTH)),
            const((S5_WIDTH, S5_WIDTH)),
            const((1, S5_WIDTH)),
        ],
        out_specs=pl.BlockSpec((bsz, ln, S5_WIDTH), lambda i: (0, i, 0)),
        out_shape=jax.ShapeDtypeStruct((bsz, seq, S5_WIDTH), BF16),
        scratch_shapes=[
            pltpu.VMEM((2, rows, LANES), F32),
            pltpu.VMEM((rows, 2 * S5_NSTATE), F32),
            pltpu.VMEM((2, rows, LANES), F32),
            pltpu.VMEM((bsz, 2 * S5_NSTATE), F32),
        ],
        compiler_params=_cparams(("arbitrary",)),
        name="s5_mixer",
    )(su, bmat, ar, ai, cmat, dvec, glu_w, glu_b)


def _block_row(a, blk, r):
    t = a.shape[0]
    if blk >= 8:
        a3 = a.reshape(t // blk, blk, LANES)
        return jnp.broadcast_to(a3[:, r:r + 1, :], a3.shape).reshape(t, LANES)
    if blk == 4:
        a3 = a.reshape(t // 8, 8, LANES)
        sub = lax.broadcasted_iota(jnp.int32, a3.shape, 1)
        res = jnp.where(sub < 4, jnp.broadcast_to(a3[:, r:r + 1, :], a3.shape),
                        jnp.broadcast_to(a3[:, 4 + r:5 + r, :], a3.shape))
        return res.reshape(t, LANES)
    row = lax.broadcasted_iota(jnp.int32, a.shape, 0)
    res = a
    for q in range(blk):
        if q != r:
            res = jnp.where(row % blk == q, pltpu.roll(a, (q - r) % t, axis=0), res)
    return res


def _hgrn_kernel(hq_ref, hi_ref, hg_ref, hf_ref, lb_ref, omlb_ref, nw_ref, o_ref, lvl_sc, mask_sc, lo_sc,
                 *, tc):
    seq = hf_ref.shape[0]
    nlev = tc.bit_length() - 1
    lane = lax.broadcasted_iota(jnp.int32, (tc, LANES), 1)
    first = lane < HEAD_DIM
    row = lax.broadcasted_iota(jnp.int32, (tc, LANES), 0)
    rr = lax.broadcasted_iota(jnp.int32, (tc, tc), 0)
    cc = lax.broadcasted_iota(jnp.int32, (tc, tc), 1)
    diff = rr ^ cc
    bits = jnp.zeros((tc, tc), jnp.int32)
    for i in range(nlev):
        bits = bits + jnp.where((diff >> i) != 0, 1, 0)
    lvl_sc[...] = jnp.where(rr >= cc, bits, -1)
    for i in range(nlev):
        upper = ((row >> i) & 1) == 1
        mask_sc[i, 0] = jnp.where(jnp.logical_and(upper, first), 1.0, 0.0).astype(BF16)
        mask_sc[i, 1] = jnp.where(jnp.logical_and(upper, jnp.logical_not(first)), 1.0, 0.0).astype(BF16)
        mask_sc[i, 2] = jnp.where(upper, 0.0, 1.0).astype(BF16)
        lo_sc[i] = jnp.where(upper, 0.0, 1.0)
    sr = lax.broadcasted_iota(jnp.int32, (LANES, LANES), 0)
    sc = lax.broadcasted_iota(jnp.int32, (LANES, LANES), 1)
    blockdiag = (sr < HEAD_DIM) == (sc < HEAD_DIM)
    lb = lb_ref[...]
    omlb = omlb_ref[...]
    nw = nw_ref[...]

    def chunk(c, st):
        r0 = pl.multiple_of(c * tc, tc)
        z = hf_ref[pl.ds(r0, tc), :]
        q = _silu(hq_ref[pl.ds(r0, tc), :].astype(F32))
        v = hi_ref[pl.ds(r0, tc), :]
        g = hg_ref[pl.ds(r0, tc), :].astype(F32)
        e = jnp.exp(-jnp.abs(z))
        s_big = 1.0 / (1.0 + e)
        s_small = e * s_big
        pos = z >= 0.0
        f = lb + omlb * jnp.where(pos, s_big, s_small)
        kk = omlb * jnp.where(pos, s_small, s_big)

        def scores(q_a, q_b, km):
            return lax.dot_general(jnp.concatenate([q_a, q_b], axis=0), km, NT_DIMS,
                                   preferred_element_type=F32)

        s = scores(jnp.where(first, q, 0.0).astype(BF16), jnp.where(first, 0.0, q).astype(BF16),
                   kk.astype(BF16))
        on_diag = lvl_sc[...] == 0
        tot_a = jnp.where(on_diag, s[:tc], 0.0)
        tot_b = jnp.where(on_diag, s[tc:], 0.0)
        a_m = f
        b_m = jnp.ones_like(f)
        for i in range(nlev):
            m = 1 << i
            qa = (q * a_m).astype(BF16)
            s = scores(qa * mask_sc[i, 0], qa * mask_sc[i, 1], (kk * b_m).astype(BF16) * mask_sc[i, 2])
            sel = lvl_sc[...] == i + 1
            tot_a = jnp.where(sel, s[:tc], tot_a)
            tot_b = jnp.where(sel, s[tc:], tot_b)
            lower_total = _block_row(a_m, 2 * m, m - 1)
            upper_total = _block_row(a_m, 2 * m, 2 * m - 1)
            upper = lo_sc[i] < 0.5
            a_m = jnp.where(upper, a_m * lower_total, a_m)
            b_m = jnp.where(upper, b_m, b_m * upper_total)

        o = lax.dot_general((q * a_m).astype(BF16), st.astype(BF16), NT_DIMS, preferred_element_type=F32)
        pv = jnp.dot(jnp.concatenate([tot_a, tot_b], axis=0).astype(BF16), v, preferred_element_type=F32)
        o = o + jnp.where(first, pv[:tc], pv[tc:])

        upd = jnp.dot(v.astype(F32).T.astype(BF16), (kk * b_m).astype(BF16), preferred_element_type=F32)
        st_new = st * a_m[tc - 1:tc, :] + jnp.where(blockdiag, upd, 0.0)

        o2 = o * o
        s_a = jnp.sum(jnp.where(first, o2, 0.0), axis=-1, keepdims=True)
        s_b = jnp.sum(jnp.where(first, 0.0, o2), axis=-1, keepdims=True)
        ms = jnp.where(first, s_a, s_b) * (1.0 / HEAD_DIM)
        o_ref[pl.ds(r0, tc), :] = (o * lax.rsqrt(ms + EPS) * nw * _silu(g)).astype(BF16)
        return st_new

    lax.fori_loop(0, seq // tc, chunk, jnp.zeros((LANES, LANES), F32), unroll=4)


def _hgrn_mixer(hqig, hf, lb, om_lb, nw2, tc):
    bsz, seq, _ = hf.shape
    tc = min(tc, seq)
    npair = HGRN_WIDTH // LANES
    kern = functools.partial(_hgrn_kernel, tc=tc)
    col = lambda off: pl.BlockSpec((None, seq, LANES), lambda b, p: (b, 0, off + p))
    par = pl.BlockSpec((1, LANES), lambda b, p: (0, p))
    return pl.pallas_call(
        kern,
        grid=(bsz, npair),
        in_specs=[col(0), col(npair), col(2 * npair), col(0), par, par,
                  pl.BlockSpec((1, LANES), lambda b, p: (0, 0))],
        out_specs=col(0),
        out_shape=jax.ShapeDtypeStruct((bsz, seq, HGRN_WIDTH), BF16),
        scratch_shapes=[
            pltpu.VMEM((tc, tc), jnp.int32),
            pltpu.VMEM((tc.bit_length() - 1, 3, tc, LANES), BF16),
            pltpu.VMEM((tc.bit_length() - 1, tc, LANES), F32),
        ],
        compiler_params=_cparams(("arbitrary", "arbitrary")),
        name="hgrn_mixer",
    )(hqig, hqig, hqig, hf, lb, om_lb, nw2)


def _routing(logits_t):
    mx = jnp.max(logits_t, axis=0, keepdims=True)
    ex = jnp.exp(logits_t - mx)
    probs = ex / jnp.sum(ex, axis=0, keepdims=True)
    p = [probs[e:e + 1, :] for e in range(N_EXPERTS)]
    epg = N_EXPERTS // N_GROUPS
    scores = []
    for g in range(N_GROUPS):
        a, b, c, d = p[epg * g:epg * g + epg]
        hi1, lo1 = jnp.maximum(a, b), jnp.minimum(a, b)
        hi2, lo2 = jnp.maximum(c, d), jnp.minimum(c, d)
        top1 = jnp.maximum(hi1, hi2)
        top2 = jnp.maximum(jnp.minimum(hi1, hi2), jnp.maximum(lo1, lo2))
        scores.append(top1 + top2)
    best = jnp.zeros_like(scores[0], dtype=jnp.int32)
    bs = scores[0]
    for g in range(1, N_GROUPS):
        upd = scores[g] > bs
        best = jnp.where(upd, g, best)
        bs = jnp.where(upd, scores[g], bs)
    ig = []
    for j in range(epg):
        val = p[j]
        for g in range(1, N_GROUPS):
            val = jnp.where(best == g, p[epg * g + j], val)
        ig.append(val)
    i1 = jnp.zeros_like(best)
    w1 = ig[0]
    for j in range(1, epg):
        upd = ig[j] > w1
        i1 = jnp.where(upd, j, i1)
        w1 = jnp.where(upd, ig[j], w1)
    i2 = jnp.zeros_like(best)
    w2 = jnp.full_like(w1, -1.0)
    for j in range(epg):
        upd = jnp.logical_and(i1 != j, ig[j] > w2)
        i2 = jnp.where(upd, j, i2)
        w2 = jnp.where(upd, ig[j], w2)
    den = w1 + w2
    tw1 = w1 / den
    tw2 = w2 / den
    e1 = best * epg + i1
    e2 = best * epg + i2
    return [jnp.where(e1 == e, tw1, 0.0) + jnp.where(e2 == e, tw2, 0.0) for e in range(N_EXPERTS)], best


def _moe_kernel(fox_ref, s5_ref, hg_ref, x_ref, mod_ref, nf_ref, ns_ref, wo_ref,
                nw_ref, rw_ref, rb_ref, wg_ref, wu_ref, wd_ref, fnw_ref,
                o_ref, h_sc, comb_sc, combt_sc, hs_sc, ws_sc, ys_sc, acc_sc, cnt_sm, *, final_norm):
    grp = pl.program_id(2)
    tm, d = x_ref.shape
    epg = N_EXPERTS // N_GROUPS
    sub, cap = MOE_SUB, MOE_CAP
    ns = tm // sub
    gc = N_GROUPS * cap
    row_oh, row_slot = N_EXPERTS, N_EXPERTS + 8

    @pl.when(grp == 0)
    def _():
        a, b = FOX_WIDTH, FOX_WIDTH + S5_WIDTH
        for s in range(ns):
            toks = slice(s * sub, (s + 1) * sub)
            of = (_rms(fox_ref[toks, :].astype(F32)) * nf_ref[...]).astype(BF16)
            os5 = (_rms(s5_ref[toks, :].astype(F32)) * ns_ref[...]).astype(BF16)
            mix = jnp.dot(of, wo_ref[0:a, :], preferred_element_type=F32)
            mix = mix + jnp.dot(os5, wo_ref[a:b, :], preferred_element_type=F32)
            mix = mix + jnp.dot(hg_ref[toks, :], wo_ref[b:, :], preferred_element_type=F32)
            o_ref[toks, :] = x_ref[toks, :] + mod_ref[2:3, :] * mix
        h = _rms(o_ref[...]) * nw_ref[...]
        h = h * (1.0 + mod_ref[4:5, :]) + mod_ref[3:4, :]
        hb = h.astype(BF16)
        h_sc[...] = hb
        h_lo = (h - hb.astype(F32)).astype(BF16)
        hi_part = jnp.dot(hb, rw_ref[...], preferred_element_type=F32)
        lo_part = jnp.dot(h_lo, rw_ref[:, 0:LANES], preferred_element_type=F32)
        logits = hi_part[:, 0:LANES] + hi_part[:, LANES:] + lo_part + rb_ref[...]
        rows, best = _routing(logits.T[0:N_EXPERTS, :])
        combt_sc[...] = jnp.zeros_like(combt_sc)
        for j in range(N_EXPERTS):
            combt_sc[j:j + 1, :] = rows[j]
        for g in range(N_GROUPS):
            combt_sc[row_oh + g:row_oh + g + 1, :] = jnp.where(best == g, 1.0, 0.0)
        oh8 = combt_sc[row_oh:row_oh + 8, :]
        r = lax.broadcasted_iota(jnp.int32, (sub, sub), 0)
        c = lax.broadcasted_iota(jnp.int32, (sub, sub), 1)
        tri_u = jnp.where(r <= c, 1.0, 0.0).astype(BF16)
        incl = [jnp.dot(oh8[:, s * sub:(s + 1) * sub].astype(BF16), tri_u, preferred_element_type=F32)
                for s in range(ns)]
        cmax = incl[0][:, sub - 1:sub]
        for s in range(1, ns):
            cmax = jnp.maximum(cmax, incl[s][:, sub - 1:sub])
        for g in range(N_GROUPS):
            cnt_sm[g] = jnp.max(cmax[g:g + 1, :]).astype(jnp.int32)
        over = jnp.where(cmax > cap, 1.0, 0.0)
        best_f = best.astype(F32)
        for s in range(ns):
            cols = slice(s * sub, (s + 1) * sub)
            oh_s = oh8[:, cols]
            rank = jnp.sum(oh_s * incl[s], axis=0, keepdims=True) - 1.0
            dense = jnp.sum(oh_s * over, axis=0, keepdims=True)
            combt_sc[row_slot:row_slot + 1, cols] = jnp.where(dense > 0.5, -1.0, best_f[:, cols] * cap + rank)
        comb_sc[...] = combt_sc[...].T
        for s in range(ns):
            toks = slice(s * sub, (s + 1) * sub)
            slot_row = combt_sc[row_slot:row_slot + 1, toks].astype(jnp.int32)
            ri = lax.broadcasted_iota(jnp.int32, (gc, sub), 0)
            p = jnp.where(ri == slot_row, 1.0, 0.0).astype(BF16)
            hs_all = jnp.dot(p, h_sc[toks, :], preferred_element_type=F32).astype(BF16)
            cs = comb_sc[toks, :]
            c_hi = cs.astype(BF16)
            c_lo = (cs - c_hi.astype(F32)).astype(BF16)
            ws_all = jnp.dot(p, jnp.concatenate([c_hi, c_lo], axis=1), preferred_element_type=F32)
            ws_all = ws_all[:, 0:LANES] + ws_all[:, LANES:]
            for g in range(N_GROUPS):
                hs_sc[g, s * cap:(s + 1) * cap, :] = hs_all[g * cap:(g + 1) * cap, :]
                ws_sc[g, s * cap:(s + 1) * cap, :] = ws_all[g * cap:(g + 1) * cap, :]
        acc_sc[...] = jnp.zeros_like(acc_sc)

    def experts(h_rows, w_rows):
        lane = lax.broadcasted_iota(jnp.int32, w_rows.shape, 1)
        hid = []
        for j in range(epg):
            gate = jnp.dot(h_rows, wg_ref[j], preferred_element_type=F32)
            up = jnp.dot(h_rows, wu_ref[j], preferred_element_type=F32)
            w_e = jnp.sum(jnp.where(lane == grp * epg + j, w_rows, 0.0), axis=-1, keepdims=True)
            hid.append((_silu(gate) * up * w_e).astype(BF16))
        wd = wd_ref[...].reshape(epg * D_FF, d)
        return jnp.dot(jnp.concatenate(hid, axis=1), wd, preferred_element_type=F32)

    cnt = cnt_sm[grp]

    @pl.when(cnt <= cap)
    def _():
        ys = experts(hs_sc[grp], ws_sc[grp]).astype(BF16)
        for s in range(ns):
            ys_sc[s, grp] = ys[s * cap:(s + 1) * cap, :]

    @pl.when(cnt > cap)
    def _():
        acc_sc[...] += experts(h_sc[...], comb_sc[...])
        for s in range(ns):
            ys_sc[s, grp] = jnp.zeros((cap, d), BF16)

    @pl.when(grp == N_GROUPS - 1)
    def _():
        for s in range(ns):
            toks = slice(s * sub, (s + 1) * sub)
            slot_col = comb_sc[toks, row_slot:row_slot + 1].astype(jnp.int32)
            li = lax.broadcasted_iota(jnp.int32, (sub, gc), 1)
            pt = jnp.where(li == slot_col, 1.0, 0.0).astype(BF16)
            y = jnp.dot(pt, ys_sc[s].reshape(gc, d), preferred_element_type=F32) + acc_sc[toks, :]
            y = o_ref[toks, :] + mod_ref[5:6, :] * y
            if final_norm:
                y = _rms(y) * fnw_ref[...]
            o_ref[toks, :] = y


def _outproj_moe(o_fox, o_s5, o_hgrn, x, mod_l, nf, ns_w, w_out, norm_w, rw_pad, rb_pad, wg, wu, wd, fnw,
                 tm, final_norm):
    bsz, seq, d = x.shape
    tm = min(tm, seq)
    assert tm % MOE_SUB == 0
    ns = tm // MOE_SUB
    epg = N_EXPERTS // N_GROUPS
    kern = functools.partial(_moe_kernel, final_norm=final_norm)
    row = lambda w: pl.BlockSpec((None, tm, w), lambda b, i, e: (b, i, 0))
    return pl.pallas_call(
        kern,
        grid=(bsz, seq // tm, N_GROUPS),
        in_specs=[
            row(FOX_WIDTH), row(S5_WIDTH), row(HGRN_WIDTH), row(d),
            pl.BlockSpec((None, 6, d), lambda b, i, e: (b, 0, 0)),
            pl.BlockSpec((1, FOX_WIDTH), lambda b, i, e: (0, 0)),
            pl.BlockSpec((1, S5_WIDTH), lambda b, i, e: (0, 0)),
            pl.BlockSpec((d, d), lambda b, i, e: (0, 0)),
            pl.BlockSpec((1, d), lambda b, i, e: (0, 0)),
            pl.BlockSpec((d, 2 * LANES), lambda b, i, e: (0, 0)),
            pl.BlockSpec((1, LANES), lambda b, i, e: (0, 0)),
            pl.BlockSpec((epg, d, D_FF), lambda b, i, e: (e, 0, 0)),
            pl.BlockSpec((epg, d, D_FF), lambda b, i, e: (e, 0, 0)),
            pl.BlockSpec((epg, D_FF, d), lambda b, i, e: (e, 0, 0)),
            pl.BlockSpec((1, d), lambda b, i, e: (0, 0)),
        ],
        out_specs=pl.BlockSpec((None, tm, d), lambda b, i, e: (b, i, 0)),
        out_shape=jax.ShapeDtypeStruct((bsz, seq, d), F32),
        scratch_shapes=[
            pltpu.VMEM((tm, d), BF16),
            pltpu.VMEM((tm, LANES), F32),
            pltpu.VMEM((LANES, tm), F32),
            pltpu.VMEM((N_GROUPS, ns * MOE_CAP, d), BF16),
            pltpu.VMEM((N_GROUPS, ns * MOE_CAP, LANES), F32),
            pltpu.VMEM((ns, N_GROUPS, MOE_CAP, d), BF16),
            pltpu.VMEM((tm, d), F32),
            pltpu.SMEM((N_GROUPS,), jnp.int32),
        ],
        compiler_params=_cparams(("arbitrary", "arbitrary", "arbitrary")),
        name="moe",
    )(o_fox, o_s5, o_hgrn, x, mod_l, nf, ns_w, w_out, norm_w, rw_pad, rb_pad, wg, wu, wd, fnw)


def _prep_w_in(w_in_l, f_bias_l):
    a = FOX_WIDTH
    o_ff = 3 * a
    o_su = o_ff + FOX_HEADS
    o_hq = o_su + S5_WIDTH
    hw = HGRN_WIDTH
    fq = w_in_l[:, 0:a] * (HEAD_DIM ** -0.5 * LOG2E)
    fk, fv = w_in_l[:, a:2 * a], w_in_l[:, 2 * a:3 * a]
    ff = jnp.pad(w_in_l[:, o_ff:o_su], ((0, 0), (0, LANES - FOX_HEADS)))
    su = w_in_l[:, o_su:o_hq]
    hq = w_in_l[:, o_hq:o_hq + hw]
    hf = w_in_l[:, o_hq + hw:o_hq + 2 * hw]
    hi = w_in_l[:, o_hq + 2 * hw:o_hq + 3 * hw]
    hg = w_in_l[:, o_hq + 3 * hw:o_hq + 4 * hw]
    w = jnp.concatenate([fq, fk, fv, su, hq, hi, hg, hf, ff], axis=1).astype(BF16)
    fb = jnp.pad(f_bias_l.astype(F32), (0, LANES - FOX_HEADS)).reshape(1, LANES)
    return w, fb


def _prep_s5(a_re, a_im, b_re, b_im, c_re, c_im, log_dt, bsz):
    lam = lax.complex(a_re.astype(F32), a_im.astype(F32))
    dt = jnp.exp(log_dt.astype(F32))[:, None]
    a_bar = jnp.exp(lam * dt)
    b_bar = ((a_bar - 1.0) / lam)[..., None] * lax.complex(b_re.astype(F32), b_im.astype(F32))
    eye = jnp.eye(S5_GROUPS, dtype=F32)
    bm_re = jnp.einsum('gpc,gh->gchp', jnp.real(b_bar), eye).reshape(S5_WIDTH, S5_NSTATE)
    bm_im = jnp.einsum('gpc,gh->gchp', jnp.imag(b_bar), eye).reshape(S5_WIDTH, S5_NSTATE)
    bmat = jnp.concatenate([bm_re, bm_im], axis=1).astype(BF16)
    cm_re = jnp.einsum('gcp,gh->gphc', c_re.astype(F32), eye).reshape(S5_NSTATE, S5_WIDTH)
    cm_im = jnp.einsum('gcp,gh->gphc', c_im.astype(F32), eye).reshape(S5_NSTATE, S5_WIDTH)
    cmat = jnp.concatenate([cm_re, -cm_im], axis=0).astype(BF16)
    ar = jnp.broadcast_to(jnp.real(a_bar).reshape(1, S5_NSTATE), (bsz, S5_NSTATE))
    ai = jnp.broadcast_to(jnp.imag(a_bar).reshape(1, S5_NSTATE), (bsz, S5_NSTATE))
    return bmat, ar, ai, cmat


def kernel(x, c, ada_w, ada_b, norm_mix_w, norm_ffn_w, w_in, fox_f_bias, s5_a_re, s5_a_im, s5_b_re, s5_b_im, s5_c_re, s5_c_im, s5_d, s5_log_dt, s5_glu_w, s5_glu_b, hgrn_lb_logits, hgrn_norm_w, branch_norm_fox, branch_norm_s5, w_out, router_w, router_b, moe_w_gate, moe_w_up, moe_w_down, final_norm_w):
    bsz, seq, d = x.shape
    depth = w_in.shape[0]
    lb_cum = jnp.cumsum(jax.nn.softmax(hgrn_lb_logits.astype(F32), axis=0), axis=0)
    lower = lb_cum - lb_cum[0:1]
    om_lb = 1.0 - lower

    mod = _ada_mod(c, ada_w, ada_b)
    rw32 = jnp.pad(router_w.astype(F32), ((0, 0), (0, LANES - N_EXPERTS)))
    rw_hi = rw32.astype(BF16)
    rw_pad = jnp.concatenate([rw_hi, (rw32 - rw_hi.astype(F32)).astype(BF16)], axis=1)
    rb_pad = jnp.pad(router_b.astype(F32), (0, LANES - N_EXPERTS)).reshape(1, LANES)
    fnw = final_norm_w.reshape(1, d).astype(F32)

    for l in range(depth):
        w_pad, fb_pad = _prep_w_in(w_in[l], fox_f_bias[l])
        qkv, su, hqig, hf, cum_t = _in_proj(x, mod[l], norm_mix_w[l].reshape(1, d), w_pad, fb_pad, tm=1024)
        o_fox = _fox_attention(qkv, cum_t, tq=256, tk=512)
        bmat, ar, ai, cmat = _prep_s5(s5_a_re[l], s5_a_im[l], s5_b_re[l], s5_b_im[l],
                                      s5_c_re[l], s5_c_im[l], s5_log_dt[l], bsz)
        o_s5 = _s5_mixer(su, bmat, ar, ai, cmat, s5_d[l].reshape(1, S5_WIDTH).astype(F32),
                         s5_glu_w[l].astype(BF16), s5_glu_b[l].reshape(1, S5_WIDTH).astype(F32), ln=128)
        nw2 = jnp.tile(hgrn_norm_w[l].astype(F32), 2).reshape(1, LANES)
        o_hgrn = _hgrn_mixer(hqig, hf, lower[l].reshape(1, -1), om_lb[l].reshape(1, -1), nw2, tc=128)
        x = _outproj_moe(o_fox, o_s5, o_hgrn, x, mod[l],
                         branch_norm_fox[l].reshape(1, -1).astype(F32),
                         branch_norm_s5[l].reshape(1, -1).astype(F32),
                         w_out[l].astype(BF16),
                         norm_ffn_w[l].reshape(1, d).astype(F32), rw_pad, rb_pad,
                         moe_w_gate[l].astype(BF16), moe_w_up[l].astype(BF16), moe_w_down[l].astype(BF16),
                         fnw, tm=1024, final_norm=(l == depth - 1))
    return x
```

```python
import functools
import math

import jax
import jax.numpy as jnp
from jax import lax
from jax.experimental import pallas as pl
from jax.experimental.pallas import tpu as pltpu

F32 = jnp.float32
BF16 = jnp.bfloat16
HIGHEST = lax.Precision.HIGHEST

D_MODEL = 1024
FOX_HEADS = 6
HEAD_DIM = 64
FOX_WIDTH = 384
S5_WIDTH = 256
S5_GROUPS = 16
S5_CH = 16
S5_STATE = 64
S5_NSTATE = S5_GROUPS * S5_STATE
HGRN_WIDTH = 384
N_EXPERTS = 16
N_GROUPS = 4
D_FF = 256
EPS = 1e-6
LANES = 128
NEG_BIG = -1e30
LOG2E = math.log2(math.e)
SOFTMAX_ROWS = 64
S5_PIECE = 64
MOE_SUB = 256
MOE_CAP = 96

C_QKV = (0, 1152)
C_SU = (1152, 1408)
C_HQIG = (1408, 2560)
C_HF = (2560, 2944)
C_FF = (2944, 3072)
N_IN_PAD = 3072

NT_DIMS = (((1,), (1,)), ((), ()))

VMEM_LIMIT = 56 * 1024 * 1024


def _cparams(sem):
    return pltpu.CompilerParams(dimension_semantics=sem, vmem_limit_bytes=VMEM_LIMIT)


def _sigmoid(x):
    return 1.0 / (1.0 + jnp.exp(-x))


def _silu(x):
    hx = 0.5 * x
    return hx + hx * jnp.tanh(hx)


def _log_sigmoid(x):
    return jnp.minimum(x, 0.0) - jnp.log1p(jnp.exp(-jnp.abs(x)))


def _rms(x):
    return x * lax.rsqrt(jnp.mean(x * x, axis=-1, keepdims=True) + EPS)


def _ada_kernel(c_ref, w_ref, b_ref, o_ref):
    c = c_ref[...]
    o_ref[...] = jnp.dot(_silu(c), w_ref[...], precision=HIGHEST,
                         preferred_element_type=F32) + b_ref[...]


def _ada_mod(c, ada_w, ada_b):
    depth, d, n6 = ada_w.shape
    bsz = c.shape[0]
    nblk = n6 // d
    out = pl.pallas_call(
        _ada_kernel,
        grid=(depth, nblk),
        in_specs=[
            pl.BlockSpec((bsz, d), lambda l, j: (0, 0)),
            pl.BlockSpec((None, d, d), lambda l, j: (l, 0, j)),
            pl.BlockSpec((None, 1, d), lambda l, j: (l, 0, j)),
        ],
        out_specs=pl.BlockSpec((None, bsz, d), lambda l, j: (l, 0, j)),
        out_shape=jax.ShapeDtypeStruct((depth, bsz, n6), F32),
        compiler_params=_cparams(("arbitrary", "arbitrary")),
        name="ada_mod",
    )(c, ada_w, ada_b.reshape(depth, 1, n6))
    return out.reshape(depth, bsz, nblk, d)


def _inproj_kernel(x_ref, mod_ref, nw_ref, w_ref, fb_ref,
                   qkv_ref, su_ref, hqig_ref, hf_ref, cum_ref, carry_ref):
    i = pl.program_id(1)
    tm = x_ref.shape[0]

    @pl.when(i == 0)
    def _():
        carry_ref[...] = jnp.zeros_like(carry_ref)

    h = _rms(x_ref[...]) * nw_ref[...]
    h = h * (1.0 + mod_ref[1:2, :]) + mod_ref[0:1, :]
    hb = h.astype(BF16)

    half = N_IN_PAD // 2
    z = (jnp.dot(hb, w_ref[:, 0:half], preferred_element_type=F32),
         jnp.dot(hb, w_ref[:, half:], preferred_element_type=F32))

    def proj(c):
        lo, hi = c
        if hi <= half:
            return z[0][:, lo:hi]
        if lo >= half:
            return z[1][:, lo - half:hi - half]
        return jnp.concatenate([z[0][:, lo:], z[1][:, :hi - half]], axis=1)

    qkv_ref[...] = proj(C_QKV).astype(BF16)
    su = proj(C_SU)
    su_ref[0] = su[:, :LANES]
    su_ref[1] = su[:, LANES:]
    hqig_ref[...] = proj(C_HQIG).astype(BF16)
    hf_ref[...] = proj(C_HF)

    lf = _log_sigmoid(proj(C_FF) + fb_ref[...])
    lf_t = lf.T[0:8, :]
    r = lax.broadcasted_iota(jnp.int32, (tm, tm), 0)
    c = lax.broadcasted_iota(jnp.int32, (tm, tm), 1)
    tri_u = jnp.where(r <= c, 1.0, 0.0).astype(BF16)
    hi = lf_t.astype(BF16).astype(F32)
    r1 = lf_t - hi
    mid = r1.astype(BF16).astype(F32)
    pieces = jnp.concatenate([hi, mid, r1 - mid], axis=0).astype(BF16)
    parts = jnp.dot(pieces, tri_u, preferred_element_type=F32)
    cum = parts[0:8] + parts[8:16] + parts[16:24] + carry_ref[:, 0:1]
    cum_ref[...] = cum * LOG2E
    carry_ref[...] = jnp.broadcast_to(cum[:, tm - 1:tm], carry_ref.shape)


def _in_proj(x, mod_l, norm_w, w_pad, fb_pad, tm):
    bsz, seq, d = x.shape
    tm = min(tm, seq)
    return pl.pallas_call(
        _inproj_kernel,
        grid=(bsz, seq // tm),
        in_specs=[
            pl.BlockSpec((None, tm, d), lambda b, i: (b, i, 0)),
            pl.BlockSpec((None, 6, d), lambda b, i: (b, 0, 0)),
            pl.BlockSpec((1, d), lambda b, i: (0, 0)),
            pl.BlockSpec((d, N_IN_PAD), lambda b, i: (0, 0)),
            pl.BlockSpec((1, LANES), lambda b, i: (0, 0)),
        ],
        out_specs=[
            pl.BlockSpec((None, tm, 1152), lambda b, i: (b, i, 0)),
            pl.BlockSpec((2, None, tm, LANES), lambda b, i: (0, b, i, 0)),
            pl.BlockSpec((None, tm, 1152), lambda b, i: (b, i, 0)),
            pl.BlockSpec((None, tm, HGRN_WIDTH), lambda b, i: (b, i, 0)),
            pl.BlockSpec((None, 8, tm), lambda b, i: (b, 0, i)),
        ],
        out_shape=[
            jax.ShapeDtypeStruct((bsz, seq, 1152), BF16),
            jax.ShapeDtypeStruct((2, bsz, seq, LANES), F32),
            jax.ShapeDtypeStruct((bsz, seq, 1152), BF16),
            jax.ShapeDtypeStruct((bsz, seq, HGRN_WIDTH), F32),
            jax.ShapeDtypeStruct((bsz, 8, seq), F32),
        ],
        scratch_shapes=[pltpu.VMEM((8, LANES), F32)],
        compiler_params=_cparams(("arbitrary", "arbitrary")),
        name="in_proj",
    )(x, mod_l, norm_w, w_pad, fb_pad)


def _fox_kernel(q_ref, k_ref, v_ref, cum_ref, o_ref, s_sc, p_sc, al_sc, m_sc, l_sc, acc_sc, d_sc,
                *, tq, tk, n_steps, unroll):
    p = pl.program_id(1)
    seq = q_ref.shape[0]
    nq = seq // tq
    lane = lax.broadcasted_iota(jnp.int32, (tq, LANES), 1)
    first = lane < HEAD_DIM

    def advance(i, j):
        last = j == (i * tq) // tk
        return jnp.minimum(i + last.astype(jnp.int32), nq - 1), jnp.where(last, 0, j + 1)

    def stage_qk(i, j):
        q = q_ref[pl.ds(pl.multiple_of(i * tq, tq), tq), :]
        zero = jnp.zeros_like(q)
        q2 = jnp.concatenate([jnp.where(first, q, zero), jnp.where(first, zero, q)], axis=0)
        k0 = pl.multiple_of(j * tk, tk)
        s = lax.dot_general(q2, k_ref[pl.ds(k0, tk), :], NT_DIMS, preferred_element_type=F32)
        bias_a = cum_ref[pl.ds(2 * p, 1), pl.ds(k0, tk)]
        bias_b = cum_ref[pl.ds(2 * p + 1, 1), pl.ds(k0, tk)]
        is_last = j == (i * tq) // tk
        mask = d_sc[jnp.where(is_last, 1 + i - j * (tk // tq), 0)]
        s_sc[...] = jnp.concatenate([s[:tq] + (mask - bias_a), s[tq:] + (mask - bias_b)], axis=0)

    def stage_softmax(j):
        cap = jnp.where(j == 0, NEG_BIG, -NEG_BIG)
        for r0 in range(0, 2 * tq, SOFTMAX_ROWS):
            rs = slice(r0, r0 + SOFTMAX_ROWS)
            m_prev = jnp.minimum(m_sc[rs, :], cap)
            s = s_sc[rs, :]
            m_next = jnp.maximum(m_prev, jnp.max(s, axis=1, keepdims=True))
            pr = jnp.exp2(s - jnp.tile(m_next, (1, tk // LANES)))
            alpha = jnp.exp2(m_prev - m_next)
            l_sc[rs, :] = alpha * l_sc[rs, :] + jnp.sum(pr, axis=1, keepdims=True)
            m_sc[rs, :] = m_next
            p_sc[rs, :] = pr.astype(BF16)
            al_sc[rs, :] = alpha

    def stage_pv(i, j):
        k0 = pl.multiple_of(j * tk, tk)
        acc = acc_sc[...] * al_sc[...] + jnp.dot(p_sc[...], v_ref[pl.ds(k0, tk), :],
                                                  preferred_element_type=F32)
        acc_sc[...] = acc
        o = acc / l_sc[...]
        o_ref[pl.ds(pl.multiple_of(i * tq, tq), tq), :] = jnp.where(first, o[:tq], o[tq:]).astype(BF16)

    col_minus_row = (lax.broadcasted_iota(jnp.int32, (tq, tk), 1)
                     - lax.broadcasted_iota(jnp.int32, (tq, tk), 0))
    d_sc[0] = jnp.zeros((tq, tk), F32)
    for k in range(tk // tq):
        d_sc[1 + k] = jnp.where(col_minus_row <= k * tq, 0.0, NEG_BIG)
    s_sc[...] = jnp.zeros(s_sc.shape, F32)
    p_sc[...] = jnp.zeros(p_sc.shape, BF16)
    al_sc[...] = jnp.ones(al_sc.shape, F32)
    m_sc[...] = jnp.full(m_sc.shape, NEG_BIG, F32)
    l_sc[...] = jnp.ones(l_sc.shape, F32)
    acc_sc[...] = jnp.zeros(acc_sc.shape, F32)

    def body(t, carry):
        ia, ja, ib, jb, ic, jc = carry
        stage_pv(ic, jc)
        stage_softmax(jb)
        stage_qk(ia, ja)
        na, nja = advance(ia, ja)
        return na, nja, ia, ja, ib, jb

    zero = jnp.int32(0)
    lax.fori_loop(0, n_steps + 2, body, (zero, zero, zero, zero, zero, zero), unroll=unroll)


def _fox_attention(qkv, cum_t, tq, tk):
    bsz, seq, _ = qkv.shape
    tq = min(tq, seq)
    tk = min(max(tk, tq), seq)
    assert tk % tq == 0 and seq % tk == 0
    npair = FOX_HEADS // 2
    n_steps = sum((i * tq) // tk + 1 for i in range(seq // tq))
    kern = functools.partial(_fox_kernel, tq=tq, tk=tk, n_steps=n_steps,
                             unroll=2 if n_steps % 2 == 0 else 1)
    col = lambda off: pl.BlockSpec((None, seq, LANES), lambda b, p: (b, 0, off + p))
    return pl.pallas_call(
        kern,
        grid=(bsz, npair),
        in_specs=[col(0), col(npair), col(2 * npair),
                  pl.BlockSpec((None, 8, seq), lambda b, p: (b, 0, 0))],
        out_specs=col(0),
        out_shape=jax.ShapeDtypeStruct((bsz, seq, FOX_WIDTH), BF16),
        scratch_shapes=[
            pltpu.VMEM((2 * tq, tk), F32),
            pltpu.VMEM((2 * tq, tk), BF16),
            pltpu.VMEM((2 * tq, LANES), F32),
            pltpu.VMEM((2 * tq, LANES), F32),
            pltpu.VMEM((2 * tq, LANES), F32),
            pltpu.VMEM((2 * tq, LANES), F32),
            pltpu.VMEM((1 + tk // tq, tq, tk), F32),
        ],
        compiler_params=_cparams(("arbitrary", "arbitrary")),
        name="fox_attn",
    )(qkv, qkv, qkv, cum_t)


def _gelu_tanh(x):
    return 0.5 * x * (1.0 + jnp.tanh(math.sqrt(2.0 / math.pi) * (x + 0.044715 * (x * x * x))))


def _s5_kernel(su_ref, bm_ref, ar_ref, ai_ref, cm_ref, d_ref, gw_ref, gb_ref, o_ref,
               utb, bu, ytb, st):
    i = pl.program_id(0)
    nb = su_ref.shape[1]
    ln = su_ref.shape[2]
    ns = S5_NSTATE

    @pl.when(i == 0)
    def _():
        st[...] = jnp.zeros_like(st)

    for k in range(2):
        for b in range(nb):
            utb[k, pl.ds(b, ln, stride=nb), :] = su_ref[k, b]
    u = jnp.concatenate([utb[0], utb[1]], axis=1)
    ub = u.astype(BF16)
    ar = ar_ref[...]
    ai = ai_ref[...]
    re, im = st[:, 0:ns], st[:, ns:2 * ns]
    npiece = max(ln // S5_PIECE, 1)
    prow = (ln // npiece) * nb
    for c in range(npiece):
        rows_c = slice(c * prow, (c + 1) * prow)
        bu[rows_c, :] = jnp.dot(ub[rows_c, :], bm_ref[...], preferred_element_type=F32)
    ys = []
    for c in range(npiece):
        for t in range(c * prow // nb, (c + 1) * prow // nb):
            r0 = t * nb
            nre = ar * re - ai * im + bu[r0:r0 + nb, 0:ns]
            nim = ar * im + ai * re + bu[r0:r0 + nb, ns:2 * ns]
            bu[r0:r0 + nb, 0:ns] = nre
            bu[r0:r0 + nb, ns:2 * ns] = nim
            re, im = nre, nim
        rows_c = slice(c * prow, (c + 1) * prow)
        ys.append(jnp.dot(bu[rows_c, :].astype(BF16), cm_ref[...], preferred_element_type=F32))
    st[:, 0:ns] = re
    st[:, ns:2 * ns] = im

    y = jnp.concatenate(ys, axis=0) + d_ref[...] * u
    y = _gelu_tanh(y)
    gate = _sigmoid(jnp.dot(y.astype(BF16), gw_ref[...], preferred_element_type=F32) + gb_ref[...])
    out = y * gate
    ytb[0] = out[:, :LANES]
    ytb[1] = out[:, LANES:]
    for k in range(2):
        for b in range(nb):
            o_ref[b, :, k * LANES:(k + 1) * LANES] = ytb[k, pl.ds(b, ln, stride=nb), :].astype(BF16)


def _s5_mixer(su, bmat, ar, ai, cmat, dvec, glu_w, glu_b, ln):
    _, bsz, seq, _ = su.shape
    ln = min(ln, seq)
    rows = bsz * ln
    const = lambda shape: pl.BlockSpec(shape, lambda i: (0,) * len(shape))
    return pl.pallas_call(
        _s5_kernel,
        grid=(seq // ln,),
        in_specs=[
            pl.BlockSpec((2, bsz, ln, LANES), lambda i: (0, 0, i, 0)),
            const((S5_WIDTH, 2 * S5_NSTATE)),
            const((bsz, S5_NSTATE)),
            const((bsz, S5_NSTATE)),
            const((2 * S5_NSTATE, S5_WIDTH)),
            const((1, S5_WIDTH)),
            const((S5_WIDTH, S5_WIDTH)),
            const((1, S5_WIDTH)),
        ],
        out_specs=pl.BlockSpec((bsz, ln, S5_WIDTH), lambda i: (0, i, 0)),
        out_shape=jax.ShapeDtypeStruct((bsz, seq, S5_WIDTH), BF16),
        scratch_shapes=[
            pltpu.VMEM((2, rows, LANES), F32),
            pltpu.VMEM((rows, 2 * S5_NSTATE), F32),
            pltpu.VMEM((2, rows, LANES), F32),
            pltpu.VMEM((bsz, 2 * S5_NSTATE), F32),
        ],
        compiler_params=_cparams(("arbitrary",)),
        name="s5_mixer",
    )(su, bmat, ar, ai, cmat, dvec, glu_w, glu_b)


def _block_row(a, blk, r):
    t = a.shape[0]
    if blk >= 8:
        a3 = a.reshape(t // blk, blk, LANES)
        return jnp.broadcast_to(a3[:, r:r + 1, :], a3.shape).reshape(t, LANES)
    if blk == 4:
        a3 = a.reshape(t // 8, 8, LANES)
        sub = lax.broadcasted_iota(jnp.int32, a3.shape, 1)
        res = jnp.where(sub < 4, jnp.broadcast_to(a3[:, r:r + 1, :], a3.shape),
                        jnp.broadcast_to(a3[:, 4 + r:5 + r, :], a3.shape))
        return res.reshape(t, LANES)
    row = lax.broadcasted_iota(jnp.int32, a.shape, 0)
    res = a
    for q in range(blk):
        if q != r:
            res = jnp.where(row % blk == q, pltpu.roll(a, (q - r) % t, axis=0), res)
    return res


def _hgrn_kernel(hq_ref, hi_ref, hg_ref, hf_ref, lb_ref, omlb_ref, nw_ref, o_ref, lvl_sc, mask_sc, lo_sc,
                 *, tc):
    seq = hf_ref.shape[0]
    nlev = tc.bit_length() - 1
    lane = lax.broadcasted_iota(jnp.int32, (tc, LANES), 1)
    first = lane < HEAD_DIM
    row = lax.broadcasted_iota(jnp.int32, (tc, LANES), 0)
    rr = lax.broadcasted_iota(jnp.int32, (tc, tc), 0)
    cc = lax.broadcasted_iota(jnp.int32, (tc, tc), 1)
    diff = rr ^ cc
    bits = jnp.zeros((tc, tc), jnp.int32)
    for i in range(nlev):
        bits = bits + jnp.where((diff >> i) != 0, 1, 0)
    lvl_sc[...] = jnp.where(rr >= cc, bits, -1)
    for i in range(nlev):
        upper = ((row >> i) & 1) == 1
        mask_sc[i, 0] = jnp.where(jnp.logical_and(upper, first), 1.0, 0.0).astype(BF16)
        mask_sc[i, 1] = jnp.where(jnp.logical_and(upper, jnp.logical_not(first)), 1.0, 0.0).astype(BF16)
        mask_sc[i, 2] = jnp.where(upper, 0.0, 1.0).astype(BF16)
        lo_sc[i] = jnp.where(upper, 0.0, 1.0)
    sr = lax.broadcasted_iota(jnp.int32, (LANES, LANES), 0)
    sc = lax.broadcasted_iota(jnp.int32, (LANES, LANES), 1)
    blockdiag = (sr < HEAD_DIM) == (sc < HEAD_DIM)
    lb = lb_ref[...]
    omlb = omlb_ref[...]
    nw = nw_ref[...]

    def chunk(c, st):
        r0 = pl.multiple_of(c * tc, tc)
        z = hf_ref[pl.ds(r0, tc), :]
        q = _silu(hq_ref[pl.ds(r0, tc), :].astype(F32))
        v = hi_ref[pl.ds(r0, tc), :]
        g = hg_ref[pl.ds(r0, tc), :].astype(F32)
        e = jnp.exp(-jnp.abs(z))
        s_big = 1.0 / (1.0 + e)
        s_small = e * s_big
        pos = z >= 0.0
        f = lb + omlb * jnp.where(pos, s_big, s_small)
        kk = omlb * jnp.where(pos, s_small, s_big)

        def scores(q_a, q_b, km):
            return lax.dot_general(jnp.concatenate([q_a, q_b], axis=0), km, NT_DIMS,
                                   preferred_element_type=F32)

        s = scores(jnp.where(first, q, 0.0).astype(BF16), jnp.where(first, 0.0, q).astype(BF16),
                   kk.astype(BF16))
        on_diag = lvl_sc[...] == 0
        tot_a = jnp.where(on_diag, s[:tc], 0.0)
        tot_b = jnp.where(on_diag, s[tc:], 0.0)
        a_m = f
        b_m = jnp.ones_like(f)
        for i in range(nlev):
            m = 1 << i
            qa = (q * a_m).astype(BF16)
            s = scores(qa * mask_sc[i, 0], qa * mask_sc[i, 1], (kk * b_m).astype(BF16) * mask_sc[i, 2])
            sel = lvl_sc[...] == i + 1
            tot_a = jnp.where(sel, s[:tc], tot_a)
            tot_b = jnp.where(sel, s[tc:], tot_b)
            lower_total = _block_row(a_m, 2 * m, m - 1)
            upper_total = _block_row(a_m, 2 * m, 2 * m - 1)
            upper = lo_sc[i] < 0.5
            a_m = jnp.where(upper, a_m * lower_total, a_m)
            b_m = jnp.where(upper, b_m, b_m * upper_total)

        o = lax.dot_general((q * a_m).astype(BF16), st.astype(BF16), NT_DIMS, preferred_element_type=F32)
        pv = jnp.dot(jnp.concatenate([tot_a, tot_b], axis=0).astype(BF16), v, preferred_element_type=F32)
        o = o + jnp.where(first, pv[:tc], pv[tc:])

        upd = jnp.dot(v.astype(F32).T.astype(BF16), (kk * b_m).astype(BF16), preferred_element_type=F32)
        st_new = st * a_m[tc - 1:tc, :] + jnp.where(blockdiag, upd, 0.0)

        o2 = o * o
        s_a = jnp.sum(jnp.where(first, o2, 0.0), axis=-1, keepdims=True)
        s_b = jnp.sum(jnp.where(first, 0.0, o2), axis=-1, keepdims=True)
        ms = jnp.where(first, s_a, s_b) * (1.0 / HEAD_DIM)
        o_ref[pl.ds(r0, tc), :] = (o * lax.rsqrt(ms + EPS) * nw * _silu(g)).astype(BF16)
        return st_new

    lax.fori_loop(0, seq // tc, chunk, jnp.zeros((LANES, LANES), F32), unroll=4)


def _hgrn_mixer(hqig, hf, lb, om_lb, nw2, tc):
    bsz, seq, _ = hf.shape
    tc = min(tc, seq)
    npair = HGRN_WIDTH // LANES
    kern = functools.partial(_hgrn_kernel, tc=tc)
    col = lambda off: pl.BlockSpec((None, seq, LANES), lambda b, p: (b, 0, off + p))
    par = pl.BlockSpec((1, LANES), lambda b, p: (0, p))
    return pl.pallas_call(
        kern,
        grid=(bsz, npair),
        in_specs=[col(0), col(npair), col(2 * npair), col(0), par, par,
                  pl.BlockSpec((1, LANES), lambda b, p: (0, 0))],
        out_specs=col(0),
        out_shape=jax.ShapeDtypeStruct((bsz, seq, HGRN_WIDTH), BF16),
        scratch_shapes=[
            pltpu.VMEM((tc, tc), jnp.int32),
            pltpu.VMEM((tc.bit_length() - 1, 3, tc, LANES), BF16),
            pltpu.VMEM((tc.bit_length() - 1, tc, LANES), F32),
        ],
        compiler_params=_cparams(("arbitrary", "arbitrary")),
        name="hgrn_mixer",
    )(hqig, hqig, hqig, hf, lb, om_lb, nw2)


def _routing(logits_t):
    mx = jnp.max(logits_t, axis=0, keepdims=True)
    ex = jnp.exp(logits_t - mx)
    probs = ex / jnp.sum(ex, axis=0, keepdims=True)
    p = [probs[e:e + 1, :] for e in range(N_EXPERTS)]
    epg = N_EXPERTS // N_GROUPS
    scores = []
    for g in range(N_GROUPS):
        a, b, c, d = p[epg * g:epg * g + epg]
        hi1, lo1 = jnp.maximum(a, b), jnp.minimum(a, b)
        hi2, lo2 = jnp.maximum(c, d), jnp.minimum(c, d)
        top1 = jnp.maximum(hi1, hi2)
        top2 = jnp.maximum(jnp.minimum(hi1, hi2), jnp.maximum(lo1, lo2))
        scores.append(top1 + top2)
    best = jnp.zeros_like(scores[0], dtype=jnp.int32)
    bs = scores[0]
    for g in range(1, N_GROUPS):
        upd = scores[g] > bs
        best = jnp.where(upd, g, best)
        bs = jnp.where(upd, scores[g], bs)
    ig = []
    for j in range(epg):
        val = p[j]
        for g in range(1, N_GROUPS):
            val = jnp.where(best == g, p[epg * g + j], val)
        ig.append(val)
    i1 = jnp.zeros_like(best)
    w1 = ig[0]
    for j in range(1, epg):
        upd = ig[j] > w1
        i1 = jnp.where(upd, j, i1)
        w1 = jnp.where(upd, ig[j], w1)
    i2 = jnp.zeros_like(best)
    w2 = jnp.full_like(w1, -1.0)
    for j in range(epg):
        upd = jnp.logical_and(i1 != j, ig[j] > w2)
        i2 = jnp.where(upd, j, i2)
        w2 = jnp.where(upd, ig[j], w2)
    den = w1 + w2
    tw1 = w1 / den
    tw2 = w2 / den
    e1 = best * epg + i1
    e2 = best * epg + i2
    return [jnp.where(e1 == e, tw1, 0.0) + jnp.where(e2 == e, tw2, 0.0) for e in range(N_EXPERTS)], best


def _moe_kernel(fox_ref, s5_ref, hg_ref, x_ref, mod_ref, nf_ref, ns_ref, wo_ref,
                nw_ref, rw_ref, rb_ref, wg_ref, wu_ref, wd_ref, fnw_ref,
                o_ref, h_sc, comb_sc, combt_sc, hs_sc, ws_sc, ys_sc, acc_sc, cnt_sm, *, final_norm):
    grp = pl.program_id(2)
    tm, d = x_ref.shape
    epg = N_EXPERTS // N_GROUPS
    sub, cap = MOE_SUB, MOE_CAP
    ns = tm // sub
    gc = N_GROUPS * cap
    row_oh, row_slot = N_EXPERTS, N_EXPERTS + 8

    @pl.when(grp == 0)
    def _():
        for s in range(ns):
            toks = slice(s * sub, (s + 1) * sub)
            of = (_rms(fox_ref[toks, :].astype(F32)) * nf_ref[...]).astype(BF16)
            os5 = (_rms(s5_ref[toks, :].astype(F32)) * ns_ref[...]).astype(BF16)
            mixed = jnp.concatenate([of, os5, hg_ref[toks, :]], axis=1)
            mix = jnp.dot(mixed, wo_ref[...], preferred_element_type=F32)
            o_ref[toks, :] = x_ref[toks, :] + mod_ref[2:3, :] * mix
        h = _rms(o_ref[...]) * nw_ref[...]
        h = h * (1.0 + mod_ref[4:5, :]) + mod_ref[3:4, :]
        hb = h.astype(BF16)
        h_sc[...] = hb
        h_lo = (h - hb.astype(F32)).astype(BF16)
        hi_part = jnp.dot(hb, rw_ref[...], preferred_element_type=F32)
        lo_part = jnp.dot(h_lo, rw_ref[:, 0:LANES], preferred_element_type=F32)
        logits = hi_part[:, 0:LANES] + hi_part[:, LANES:] + lo_part + rb_ref[...]
        rows, best = _routing(logits.T[0:N_EXPERTS, :])
        combt_sc[...] = jnp.zeros_like(combt_sc)
        for j in range(N_EXPERTS):
            combt_sc[j:j + 1, :] = rows[j]
        for g in range(N_GROUPS):
            combt_sc[row_oh + g:row_oh + g + 1, :] = jnp.where(best == g, 1.0, 0.0)
        oh8 = combt_sc[row_oh:row_oh + 8, :]
        r = lax.broadcasted_iota(jnp.int32, (sub, sub), 0)
        c = lax.broadcasted_iota(jnp.int32, (sub, sub), 1)
        tri_u = jnp.where(r <= c, 1.0, 0.0).astype(BF16)
        incl = [jnp.dot(oh8[:, s * sub:(s + 1) * sub].astype(BF16), tri_u, preferred_element_type=F32)
                for s in range(ns)]
        cmax = incl[0][:, sub - 1:sub]
        for s in range(1, ns):
            cmax = jnp.maximum(cmax, incl[s][:, sub - 1:sub])
        for g in range(N_GROUPS):
            cnt_sm[g] = jnp.max(cmax[g:g + 1, :]).astype(jnp.int32)
        over = jnp.where(cmax > cap, 1.0, 0.0)
        best_f = best.astype(F32)
        for s in range(ns):
            cols = slice(s * sub, (s + 1) * sub)
            oh_s = oh8[:, cols]
            rank = jnp.sum(oh_s * incl[s], axis=0, keepdims=True) - 1.0
            dense = jnp.sum(oh_s * over, axis=0, keepdims=True)
            combt_sc[row_slot:row_slot + 1, cols] = jnp.where(dense > 0.5, -1.0, best_f[:, cols] * cap + rank)
        comb_sc[...] = combt_sc[...].T
        for s in range(ns):
            toks = slice(s * sub, (s + 1) * sub)
            slot_row = combt_sc[row_slot:row_slot + 1, toks].astype(jnp.int32)
            ri = lax.broadcasted_iota(jnp.int32, (gc, sub), 0)
            p = jnp.where(ri == slot_row, 1.0, 0.0).astype(BF16)
            hs_all = jnp.dot(p, h_sc[toks, :], preferred_element_type=F32).astype(BF16)
            cs = comb_sc[toks, :]
            c_hi = cs.astype(BF16)
            c_lo = (cs - c_hi.astype(F32)).astype(BF16)
            ws_all = jnp.dot(p, jnp.concatenate([c_hi, c_lo], axis=1), preferred_element_type=F32)
            ws_all = ws_all[:, 0:LANES] + ws_all[:, LANES:]
            for g in range(N_GROUPS):
                hs_sc[g, s * cap:(s + 1) * cap, :] = hs_all[g * cap:(g + 1) * cap, :]
                ws_sc[g, s * cap:(s + 1) * cap, :] = ws_all[g * cap:(g + 1) * cap, :]
        acc_sc[...] = jnp.zeros_like(acc_sc)

    def experts(h_rows, w_rows):
        lane = lax.broadcasted_iota(jnp.int32, w_rows.shape, 1)
        hid = []
        for j in range(epg):
            gate = jnp.dot(h_rows, wg_ref[j], preferred_element_type=F32)
            up = jnp.dot(h_rows, wu_ref[j], preferred_element_type=F32)
            w_e = jnp.sum(jnp.where(lane == grp * epg + j, w_rows, 0.0), axis=-1, keepdims=True)
            hid.append((_silu(gate) * up * w_e).astype(BF16))
        wd = wd_ref[...].reshape(epg * D_FF, d)
        return jnp.dot(jnp.concatenate(hid, axis=1), wd, preferred_element_type=F32)

    cnt = cnt_sm[grp]

    @pl.when(cnt <= cap)
    def _():
        ys = experts(hs_sc[grp], ws_sc[grp]).astype(BF16)
        for s in range(ns):
            ys_sc[s, grp] = ys[s * cap:(s + 1) * cap, :]

    @pl.when(cnt > cap)
    def _():
        acc_sc[...] += experts(h_sc[...], comb_sc[...])
        for s in range(ns):
            ys_sc[s, grp] = jnp.zeros((cap, d), BF16)

    @pl.when(grp == N_GROUPS - 1)
    def _():
        for s in range(ns):
            toks = slice(s * sub, (s + 1) * sub)
            slot_col = comb_sc[toks, row_slot:row_slot + 1].astype(jnp.int32)
            li = lax.broadcasted_iota(jnp.int32, (sub, gc), 1)
            pt = jnp.where(li == slot_col, 1.0, 0.0).astype(BF16)
            y = jnp.dot(pt, ys_sc[s].reshape(gc, d), preferred_element_type=F32) + acc_sc[toks, :]
            y = o_ref[toks, :] + mod_ref[5:6, :] * y
            if final_norm:
                y = _rms(y) * fnw_ref[...]
            o_ref[toks, :] = y


def _outproj_moe(o_fox, o_s5, o_hgrn, x, mod_l, nf, ns_w, w_out, norm_w, rw_pad, rb_pad, wg, wu, wd, fnw,
                 tm, final_norm):
    bsz, seq, d = x.shape
    tm = min(tm, seq)
    assert tm % MOE_SUB == 0
    ns = tm // MOE_SUB
    epg = N_EXPERTS // N_GROUPS
    kern = functools.partial(_moe_kernel, final_norm=final_norm)
    row = lambda w: pl.BlockSpec((None, tm, w), lambda b, i, e: (b, i, 0))
    return pl.pallas_call(
        kern,
        grid=(bsz, seq // tm, N_GROUPS),
        in_specs=[
            row(FOX_WIDTH), row(S5_WIDTH), row(HGRN_WIDTH), row(d),
            pl.BlockSpec((None, 6, d), lambda b, i, e: (b, 0, 0)),
            pl.BlockSpec((1, FOX_WIDTH), lambda b, i, e: (0, 0)),
            pl.BlockSpec((1, S5_WIDTH), lambda b, i, e: (0, 0)),
            pl.BlockSpec((d, d), lambda b, i, e: (0, 0)),
            pl.BlockSpec((1, d), lambda b, i, e: (0, 0)),
            pl.BlockSpec((d, 2 * LANES), lambda b, i, e: (0, 0)),
            pl.BlockSpec((1, LANES), lambda b, i, e: (0, 0)),
            pl.BlockSpec((epg, d, D_FF), lambda b, i, e: (e, 0, 0)),
            pl.BlockSpec((epg, d, D_FF), lambda b, i, e: (e, 0, 0)),
            pl.BlockSpec((epg, D_FF, d), lambda b, i, e: (e, 0, 0)),
            pl.BlockSpec((1, d), lambda b, i, e: (0, 0)),
        ],
        out_specs=pl.BlockSpec((None, tm, d), lambda b, i, e: (b, i, 0)),
        out_shape=jax.ShapeDtypeStruct((bsz, seq, d), F32),
        scratch_shapes=[
            pltpu.VMEM((tm, d), BF16),
            pltpu.VMEM((tm, LANES), F32),
            pltpu.VMEM((LANES, tm), F32),
            pltpu.VMEM((N_GROUPS, ns * MOE_CAP, d), BF16),
            pltpu.VMEM((N_GROUPS, ns * MOE_CAP, LANES), F32),
            pltpu.VMEM((ns, N_GROUPS, MOE_CAP, d), BF16),
            pltpu.VMEM((tm, d), F32),
            pltpu.SMEM((N_GROUPS,), jnp.int32),
        ],
        compiler_params=_cparams(("arbitrary", "arbitrary", "arbitrary")),
        name="moe",
    )(o_fox, o_s5, o_hgrn, x, mod_l, nf, ns_w, w_out, norm_w, rw_pad, rb_pad, wg, wu, wd, fnw)


def _prep_w_in(w_in_l, f_bias_l):
    a = FOX_WIDTH
    o_ff = 3 * a
    o_su = o_ff + FOX_HEADS
    o_hq = o_su + S5_WIDTH
    hw = HGRN_WIDTH
    fq = w_in_l[:, 0:a] * (HEAD_DIM ** -0.5 * LOG2E)
    fk, fv = w_in_l[:, a:2 * a], w_in_l[:, 2 * a:3 * a]
    ff = jnp.pad(w_in_l[:, o_ff:o_su], ((0, 0), (0, LANES - FOX_HEADS)))
    su = w_in_l[:, o_su:o_hq]
    hq = w_in_l[:, o_hq:o_hq + hw]
    hf = w_in_l[:, o_hq + hw:o_hq + 2 * hw]
    hi = w_in_l[:, o_hq + 2 * hw:o_hq + 3 * hw]
    hg = w_in_l[:, o_hq + 3 * hw:o_hq + 4 * hw]
    w = jnp.concatenate([fq, fk, fv, su, hq, hi, hg, hf, ff], axis=1).astype(BF16)
    fb = jnp.pad(f_bias_l.astype(F32), (0, LANES - FOX_HEADS)).reshape(1, LANES)
    return w, fb


def _prep_s5(a_re, a_im, b_re, b_im, c_re, c_im, log_dt, bsz):
    lam = lax.complex(a_re.astype(F32), a_im.astype(F32))
    dt = jnp.exp(log_dt.astype(F32))[:, None]
    a_bar = jnp.exp(lam * dt)
    b_bar = ((a_bar - 1.0) / lam)[..., None] * lax.complex(b_re.astype(F32), b_im.astype(F32))
    eye = jnp.eye(S5_GROUPS, dtype=F32)
    bm_re = jnp.einsum('gpc,gh->gchp', jnp.real(b_bar), eye).reshape(S5_WIDTH, S5_NSTATE)
    bm_im = jnp.einsum('gpc,gh->gchp', jnp.imag(b_bar), eye).reshape(S5_WIDTH, S5_NSTATE)
    bmat = jnp.concatenate([bm_re, bm_im], axis=1).astype(BF16)
    cm_re = jnp.einsum('gcp,gh->gphc', c_re.astype(F32), eye).reshape(S5_NSTATE, S5_WIDTH)
    cm_im = jnp.einsum('gcp,gh->gphc', c_im.astype(F32), eye).reshape(S5_NSTATE, S5_WIDTH)
    cmat = jnp.concatenate([cm_re, -cm_im], axis=0).astype(BF16)
    ar = jnp.broadcast_to(jnp.real(a_bar).reshape(1, S5_NSTATE), (bsz, S5_NSTATE))
    ai = jnp.broadcast_to(jnp.imag(a_bar).reshape(1, S5_NSTATE), (bsz, S5_NSTATE))
    return bmat, ar, ai, cmat


def kernel(x, c, ada_w, ada_b, norm_mix_w, norm_ffn_w, w_in, fox_f_bias, s5_a_re, s5_a_im, s5_b_re, s5_b_im, s5_c_re, s5_c_im, s5_d, s5_log_dt, s5_glu_w, s5_glu_b, hgrn_lb_logits, hgrn_norm_w, branch_norm_fox, branch_norm_s5, w_out, router_w, router_b, moe_w_gate, moe_w_up, moe_w_down, final_norm_w):
    bsz, seq, d = x.shape
    depth = w_in.shape[0]
    lb_cum = jnp.cumsum(jax.nn.softmax(hgrn_lb_logits.astype(F32), axis=0), axis=0)
    lower = lb_cum - lb_cum[0:1]
    om_lb = 1.0 - lower

    mod = _ada_mod(c, ada_w, ada_b)
    rw32 = jnp.pad(router_w.astype(F32), ((0, 0), (0, LANES - N_EXPERTS)))
    rw_hi = rw32.astype(BF16)
    rw_pad = jnp.concatenate([rw_hi, (rw32 - rw_hi.astype(F32)).astype(BF16)], axis=1)
    rb_pad = jnp.pad(router_b.astype(F32), (0, LANES - N_EXPERTS)).reshape(1, LANES)
    fnw = final_norm_w.reshape(1, d).astype(F32)

    for l in range(depth):
        w_pad, fb_pad = _prep_w_in(w_in[l], fox_f_bias[l])
        qkv, su, hqig, hf, cum_t = _in_proj(x, mod[l], norm_mix_w[l].reshape(1, d), w_pad, fb_pad, tm=1024)
        o_fox = _fox_attention(qkv, cum_t, tq=256, tk=512)
        bmat, ar, ai, cmat = _prep_s5(s5_a_re[l], s5_a_im[l], s5_b_re[l], s5_b_im[l],
                                      s5_c_re[l], s5_c_im[l], s5_log_dt[l], bsz)
        o_s5 = _s5_mixer(su, bmat, ar, ai, cmat, s5_d[l].reshape(1, S5_WIDTH).astype(F32),
                         s5_glu_w[l].astype(BF16), s5_glu_b[l].reshape(1, S5_WIDTH).astype(F32), ln=128)
        nw2 = jnp.tile(hgrn_norm_w[l].astype(F32), 2).reshape(1, LANES)
        o_hgrn = _hgrn_mixer(hqig, hf, lower[l].reshape(1, -1), om_lb[l].reshape(1, -1), nw2, tc=128)
        x = _outproj_moe(o_fox, o_s5, o_hgrn, x, mod[l],
                         branch_norm_fox[l].reshape(1, -1).astype(F32),
                         branch_norm_s5[l].reshape(1, -1).astype(F32),
                         w_out[l].astype(BF16),
                         norm_ffn_w[l].reshape(1, d).astype(F32), rw_pad, rb_pad,
                         moe_w_gate[l].astype(BF16), moe_w_up[l].astype(BF16), moe_w_down[l].astype(BF16),
                         fnw, tm=1024, final_norm=(l == depth - 1))
    return x
```

```python
import functools
import math

import jax
import jax.numpy as jnp
from jax import lax
from jax.experimental import pallas as pl
from jax.experimental.pallas import tpu as pltpu

F32 = jnp.float32
BF16 = jnp.bfloat16
HIGHEST = lax.Precision.HIGHEST

D_MODEL = 1024
FOX_HEADS = 6
HEAD_DIM = 64
FOX_WIDTH = 384
S5_WIDTH = 256
S5_GROUPS = 16
S5_CH = 16
S5_STATE = 64
S5_NSTATE = S5_GROUPS * S5_STATE
HGRN_WIDTH = 384
N_EXPERTS = 16
N_GROUPS = 4
D_FF = 256
EPS = 1e-6
LANES = 128
NEG_BIG = -1e30
LOG2E = math.log2(math.e)
SOFTMAX_ROWS = 64
S5_PIECE = 64
MOE_SUB = 256
MOE_CAP = 80

C_QKV = (0, 1152)
C_SU = (1152, 1408)
C_HQIG = (1408, 2560)
C_HF = (2560, 2944)
C_FF = (2944, 3072)
N_IN_PAD = 3072

NT_DIMS = (((1,), (1,)), ((), ()))

VMEM_LIMIT = 56 * 1024 * 1024


def _cparams(sem):
    return pltpu.CompilerParams(dimension_semantics=sem, vmem_limit_bytes=VMEM_LIMIT)


def _sigmoid(x):
    return 1.0 / (1.0 + jnp.exp(-x))


def _silu(x):
    hx = 0.5 * x
    return hx + hx * jnp.tanh(hx)


def _log_sigmoid(x):
    return jnp.minimum(x, 0.0) - jnp.log1p(jnp.exp(-jnp.abs(x)))


def _rms(x):
    return x * lax.rsqrt(jnp.mean(x * x, axis=-1, keepdims=True) + EPS)


def _ada_kernel(c_ref, w_ref, b_ref, o_ref):
    c = c_ref[...]
    o_ref[...] = jnp.dot(_silu(c), w_ref[...], precision=HIGHEST,
                         preferred_element_type=F32) + b_ref[...]


def _ada_mod(c, ada_w, ada_b):
    depth, d, n6 = ada_w.shape
    bsz = c.shape[0]
    nblk = n6 // d
    out = pl.pallas_call(
        _ada_kernel,
        grid=(depth, nblk),
        in_specs=[
            pl.BlockSpec((bsz, d), lambda l, j: (0, 0)),
            pl.BlockSpec((None, d, d), lambda l, j: (l, 0, j)),
            pl.BlockSpec((None, 1, d), lambda l, j: (l, 0, j)),
        ],
        out_specs=pl.BlockSpec((None, bsz, d), lambda l, j: (l, 0, j)),
        out_shape=jax.ShapeDtypeStruct((depth, bsz, n6), F32),
        compiler_params=_cparams(("arbitrary", "arbitrary")),
        name="ada_mod",
    )(c, ada_w, ada_b.reshape(depth, 1, n6))
    return out.reshape(depth, bsz, nblk, d)


def _inproj_kernel(x_ref, mod_ref, nw_ref, w_ref, fb_ref,
                   qkv_ref, su_ref, hqig_ref, hf_ref, cum_ref, carry_ref):
    i = pl.program_id(1)
    tm = x_ref.shape[0]

    @pl.when(i == 0)
    def _():
        carry_ref[...] = jnp.zeros_like(carry_ref)

    h = _rms(x_ref[...]) * nw_ref[...]
    h = h * (1.0 + mod_ref[1:2, :]) + mod_ref[0:1, :]
    hb = h.astype(BF16)

    half = N_IN_PAD // 2
    z = (jnp.dot(hb, w_ref[:, 0:half], preferred_element_type=F32),
         jnp.dot(hb, w_ref[:, half:], preferred_element_type=F32))

    def proj(c):
        lo, hi = c
        if hi <= half:
            return z[0][:, lo:hi]
        if lo >= half:
            return z[1][:, lo - half:hi - half]
        return jnp.concatenate([z[0][:, lo:], z[1][:, :hi - half]], axis=1)

    qkv_ref[...] = proj(C_QKV).astype(BF16)
    su = proj(C_SU)
    su_ref[0] = su[:, :LANES]
    su_ref[1] = su[:, LANES:]
    hqig_ref[...] = proj(C_HQIG).astype(BF16)
    hf_ref[...] = proj(C_HF)

    lf = _log_sigmoid(proj(C_FF) + fb_ref[...])
    lf_t = lf.T[0:8, :]
    r = lax.broadcasted_iota(jnp.int32, (tm, tm), 0)
    c = lax.broadcasted_iota(jnp.int32, (tm, tm), 1)
    tri_u = jnp.where(r <= c, 1.0, 0.0).astype(BF16)
    hi = lf_t.astype(BF16).astype(F32)
    r1 = lf_t - hi
    mid = r1.astype(BF16).astype(F32)
    pieces = jnp.concatenate([hi, mid, r1 - mid], axis=0).astype(BF16)
    parts = jnp.dot(pieces, tri_u, preferred_element_type=F32)
    cum = parts[0:8] + parts[8:16] + parts[16:24] + carry_ref[:, 0:1]
    cum_ref[...] = cum * LOG2E
    carry_ref[...] = jnp.broadcast_to(cum[:, tm - 1:tm], carry_ref.shape)


def _in_proj(x, mod_l, norm_w, w_pad, fb_pad, tm):
    bsz, seq, d = x.shape
    tm = min(tm, seq)
    return pl.pallas_call(
        _inproj_kernel,
        grid=(bsz, seq // tm),
        in_specs=[
            pl.BlockSpec((None, tm, d), lambda b, i: (b, i, 0)),
            pl.BlockSpec((None, 6, d), lambda b, i: (b, 0, 0)),
            pl.BlockSpec((1, d), lambda b, i: (0, 0)),
            pl.BlockSpec((d, N_IN_PAD), lambda b, i: (0, 0)),
            pl.BlockSpec((1, LANES), lambda b, i: (0, 0)),
        ],
        out_specs=[
            pl.BlockSpec((None, tm, 1152), lambda b, i: (b, i, 0)),
            pl.BlockSpec((2, None, tm, LANES), lambda b, i: (0, b, i, 0)),
            pl.BlockSpec((None, tm, 1152), lambda b, i: (b, i, 0)),
            pl.BlockSpec((None, tm, HGRN_WIDTH), lambda b, i: (b, i, 0)),
            pl.BlockSpec((None, 8, tm), lambda b, i: (b, 0, i)),
        ],
        out_shape=[
            jax.ShapeDtypeStruct((bsz, seq, 1152), BF16),
            jax.ShapeDtypeStruct((2, bsz, seq, LANES), F32),
            jax.ShapeDtypeStruct((bsz, seq, 1152), BF16),
            jax.ShapeDtypeStruct((bsz, seq, HGRN_WIDTH), F32),
            jax.ShapeDtypeStruct((bsz, 8, seq), F32),
        ],
        scratch_shapes=[pltpu.VMEM((8, LANES), F32)],
        compiler_params=_cparams(("arbitrary", "arbitrary")),
        name="in_proj",
    )(x, mod_l, norm_w, w_pad, fb_pad)


def _fox_kernel(q_ref, k_ref, v_ref, cum_ref, o_ref, s_sc, p_sc, al_sc, m_sc, l_sc, acc_sc, d_sc,
                *, tq, tk, n_steps, unroll):
    p = pl.program_id(1)
    seq = q_ref.shape[0]
    nq = seq // tq
    lane = lax.broadcasted_iota(jnp.int32, (tq, LANES), 1)
    first = lane < HEAD_DIM

    def advance(i, j):
        last = j == (i * tq) // tk
        return jnp.minimum(i + last.astype(jnp.int32), nq - 1), jnp.where(last, 0, j + 1)

    def stage_qk(i, j):
        q = q_ref[pl.ds(pl.multiple_of(i * tq, tq), tq), :]
        zero = jnp.zeros_like(q)
        q2 = jnp.concatenate([jnp.where(first, q, zero), jnp.where(first, zero, q)], axis=0)
        k0 = pl.multiple_of(j * tk, tk)
        s = lax.dot_general(q2, k_ref[pl.ds(k0, tk), :], NT_DIMS, preferred_element_type=F32)
        bias_a = cum_ref[pl.ds(2 * p, 1), pl.ds(k0, tk)]
        bias_b = cum_ref[pl.ds(2 * p + 1, 1), pl.ds(k0, tk)]
        is_last = j == (i * tq) // tk
        mask = d_sc[jnp.where(is_last, 1 + i - j * (tk // tq), 0)]
        s_sc[...] = jnp.concatenate([s[:tq] + (mask - bias_a), s[tq:] + (mask - bias_b)], axis=0)

    def stage_softmax(j):
        cap = jnp.where(j == 0, NEG_BIG, -NEG_BIG)
        for r0 in range(0, 2 * tq, SOFTMAX_ROWS):
            rs = slice(r0, r0 + SOFTMAX_ROWS)
            m_prev = jnp.minimum(m_sc[rs, :], cap)
            s = s_sc[rs, :]
            m_next = jnp.maximum(m_prev, jnp.max(s, axis=1, keepdims=True))
            pr = jnp.exp2(s - jnp.tile(m_next, (1, tk // LANES)))
            alpha = jnp.exp2(m_prev - m_next)
            l_sc[rs, :] = alpha * l_sc[rs, :] + jnp.sum(pr, axis=1, keepdims=True)
            m_sc[rs, :] = m_next
            p_sc[rs, :] = pr.astype(BF16)
            al_sc[rs, :] = alpha

    def stage_pv(i, j):
        k0 = pl.multiple_of(j * tk, tk)
        acc = acc_sc[...] * al_sc[...] + jnp.dot(p_sc[...], v_ref[pl.ds(k0, tk), :],
                                                  preferred_element_type=F32)
        acc_sc[...] = acc
        o = acc / l_sc[...]
        o_ref[pl.ds(pl.multiple_of(i * tq, tq), tq), :] = jnp.where(first, o[:tq], o[tq:]).astype(BF16)

    col_minus_row = (lax.broadcasted_iota(jnp.int32, (tq, tk), 1)
                     - lax.broadcasted_iota(jnp.int32, (tq, tk), 0))
    d_sc[0] = jnp.zeros((tq, tk), F32)
    for k in range(tk // tq):
        d_sc[1 + k] = jnp.where(col_minus_row <= k * tq, 0.0, NEG_BIG)
    s_sc[...] = jnp.zeros(s_sc.shape, F32)
    p_sc[...] = jnp.zeros(p_sc.shape, BF16)
    al_sc[...] = jnp.ones(al_sc.shape, F32)
    m_sc[...] = jnp.full(m_sc.shape, NEG_BIG, F32)
    l_sc[...] = jnp.ones(l_sc.shape, F32)
    acc_sc[...] = jnp.zeros(acc_sc.shape, F32)

    def body(t, carry):
        ia, ja, ib, jb, ic, jc = carry
        stage_pv(ic, jc)
        stage_softmax(jb)
        stage_qk(ia, ja)
        na, nja = advance(ia, ja)
        return na, nja, ia, ja, ib, jb

    zero = jnp.int32(0)
    lax.fori_loop(0, n_steps + 2, body, (zero, zero, zero, zero, zero, zero), unroll=unroll)


def _fox_attention(qkv, cum_t, tq, tk):
    bsz, seq, _ = qkv.shape
    tq = min(tq, seq)
    tk = min(max(tk, tq), seq)
    assert tk % tq == 0 and seq % tk == 0
    npair = FOX_HEADS // 2
    n_steps = sum((i * tq) // tk + 1 for i in range(seq // tq))
    kern = functools.partial(_fox_kernel, tq=tq, tk=tk, n_steps=n_steps,
                             unroll=2 if n_steps % 2 == 0 else 1)
    col = lambda off: pl.BlockSpec((None, seq, LANES), lambda b, p: (b, 0, off + p))
    return pl.pallas_call(
        kern,
        grid=(bsz, npair),
        in_specs=[col(0), col(npair), col(2 * npair),
                  pl.BlockSpec((None, 8, seq), lambda b, p: (b, 0, 0))],
        out_specs=col(0),
        out_shape=jax.ShapeDtypeStruct((bsz, seq, FOX_WIDTH), BF16),
        scratch_shapes=[
            pltpu.VMEM((2 * tq, tk), F32),
            pltpu.VMEM((2 * tq, tk), BF16),
            pltpu.VMEM((2 * tq, LANES), F32),
            pltpu.VMEM((2 * tq, LANES), F32),
            pltpu.VMEM((2 * tq, LANES), F32),
            pltpu.VMEM((2 * tq, LANES), F32),
            pltpu.VMEM((1 + tk // tq, tq, tk), F32),
        ],
        compiler_params=_cparams(("arbitrary", "arbitrary")),
        name="fox_attn",
    )(qkv, qkv, qkv, cum_t)


def _gelu_tanh(x):
    return 0.5 * x * (1.0 + jnp.tanh(math.sqrt(2.0 / math.pi) * (x + 0.044715 * (x * x * x))))


def _s5_kernel(su_ref, bm_ref, ar_ref, ai_ref, cm_ref, d_ref, gw_ref, gb_ref, o_ref,
               utb, bu, ytb, st):
    i = pl.program_id(0)
    nb = su_ref.shape[1]
    ln = su_ref.shape[2]
    ns = S5_NSTATE

    @pl.when(i == 0)
    def _():
        st[...] = jnp.zeros_like(st)

    for k in range(2):
        for b in range(nb):
            utb[k, pl.ds(b, ln, stride=nb), :] = su_ref[k, b]
    u = jnp.concatenate([utb[0], utb[1]], axis=1)
    ub = u.astype(BF16)
    ar = ar_ref[...]
    ai = ai_ref[...]
    re, im = st[:, 0:ns], st[:, ns:2 * ns]
    npiece = max(ln // S5_PIECE, 1)
    prow = (ln // npiece) * nb
    for c in range(npiece):
        rows_c = slice(c * prow, (c + 1) * prow)
        bu[rows_c, :] = jnp.dot(ub[rows_c, :], bm_ref[...], preferred_element_type=F32)
    ys = []
    for c in range(npiece):
        for t in range(c * prow // nb, (c + 1) * prow // nb):
            r0 = t * nb
            nre = ar * re - ai * im + bu[r0:r0 + nb, 0:ns]
            nim = ar * im + ai * re + bu[r0:r0 + nb, ns:2 * ns]
            bu[r0:r0 + nb, 0:ns] = nre
            bu[r0:r0 + nb, ns:2 * ns] = nim
            re, im = nre, nim
        rows_c = slice(c * prow, (c + 1) * prow)
        ys.append(jnp.dot(bu[rows_c, :].astype(BF16), cm_ref[...], preferred_element_type=F32))
    st[:, 0:ns] = re
    st[:, ns:2 * ns] = im

    y = jnp.concatenate(ys, axis=0) + d_ref[...] * u
    y = _gelu_tanh(y)
    gate = _sigmoid(jnp.dot(y.astype(BF16), gw_ref[...], preferred_element_type=F32) + gb_ref[...])
    out = y * gate
    ytb[0] = out[:, :LANES]
    ytb[1] = out[:, LANES:]
    for k in range(2):
        for b in range(nb):
            o_ref[b, :, k * LANES:(k + 1) * LANES] = ytb[k, pl.ds(b, ln, stride=nb), :].astype(BF16)


def _s5_mixer(su, bmat, ar, ai, cmat, dvec, glu_w, glu_b, ln):
    _, bsz, seq, _ = su.shape
    ln = min(ln, seq)
    rows = bsz * ln
    const = lambda shape: pl.BlockSpec(shape, lambda i: (0,) * len(shape))
    return pl.pallas_call(
        _s5_kernel,
        grid=(seq // ln,),
        in_specs=[
            pl.BlockSpec((2, bsz, ln, LANES), lambda i: (0, 0, i, 0)),
            const((S5_WIDTH, 2 * S5_NSTATE)),
            const((bsz, S5_NSTATE)),
            const((bsz, S5_NSTATE)),
            const((2 * S5_NSTATE, S5_WIDTH)),
            const((1, S5_WIDTH)),
            const((S5_WIDTH, S5_WIDTH)),
            const((1, S5_WIDTH)),
        ],
        out_specs=pl.BlockSpec((bsz, ln, S5_WIDTH), lambda i: (0, i, 0)),
        out_shape=jax.ShapeDtypeStruct((bsz, seq, S5_WIDTH), BF16),
        scratch_shapes=[
            pltpu.VMEM((2, rows, LANES), F32),
            pltpu.VMEM((rows, 2 * S5_NSTATE), F32),
            pltpu.VMEM((2, rows, LANES), F32),
            pltpu.VMEM((bsz, 2 * S5_NSTATE), F32),
        ],
        compiler_params=_cparams(("arbitrary",)),
        name="s5_mixer",
    )(su, bmat, ar, ai, cmat, dvec, glu_w, glu_b)


def _block_row(a, blk, r):
    t = a.shape[0]
    if blk >= 8:
        a3 = a.reshape(t // blk, blk, LANES)
        return jnp.broadcast_to(a3[:, r:r + 1, :], a3.shape).reshape(t, LANES)
    if blk == 4:
        a3 = a.reshape(t // 8, 8, LANES)
        sub = lax.broadcasted_iota(jnp.int32, a3.shape, 1)
        res = jnp.where(sub < 4, jnp.broadcast_to(a3[:, r:r + 1, :], a3.shape),
                        jnp.broadcast_to(a3[:, 4 + r:5 + r, :], a3.shape))
        return res.reshape(t, LANES)
    row = lax.broadcasted_iota(jnp.int32, a.shape, 0)
    res = a
    for q in range(blk):
        if q != r:
            res = jnp.where(row % blk == q, pltpu.roll(a, (q - r) % t, axis=0), res)
    return res


def _hgrn_kernel(hq_ref, hi_ref, hg_ref, hf_ref, lb_ref, omlb_ref, nw_ref, o_ref, lvl_sc, mask_sc, lo_sc,
                 *, tc):
    seq = hf_ref.shape[0]
    nlev = tc.bit_length() - 1
    lane = lax.broadcasted_iota(jnp.int32, (tc, LANES), 1)
    first = lane < HEAD_DIM
    row = lax.broadcasted_iota(jnp.int32, (tc, LANES), 0)
    rr = lax.broadcasted_iota(jnp.int32, (tc, tc), 0)
    cc = lax.broadcasted_iota(jnp.int32, (tc, tc), 1)
    diff = rr ^ cc
    bits = jnp.zeros((tc, tc), jnp.int32)
    for i in range(nlev):
        bits = bits + jnp.where((diff >> i) != 0, 1, 0)
    lvl_sc[...] = jnp.where(rr >= cc, bits, -1)
    for i in range(nlev):
        upper = ((row >> i) & 1) == 1
        mask_sc[i, 0] = jnp.where(jnp.logical_and(upper, first), 1.0, 0.0).astype(BF16)
        mask_sc[i, 1] = jnp.where(jnp.logical_and(upper, jnp.logical_not(first)), 1.0, 0.0).astype(BF16)
        mask_sc[i, 2] = jnp.where(upper, 0.0, 1.0).astype(BF16)
        lo_sc[i] = jnp.where(upper, 0.0, 1.0)
    sr = lax.broadcasted_iota(jnp.int32, (LANES, LANES), 0)
    sc = lax.broadcasted_iota(jnp.int32, (LANES, LANES), 1)
    blockdiag = (sr < HEAD_DIM) == (sc < HEAD_DIM)
    lb = lb_ref[...]
    omlb = omlb_ref[...]
    nw = nw_ref[...]

    def chunk(c, st):
        r0 = pl.multiple_of(c * tc, tc)
        z = hf_ref[pl.ds(r0, tc), :]
        q = _silu(hq_ref[pl.ds(r0, tc), :].astype(F32))
        v = hi_ref[pl.ds(r0, tc), :]
        g = hg_ref[pl.ds(r0, tc), :].astype(F32)
        e = jnp.exp(-jnp.abs(z))
        s_big = 1.0 / (1.0 + e)
        s_small = e * s_big
        pos = z >= 0.0
        f = lb + omlb * jnp.where(pos, s_big, s_small)
        kk = omlb * jnp.where(pos, s_small, s_big)

        def scores(q_a, q_b, km):
            return lax.dot_general(jnp.concatenate([q_a, q_b], axis=0), km, NT_DIMS,
                                   preferred_element_type=F32)

        s = scores(jnp.where(first, q, 0.0).astype(BF16), jnp.where(first, 0.0, q).astype(BF16),
                   kk.astype(BF16))
        on_diag = lvl_sc[...] == 0
        tot_a = jnp.where(on_diag, s[:tc], 0.0)
        tot_b = jnp.where(on_diag, s[tc:], 0.0)
        a_m = f
        b_m = jnp.ones_like(f)
        for i in range(nlev):
            m = 1 << i
            qa = (q * a_m).astype(BF16)
            s = scores(qa * mask_sc[i, 0], qa * mask_sc[i, 1], (kk * b_m).astype(BF16) * mask_sc[i, 2])
            sel = lvl_sc[...] == i + 1
            tot_a = jnp.where(sel, s[:tc], tot_a)
            tot_b = jnp.where(sel, s[tc:], tot_b)
            lower_total = _block_row(a_m, 2 * m, m - 1)
            upper_total = _block_row(a_m, 2 * m, 2 * m - 1)
            upper = lo_sc[i] < 0.5
            a_m = jnp.where(upper, a_m * lower_total, a_m)
            b_m = jnp.where(upper, b_m, b_m * upper_total)

        o = lax.dot_general((q * a_m).astype(BF16), st.astype(BF16), NT_DIMS, preferred_element_type=F32)
        pv = jnp.dot(jnp.concatenate([tot_a, tot_b], axis=0).astype(BF16), v, preferred_element_type=F32)
        o = o + jnp.where(first, pv[:tc], pv[tc:])

        upd = jnp.dot(v.astype(F32).T.astype(BF16), (kk * b_m).astype(BF16), preferred_element_type=F32)
        st_new = st * a_m[tc - 1:tc, :] + jnp.where(blockdiag, upd, 0.0)

        o2 = o * o
        s_a = jnp.sum(jnp.where(first, o2, 0.0), axis=-1, keepdims=True)
        s_b = jnp.sum(jnp.where(first, 0.0, o2), axis=-1, keepdims=True)
        ms = jnp.where(first, s_a, s_b) * (1.0 / HEAD_DIM)
        o_ref[pl.ds(r0, tc), :] = (o * lax.rsqrt(ms + EPS) * nw * _silu(g)).astype(BF16)
        return st_new

    lax.fori_loop(0, seq // tc, chunk, jnp.zeros((LANES, LANES), F32), unroll=8)


def _hgrn_mixer(hqig, hf, lb, om_lb, nw2, tc):
    bsz, seq, _ = hf.shape
    tc = min(tc, seq)
    npair = HGRN_WIDTH // LANES
    kern = functools.partial(_hgrn_kernel, tc=tc)
    col = lambda off: pl.BlockSpec((None, seq, LANES), lambda b, p: (b, 0, off + p))
    par = pl.BlockSpec((1, LANES), lambda b, p: (0, p))
    return pl.pallas_call(
        kern,
        grid=(bsz, npair),
        in_specs=[col(0), col(npair), col(2 * npair), col(0), par, par,
                  pl.BlockSpec((1, LANES), lambda b, p: (0, 0))],
        out_specs=col(0),
        out_shape=jax.ShapeDtypeStruct((bsz, seq, HGRN_WIDTH), BF16),
        scratch_shapes=[
            pltpu.VMEM((tc, tc), jnp.int32),
            pltpu.VMEM((tc.bit_length() - 1, 3, tc, LANES), BF16),
            pltpu.VMEM((tc.bit_length() - 1, tc, LANES), F32),
        ],
        compiler_params=_cparams(("arbitrary", "arbitrary")),
        name="hgrn_mixer",
    )(hqig, hqig, hqig, hf, lb, om_lb, nw2)


def _routing(logits_t):
    mx = jnp.max(logits_t, axis=0, keepdims=True)
    ex = jnp.exp(logits_t - mx)
    probs = ex / jnp.sum(ex, axis=0, keepdims=True)
    p = [probs[e:e + 1, :] for e in range(N_EXPERTS)]
    epg = N_EXPERTS // N_GROUPS
    scores = []
    for g in range(N_GROUPS):
        a, b, c, d = p[epg * g:epg * g + epg]
        hi1, lo1 = jnp.maximum(a, b), jnp.minimum(a, b)
        hi2, lo2 = jnp.maximum(c, d), jnp.minimum(c, d)
        top1 = jnp.maximum(hi1, hi2)
        top2 = jnp.maximum(jnp.minimum(hi1, hi2), jnp.maximum(lo1, lo2))
        scores.append(top1 + top2)
    best = jnp.zeros_like(scores[0], dtype=jnp.int32)
    bs = scores[0]
    for g in range(1, N_GROUPS):
        upd = scores[g] > bs
        best = jnp.where(upd, g, best)
        bs = jnp.where(upd, scores[g], bs)
    ig = []
    for j in range(epg):
        val = p[j]
        for g in range(1, N_GROUPS):
            val = jnp.where(best == g, p[epg * g + j], val)
        ig.append(val)
    i1 = jnp.zeros_like(best)
    w1 = ig[0]
    for j in range(1, epg):
        upd = ig[j] > w1
        i1 = jnp.where(upd, j, i1)
        w1 = jnp.where(upd, ig[j], w1)
    i2 = jnp.zeros_like(best)
    w2 = jnp.full_like(w1, -1.0)
    for j in range(epg):
        upd = jnp.logical_and(i1 != j, ig[j] > w2)
        i2 = jnp.where(upd, j, i2)
        w2 = jnp.where(upd, ig[j], w2)
    den = w1 + w2
    tw1 = w1 / den
    tw2 = w2 / den
    e1 = best * epg + i1
    e2 = best * epg + i2
    return [jnp.where(e1 == e, tw1, 0.0) + jnp.where(e2 == e, tw2, 0.0) for e in range(N_EXPERTS)], best


def _moe_kernel(fox_ref, s5_ref, hg_ref, x_ref, mod_ref, nf_ref, ns_ref, wo_ref,
                nw_ref, rw_ref, rb_ref, wg_ref, wu_ref, wd_ref, fnw_ref,
                o_ref, h_sc, comb_sc, combt_sc, hs_sc, ws_sc, ys_sc, acc_sc, cnt_sm, *, final_norm):
    grp = pl.program_id(2)
    tm, d = x_ref.shape
    epg = N_EXPERTS // N_GROUPS
    sub, cap = MOE_SUB, MOE_CAP
    ns = tm // sub
    gc = N_GROUPS * cap
    row_oh, row_slot = N_EXPERTS, N_EXPERTS + 8

    @pl.when(grp == 0)
    def _():
        for s in range(ns):
            toks = slice(s * sub, (s + 1) * sub)
            of = (_rms(fox_ref[toks, :].astype(F32)) * nf_ref[...]).astype(BF16)
            os5 = (_rms(s5_ref[toks, :].astype(F32)) * ns_ref[...]).astype(BF16)
            mixed = jnp.concatenate([of, os5, hg_ref[toks, :]], axis=1)
            mix = jnp.dot(mixed, wo_ref[...], preferred_element_type=F32)
            o_ref[toks, :] = x_ref[toks, :] + mod_ref[2:3, :] * mix
        h = _rms(o_ref[...]) * nw_ref[...]
        h = h * (1.0 + mod_ref[4:5, :]) + mod_ref[3:4, :]
        hb = h.astype(BF16)
        h_sc[...] = hb
        h_lo = (h - hb.astype(F32)).astype(BF16)
        hi_part = jnp.dot(hb, rw_ref[...], preferred_element_type=F32)
        lo_part = jnp.dot(h_lo, rw_ref[:, 0:LANES], preferred_element_type=F32)
        logits = hi_part[:, 0:LANES] + hi_part[:, LANES:] + lo_part + rb_ref[...]
        rows, best = _routing(logits.T[0:N_EXPERTS, :])
        combt_sc[...] = jnp.zeros_like(combt_sc)
        for j in range(N_EXPERTS):
            combt_sc[j:j + 1, :] = rows[j]
        for g in range(N_GROUPS):
            combt_sc[row_oh + g:row_oh + g + 1, :] = jnp.where(best == g, 1.0, 0.0)
        oh8 = combt_sc[row_oh:row_oh + 8, :]
        r = lax.broadcasted_iota(jnp.int32, (sub, sub), 0)
        c = lax.broadcasted_iota(jnp.int32, (sub, sub), 1)
        tri_u = jnp.where(r <= c, 1.0, 0.0).astype(BF16)
        incl = [jnp.dot(oh8[:, s * sub:(s + 1) * sub].astype(BF16), tri_u, preferred_element_type=F32)
                for s in range(ns)]
        cmax = incl[0][:, sub - 1:sub]
        for s in range(1, ns):
            cmax = jnp.maximum(cmax, incl[s][:, sub - 1:sub])
        for g in range(N_GROUPS):
            cnt_sm[g] = jnp.max(cmax[g:g + 1, :]).astype(jnp.int32)
        over = jnp.where(cmax > cap, 1.0, 0.0)
        best_f = best.astype(F32)
        for s in range(ns):
            cols = slice(s * sub, (s + 1) * sub)
            oh_s = oh8[:, cols]
            rank = jnp.sum(oh_s * incl[s], axis=0, keepdims=True) - 1.0
            dense = jnp.sum(oh_s * over, axis=0, keepdims=True)
            combt_sc[row_slot:row_slot + 1, cols] = jnp.where(dense > 0.5, -1.0, best_f[:, cols] * cap + rank)
        comb_sc[...] = combt_sc[...].T
        for s in range(ns):
            toks = slice(s * sub, (s + 1) * sub)
            slot_row = combt_sc[row_slot:row_slot + 1, toks].astype(jnp.int32)
            ri = lax.broadcasted_iota(jnp.int32, (gc, sub), 0)
            p = jnp.where(ri == slot_row, 1.0, 0.0).astype(BF16)
            hs_all = jnp.dot(p, h_sc[toks, :], preferred_element_type=F32).astype(BF16)
            cs = comb_sc[toks, :]
            c_hi = cs.astype(BF16)
            c_lo = (cs - c_hi.astype(F32)).astype(BF16)
            ws_all = jnp.dot(p, jnp.concatenate([c_hi, c_lo], axis=1), preferred_element_type=F32)
            ws_all = ws_all[:, 0:LANES] + ws_all[:, LANES:]
            for g in range(N_GROUPS):
                hs_sc[g, s * cap:(s + 1) * cap, :] = hs_all[g * cap:(g + 1) * cap, :]
                ws_sc[g, s * cap:(s + 1) * cap, :] = ws_all[g * cap:(g + 1) * cap, :]
        acc_sc[...] = jnp.zeros_like(acc_sc)

    def experts(h_rows, w_rows):
        lane = lax.broadcasted_iota(jnp.int32, w_rows.shape, 1)
        hid = []
        for j in range(epg):
            gate = jnp.dot(h_rows, wg_ref[j], preferred_element_type=F32)
            up = jnp.dot(h_rows, wu_ref[j], preferred_element_type=F32)
            w_e = jnp.sum(jnp.where(lane == grp * epg + j, w_rows, 0.0), axis=-1, keepdims=True)
            hid.append((_silu(gate) * up * w_e).astype(BF16))
        wd = wd_ref[...].reshape(epg * D_FF, d)
        return jnp.dot(jnp.concatenate(hid, axis=1), wd, preferred_element_type=F32)

    cnt = cnt_sm[grp]

    @pl.when(cnt <= cap)
    def _():
        ys = experts(hs_sc[grp], ws_sc[grp]).astype(BF16)
        for s in range(ns):
            ys_sc[s, grp] = ys[s * cap:(s + 1) * cap, :]

    @pl.when(cnt > cap)
    def _():
        acc_sc[...] += experts(h_sc[...], comb_sc[...])
        for s in range(ns):
            ys_sc[s, grp] = jnp.zeros((cap, d), BF16)

    @pl.when(grp == N_GROUPS - 1)
    def _():
        for s in range(ns):
            toks = slice(s * sub, (s + 1) * sub)
            slot_col = comb_sc[toks, row_slot:row_slot + 1].astype(jnp.int32)
            li = lax.broadcasted_iota(jnp.int32, (sub, gc), 1)
            pt = jnp.where(li == slot_col, 1.0, 0.0).astype(BF16)
            y = jnp.dot(pt, ys_sc[s].reshape(gc, d), preferred_element_type=F32) + acc_sc[toks, :]
            y = o_ref[toks, :] + mod_ref[5:6, :] * y
            if final_norm:
                y = _rms(y) * fnw_ref[...]
            o_ref[toks, :] = y


def _outproj_moe(o_fox, o_s5, o_hgrn, x, mod_l, nf, ns_w, w_out, norm_w, rw_pad, rb_pad, wg, wu, wd, fnw,
                 tm, final_norm):
    bsz, seq, d = x.shape
    tm = min(tm, seq)
    assert tm % MOE_SUB == 0
    ns = tm // MOE_SUB
    epg = N_EXPERTS // N_GROUPS
    kern = functools.partial(_moe_kernel, final_norm=final_norm)
    row = lambda w: pl.BlockSpec((None, tm, w), lambda b, i, e: (b, i, 0))
    return pl.pallas_call(
        kern,
        grid=(bsz, seq // tm, N_GROUPS),
        in_specs=[
            row(FOX_WIDTH), row(S5_WIDTH), row(HGRN_WIDTH), row(d),
            pl.BlockSpec((None, 6, d), lambda b, i, e: (b, 0, 0)),
            pl.BlockSpec((1, FOX_WIDTH), lambda b, i, e: (0, 0)),
            pl.BlockSpec((1, S5_WIDTH), lambda b, i, e: (0, 0)),
            pl.BlockSpec((d, d), lambda b, i, e: (0, 0)),
            pl.BlockSpec((1, d), lambda b, i, e: (0, 0)),
            pl.BlockSpec((d, 2 * LANES), lambda b, i, e: (0, 0)),
            pl.BlockSpec((1, LANES), lambda b, i, e: (0, 0)),
            pl.BlockSpec((epg, d, D_FF), lambda b, i, e: (e, 0, 0)),
            pl.BlockSpec((epg, d, D_FF), lambda b, i, e: (e, 0, 0)),
            pl.BlockSpec((epg, D_FF, d), lambda b, i, e: (e, 0, 0)),
            pl.BlockSpec((1, d), lambda b, i, e: (0, 0)),
        ],
        out_specs=pl.BlockSpec((None, tm, d), lambda b, i, e: (b, i, 0)),
        out_shape=jax.ShapeDtypeStruct((bsz, seq, d), F32),
        scratch_shapes=[
            pltpu.VMEM((tm, d), BF16),
            pltpu.VMEM((tm, LANES), F32),
            pltpu.VMEM((LANES, tm), F32),
            pltpu.VMEM((N_GROUPS, ns * MOE_CAP, d), BF16),
            pltpu.VMEM((N_GROUPS, ns * MOE_CAP, LANES), F32),
            pltpu.VMEM((ns, N_GROUPS, MOE_CAP, d), BF16),
            pltpu.VMEM((tm, d), F32),
            pltpu.SMEM((N_GROUPS,), jnp.int32),
        ],
        compiler_params=_cparams(("arbitrary", "arbitrary", "arbitrary")),
        name="moe",
    )(o_fox, o_s5, o_hgrn, x, mod_l, nf, ns_w, w_out, norm_w, rw_pad, rb_pad, wg, wu, wd, fnw)


def _prep_w_in(w_in_l, f_bias_l):
    a = FOX_WIDTH
    o_ff = 3 * a
    o_su = o_ff + FOX_HEADS
    o_hq = o_su + S5_WIDTH
    hw = HGRN_WIDTH
    fq = w_in_l[:, 0:a] * (HEAD_DIM ** -0.5 * LOG2E)
    fk, fv = w_in_l[:, a:2 * a], w_in_l[:, 2 * a:3 * a]
    ff = jnp.pad(w_in_l[:, o_ff:o_su], ((0, 0), (0, LANES - FOX_HEADS)))
    su = w_in_l[:, o_su:o_hq]
    hq = w_in_l[:, o_hq:o_hq + hw]
    hf = w_in_l[:, o_hq + hw:o_hq + 2 * hw]
    hi = w_in_l[:, o_hq + 2 * hw:o_hq + 3 * hw]
    hg = w_in_l[:, o_hq + 3 * hw:o_hq + 4 * hw]
    w = jnp.concatenate([fq, fk, fv, su, hq, hi, hg, hf, ff], axis=1).astype(BF16)
    fb = jnp.pad(f_bias_l.astype(F32), (0, LANES - FOX_HEADS)).reshape(1, LANES)
    return w, fb


def _prep_s5(a_re, a_im, b_re, b_im, c_re, c_im, log_dt, bsz):
    lam = lax.complex(a_re.astype(F32), a_im.astype(F32))
    dt = jnp.exp(log_dt.astype(F32))[:, None]
    a_bar = jnp.exp(lam * dt)
    b_bar = ((a_bar - 1.0) / lam)[..., None] * lax.complex(b_re.astype(F32), b_im.astype(F32))
    eye = jnp.eye(S5_GROUPS, dtype=F32)
    bm_re = jnp.einsum('gpc,gh->gchp', jnp.real(b_bar), eye).reshape(S5_WIDTH, S5_NSTATE)
    bm_im = jnp.einsum('gpc,gh->gchp', jnp.imag(b_bar), eye).reshape(S5_WIDTH, S5_NSTATE)
    bmat = jnp.concatenate([bm_re, bm_im], axis=1).astype(BF16)
    cm_re = jnp.einsum('gcp,gh->gphc', c_re.astype(F32), eye).reshape(S5_NSTATE, S5_WIDTH)
    cm_im = jnp.einsum('gcp,gh->gphc', c_im.astype(F32), eye).reshape(S5_NSTATE, S5_WIDTH)
    cmat = jnp.concatenate([cm_re, -cm_im], axis=0).astype(BF16)
    ar = jnp.broadcast_to(jnp.real(a_bar).reshape(1, S5_NSTATE), (bsz, S5_NSTATE))
    ai = jnp.broadcast_to(jnp.imag(a_bar).reshape(1, S5_NSTATE), (bsz, S5_NSTATE))
    return bmat, ar, ai, cmat


def kernel(x, c, ada_w, ada_b, norm_mix_w, norm_ffn_w, w_in, fox_f_bias, s5_a_re, s5_a_im, s5_b_re, s5_b_im, s5_c_re, s5_c_im, s5_d, s5_log_dt, s5_glu_w, s5_glu_b, hgrn_lb_logits, hgrn_norm_w, branch_norm_fox, branch_norm_s5, w_out, router_w, router_b, moe_w_gate, moe_w_up, moe_w_down, final_norm_w):
    bsz, seq, d = x.shape
    depth = w_in.shape[0]
    lb_cum = jnp.cumsum(jax.nn.softmax(hgrn_lb_logits.astype(F32), axis=0), axis=0)
    lower = lb_cum - lb_cum[0:1]
    om_lb = 1.0 - lower

    mod = _ada_mod(c, ada_w, ada_b)
    rw32 = jnp.pad(router_w.astype(F32), ((0, 0), (0, LANES - N_EXPERTS)))
    rw_hi = rw32.astype(BF16)
    rw_pad = jnp.concatenate([rw_hi, (rw32 - rw_hi.astype(F32)).astype(BF16)], axis=1)
    rb_pad = jnp.pad(router_b.astype(F32), (0, LANES - N_EXPERTS)).reshape(1, LANES)
    fnw = final_norm_w.reshape(1, d).astype(F32)

    w_pad, fb_pad = jax.vmap(_prep_w_in)(w_in, fox_f_bias)
    bmat, ar, ai, cmat = jax.vmap(functools.partial(_prep_s5, bsz=bsz))(
        s5_a_re, s5_a_im, s5_b_re, s5_b_im, s5_c_re, s5_c_im, s5_log_dt)
    glu_w = s5_glu_w.astype(BF16)
    wo = w_out.astype(BF16)
    wg, wu, wd = moe_w_gate.astype(BF16), moe_w_up.astype(BF16), moe_w_down.astype(BF16)

    for l in range(depth):
        qkv, su, hqig, hf, cum_t = _in_proj(x, mod[l], norm_mix_w[l].reshape(1, d), w_pad[l], fb_pad[l], tm=1024)
        o_fox = _fox_attention(qkv, cum_t, tq=256, tk=512)
        o_s5 = _s5_mixer(su, bmat[l], ar[l], ai[l], cmat[l], s5_d[l].reshape(1, S5_WIDTH).astype(F32),
                         glu_w[l], s5_glu_b[l].reshape(1, S5_WIDTH).astype(F32), ln=128)
        nw2 = jnp.tile(hgrn_norm_w[l].astype(F32), 2).reshape(1, LANES)
        o_hgrn = _hgrn_mixer(hqig, hf, lower[l].reshape(1, -1), om_lb[l].reshape(1, -1), nw2, tc=128)
        x = _outproj_moe(o_fox, o_s5, o_hgrn, x, mod[l],
                         branch_norm_fox[l].reshape(1, -1).astype(F32),
                         branch_norm_s5[l].reshape(1, -1).astype(F32),
                         wo[l],
                         norm_ffn_w[l].reshape(1, d).astype(F32), rw_pad, rb_pad,
                         wg[l], wu[l], wd[l],
                         fnw, tm=1024, final_norm=(l == depth - 1))
    return x
```

```python
import functools
import math

import jax
import jax.numpy as jnp
from jax import lax
from jax.experimental import pallas as pl
from jax.experimental.pallas import tpu as pltpu

F32 = jnp.float32
BF16 = jnp.bfloat16
HIGHEST = lax.Precision.HIGHEST

D_MODEL = 1024
FOX_HEADS = 6
HEAD_DIM = 64
FOX_WIDTH = 384
S5_WIDTH = 256
S5_GROUPS = 16
S5_CH = 16
S5_STATE = 64
S5_NSTATE = S5_GROUPS * S5_STATE
HGRN_WIDTH = 384
N_EXPERTS = 16
N_GROUPS = 4
D_FF = 256
EPS = 1e-6
LANES = 128
NEG_BIG = -1e30
LOG2E = math.log2(math.e)
SOFTMAX_ROWS = 64
S5_PIECE = 64
MOE_SUB = 256
MOE_CAP = 96

C_QKV = (0, 1152)
C_SU = (1152, 1408)
C_HQIG = (1408, 2560)
C_HF = (2560, 2944)
C_FF = (2944, 3072)
N_IN_PAD = 3072

NT_DIMS = (((1,), (1,)), ((), ()))

VMEM_LIMIT = 56 * 1024 * 1024


def _cparams(sem):
    return pltpu.CompilerParams(dimension_semantics=sem, vmem_limit_bytes=VMEM_LIMIT)


def _sigmoid(x):
    return 1.0 / (1.0 + jnp.exp(-x))


def _silu(x):
    hx = 0.5 * x
    return hx + hx * jnp.tanh(hx)


def _log_sigmoid(x):
    return jnp.minimum(x, 0.0) - jnp.log1p(jnp.exp(-jnp.abs(x)))


def _rms(x):
    return x * lax.rsqrt(jnp.mean(x * x, axis=-1, keepdims=True) + EPS)


def _ada_kernel(c_ref, w_ref, b_ref, o_ref):
    c = c_ref[...]
    o_ref[...] = jnp.dot(_silu(c), w_ref[...], precision=HIGHEST,
                         preferred_element_type=F32) + b_ref[...]


def _ada_mod(c, ada_w, ada_b):
    depth, d, n6 = ada_w.shape
    bsz = c.shape[0]
    nblk = n6 // d
    out = pl.pallas_call(
        _ada_kernel,
        grid=(depth, nblk),
        in_specs=[
            pl.BlockSpec((bsz, d), lambda l, j: (0, 0)),
            pl.BlockSpec((None, d, d), lambda l, j: (l, 0, j)),
            pl.BlockSpec((None, 1, d), lambda l, j: (l, 0, j)),
        ],
        out_specs=pl.BlockSpec((None, bsz, d), lambda l, j: (l, 0, j)),
        out_shape=jax.ShapeDtypeStruct((depth, bsz, n6), F32),
        compiler_params=_cparams(("arbitrary", "arbitrary")),
        name="ada_mod",
    )(c, ada_w, ada_b.reshape(depth, 1, n6))
    return out.reshape(depth, bsz, nblk, d)


def _inproj_kernel(x_ref, mod_ref, nw_ref, w_ref, fb_ref,
                   qkv_ref, su_ref, hqig_ref, hf_ref, cum_ref, carry_ref):
    i = pl.program_id(1)
    tm = x_ref.shape[0]

    @pl.when(i == 0)
    def _():
        carry_ref[...] = jnp.zeros_like(carry_ref)

    h = _rms(x_ref[...]) * nw_ref[...]
    h = h * (1.0 + mod_ref[1:2, :]) + mod_ref[0:1, :]
    hb = h.astype(BF16)

    half = N_IN_PAD // 2
    z = (jnp.dot(hb, w_ref[:, 0:half], preferred_element_type=F32),
         jnp.dot(hb, w_ref[:, half:], preferred_element_type=F32))

    def proj(c):
        lo, hi = c
        if hi <= half:
            return z[0][:, lo:hi]
        if lo >= half:
            return z[1][:, lo - half:hi - half]
        return jnp.concatenate([z[0][:, lo:], z[1][:, :hi - half]], axis=1)

    qkv_ref[...] = proj(C_QKV).astype(BF16)
    su = proj(C_SU)
    su_ref[0] = su[:, :LANES]
    su_ref[1] = su[:, LANES:]
    hqig_ref[...] = proj(C_HQIG).astype(BF16)
    hf_ref[...] = proj(C_HF)

    lf = _log_sigmoid(proj(C_FF) + fb_ref[...])
    lf_t = lf.T[0:8, :]
    r = lax.broadcasted_iota(jnp.int32, (tm, tm), 0)
    c = lax.broadcasted_iota(jnp.int32, (tm, tm), 1)
    tri_u = jnp.where(r <= c, 1.0, 0.0).astype(BF16)
    hi = lf_t.astype(BF16).astype(F32)
    r1 = lf_t - hi
    mid = r1.astype(BF16).astype(F32)
    pieces = jnp.concatenate([hi, mid, r1 - mid], axis=0).astype(BF16)
    parts = jnp.dot(pieces, tri_u, preferred_element_type=F32)
    cum = parts[0:8] + parts[8:16] + parts[16:24] + carry_ref[:, 0:1]
    cum_ref[...] = cum * LOG2E
    carry_ref[...] = jnp.broadcast_to(cum[:, tm - 1:tm], carry_ref.shape)


def _in_proj(x, mod_l, norm_w, w_pad, fb_pad, tm):
    bsz, seq, d = x.shape
    tm = min(tm, seq)
    return pl.pallas_call(
        _inproj_kernel,
        grid=(bsz, seq // tm),
        in_specs=[
            pl.BlockSpec((None, tm, d), lambda b, i: (b, i, 0)),
            pl.BlockSpec((None, 6, d), lambda b, i: (b, 0, 0)),
            pl.BlockSpec((1, d), lambda b, i: (0, 0)),
            pl.BlockSpec((d, N_IN_PAD), lambda b, i: (0, 0)),
            pl.BlockSpec((1, LANES), lambda b, i: (0, 0)),
        ],
        out_specs=[
            pl.BlockSpec((None, tm, 1152), lambda b, i: (b, i, 0)),
            pl.BlockSpec((2, None, tm, LANES), lambda b, i: (0, b, i, 0)),
            pl.BlockSpec((None, tm, 1152), lambda b, i: (b, i, 0)),
            pl.BlockSpec((None, tm, HGRN_WIDTH), lambda b, i: (b, i, 0)),
            pl.BlockSpec((None, 8, tm), lambda b, i: (b, 0, i)),
        ],
        out_shape=[
            jax.ShapeDtypeStruct((bsz, seq, 1152), BF16),
            jax.ShapeDtypeStruct((2, bsz, seq, LANES), F32),
            jax.ShapeDtypeStruct((bsz, seq, 1152), BF16),
            jax.ShapeDtypeStruct((bsz, seq, HGRN_WIDTH), F32),
            jax.ShapeDtypeStruct((bsz, 8, seq), F32),
        ],
        scratch_shapes=[pltpu.VMEM((8, LANES), F32)],
        compiler_params=_cparams(("arbitrary", "arbitrary")),
        name="in_proj",
    )(x, mod_l, norm_w, w_pad, fb_pad)


def _fox_kernel(q_ref, k_ref, v_ref, cum_ref, o_ref, s_sc, p_sc, al_sc, m_sc, l_sc, acc_sc, d_sc,
                *, tq, tk, n_steps, unroll):
    p = pl.program_id(1)
    seq = q_ref.shape[0]
    nq = seq // tq
    lane = lax.broadcasted_iota(jnp.int32, (tq, LANES), 1)
    first = lane < HEAD_DIM

    def advance(i, j):
        last = j == (i * tq) // tk
        return jnp.minimum(i + last.astype(jnp.int32), nq - 1), jnp.where(last, 0, j + 1)

    def stage_qk(i, j):
        q = q_ref[pl.ds(pl.multiple_of(i * tq, tq), tq), :]
        zero = jnp.zeros_like(q)
        q2 = jnp.concatenate([jnp.where(first, q, zero), jnp.where(first, zero, q)], axis=0)
        k0 = pl.multiple_of(j * tk, tk)
        s = lax.dot_general(q2, k_ref[pl.ds(k0, tk), :], NT_DIMS, preferred_element_type=F32)
        bias_a = cum_ref[pl.ds(2 * p, 1), pl.ds(k0, tk)]
        bias_b = cum_ref[pl.ds(2 * p + 1, 1), pl.ds(k0, tk)]
        is_last = j == (i * tq) // tk
        mask = d_sc[jnp.where(is_last, 1 + i - j * (tk // tq), 0)]
        s_sc[...] = jnp.concatenate([s[:tq] + (mask - bias_a), s[tq:] + (mask - bias_b)], axis=0)

    def stage_softmax(j):
        cap = jnp.where(j == 0, NEG_BIG, -NEG_BIG)
        for r0 in range(0, 2 * tq, SOFTMAX_ROWS):
            rs = slice(r0, r0 + SOFTMAX_ROWS)
            m_prev = jnp.minimum(m_sc[rs, :], cap)
            s = s_sc[rs, :]
            m_next = jnp.maximum(m_prev, jnp.max(s, axis=1, keepdims=True))
            pr = jnp.exp2(s - jnp.tile(m_next, (1, tk // LANES)))
            alpha = jnp.exp2(m_prev - m_next)
            l_sc[rs, :] = alpha * l_sc[rs, :] + jnp.sum(pr, axis=1, keepdims=True)
            m_sc[rs, :] = m_next
            p_sc[rs, :] = pr.astype(BF16)
            al_sc[rs, :] = alpha

    def stage_pv(i, j):
        k0 = pl.multiple_of(j * tk, tk)
        acc = acc_sc[...] * al_sc[...] + jnp.dot(p_sc[...], v_ref[pl.ds(k0, tk), :],
                                                  preferred_element_type=F32)
        acc_sc[...] = acc
        o = acc / l_sc[...]
        o_ref[pl.ds(pl.multiple_of(i * tq, tq), tq), :] = jnp.where(first, o[:tq], o[tq:]).astype(BF16)

    col_minus_row = (lax.broadcasted_iota(jnp.int32, (tq, tk), 1)
                     - lax.broadcasted_iota(jnp.int32, (tq, tk), 0))
    d_sc[0] = jnp.zeros((tq, tk), F32)
    for k in range(tk // tq):
        d_sc[1 + k] = jnp.where(col_minus_row <= k * tq, 0.0, NEG_BIG)
    s_sc[...] = jnp.zeros(s_sc.shape, F32)
    p_sc[...] = jnp.zeros(p_sc.shape, BF16)
    al_sc[...] = jnp.ones(al_sc.shape, F32)
    m_sc[...] = jnp.full(m_sc.shape, NEG_BIG, F32)
    l_sc[...] = jnp.ones(l_sc.shape, F32)
    acc_sc[...] = jnp.zeros(acc_sc.shape, F32)

    def body(t, carry):
        ia, ja, ib, jb, ic, jc = carry
        stage_pv(ic, jc)
        stage_softmax(jb)
        stage_qk(ia, ja)
        na, nja = advance(ia, ja)
        return na, nja, ia, ja, ib, jb

    zero = jnp.int32(0)
    lax.fori_loop(0, n_steps + 2, body, (zero, zero, zero, zero, zero, zero), unroll=unroll)


def _fox_attention(qkv, cum_t, tq, tk):
    bsz, seq, _ = qkv.shape
    tq = min(tq, seq)
    tk = min(max(tk, tq), seq)
    assert tk % tq == 0 and seq % tk == 0
    npair = FOX_HEADS // 2
    n_steps = sum((i * tq) // tk + 1 for i in range(seq // tq))
    kern = functools.partial(_fox_kernel, tq=tq, tk=tk, n_steps=n_steps,
                             unroll=2 if n_steps % 2 == 0 else 1)
    col = lambda off: pl.BlockSpec((None, seq, LANES), lambda b, p: (b, 0, off + p))
    return pl.pallas_call(
        kern,
        grid=(bsz, npair),
        in_specs=[col(0), col(npair), col(2 * npair),
                  pl.BlockSpec((None, 8, seq), lambda b, p: (b, 0, 0))],
        out_specs=col(0),
        out_shape=jax.ShapeDtypeStruct((bsz, seq, FOX_WIDTH), BF16),
        scratch_shapes=[
            pltpu.VMEM((2 * tq, tk), F32),
            pltpu.VMEM((2 * tq, tk), BF16),
            pltpu.VMEM((2 * tq, LANES), F32),
            pltpu.VMEM((2 * tq, LANES), F32),
            pltpu.VMEM((2 * tq, LANES), F32),
            pltpu.VMEM((2 * tq, LANES), F32),
            pltpu.VMEM((1 + tk // tq, tq, tk), F32),
        ],
        compiler_params=_cparams(("arbitrary", "arbitrary")),
        name="fox_attn",
    )(qkv, qkv, qkv, cum_t)


def _gelu_tanh(x):
    return 0.5 * x * (1.0 + jnp.tanh(math.sqrt(2.0 / math.pi) * (x + 0.044715 * (x * x * x))))


def _s5_kernel(su_ref, bm_ref, ar_ref, ai_ref, cm_ref, d_ref, gw_ref, gb_ref, o_ref,
               utb, bu, ytb, st):
    i = pl.program_id(0)
    nb = su_ref.shape[1]
    ln = su_ref.shape[2]
    ns = S5_NSTATE

    @pl.when(i == 0)
    def _():
        st[...] = jnp.zeros_like(st)

    for k in range(2):
        for b in range(nb):
            utb[k, pl.ds(b, ln, stride=nb), :] = su_ref[k, b]
    u = jnp.concatenate([utb[0], utb[1]], axis=1)
    ub = u.astype(BF16)
    ar = ar_ref[...]
    ai = ai_ref[...]
    re, im = st[:, 0:ns], st[:, ns:2 * ns]
    npiece = max(ln // S5_PIECE, 1)
    prow = (ln // npiece) * nb
    for c in range(npiece):
        rows_c = slice(c * prow, (c + 1) * prow)
        bu[rows_c, :] = jnp.dot(ub[rows_c, :], bm_ref[...], preferred_element_type=F32)
    ys = []
    for c in range(npiece):
        for t in range(c * prow // nb, (c + 1) * prow // nb):
            r0 = t * nb
            nre = ar * re - ai * im + bu[r0:r0 + nb, 0:ns]
            nim = ar * im + ai * re + bu[r0:r0 + nb, ns:2 * ns]
            bu[r0:r0 + nb, 0:ns] = nre
            bu[r0:r0 + nb, ns:2 * ns] = nim
            re, im = nre, nim
        rows_c = slice(c * prow, (c + 1) * prow)
        ys.append(jnp.dot(bu[rows_c, :].astype(BF16), cm_ref[...], preferred_element_type=F32))
    st[:, 0:ns] = re
    st[:, ns:2 * ns] = im

    y = jnp.concatenate(ys, axis=0) + d_ref[...] * u
    y = _gelu_tanh(y)
    gate = _sigmoid(jnp.dot(y.astype(BF16), gw_ref[...], preferred_element_type=F32) + gb_ref[...])
    out = y * gate
    ytb[0] = out[:, :LANES]
    ytb[1] = out[:, LANES:]
    for k in range(2):
        for b in range(nb):
            o_ref[b, :, k * LANES:(k + 1) * LANES] = ytb[k, pl.ds(b, ln, stride=nb), :].astype(BF16)


def _s5_mixer(su, bmat, ar, ai, cmat, dvec, glu_w, glu_b, ln):
    _, bsz, seq, _ = su.shape
    ln = min(ln, seq)
    rows = bsz * ln
    const = lambda shape: pl.BlockSpec(shape, lambda i: (0,) * len(shape))
    return pl.pallas_call(
        _s5_kernel,
        grid=(seq // ln,),
        in_specs=[
            pl.BlockSpec((2, bsz, ln, LANES), lambda i: (0, 0, i, 0)),
            const((S5_WIDTH, 2 * S5_NSTATE)),
            const((bsz, S5_NSTATE)),
            const((bsz, S5_NSTATE)),
            const((2 * S5_NSTATE, S5_WIDTH)),
            const((1, S5_WIDTH)),
            const((S5_WIDTH, S5_WIDTH)),
            const((1, S5_WIDTH)),
        ],
        out_specs=pl.BlockSpec((bsz, ln, S5_WIDTH), lambda i: (0, i, 0)),
        out_shape=jax.ShapeDtypeStruct((bsz, seq, S5_WIDTH), BF16),
        scratch_shapes=[
            pltpu.VMEM((2, rows, LANES), F32),
            pltpu.VMEM((rows, 2 * S5_NSTATE), F32),
            pltpu.VMEM((2, rows, LANES), F32),
            pltpu.VMEM((bsz, 2 * S5_NSTATE), F32),
        ],
        compiler_params=_cparams(("arbitrary",)),
        name="s5_mixer",
    )(su, bmat, ar, ai, cmat, dvec, glu_w, glu_b)


def _block_row(a, blk, r):
    t = a.shape[0]
    if blk >= 8:
        a3 = a.reshape(t // blk, blk, LANES)
        return jnp.broadcast_to(a3[:, r:r + 1, :], a3.shape).reshape(t, LANES)
    if blk == 4:
        a3 = a.reshape(t // 8, 8, LANES)
        sub = lax.broadcasted_iota(jnp.int32, a3.shape, 1)
        res = jnp.where(sub < 4, jnp.broadcast_to(a3[:, r:r + 1, :], a3.shape),
                        jnp.broadcast_to(a3[:, 4 + r:5 + r, :], a3.shape))
        return res.reshape(t, LANES)
    row = lax.broadcasted_iota(jnp.int32, a.shape, 0)
    res = a
    for q in range(blk):
        if q != r:
            res = jnp.where(row % blk == q, pltpu.roll(a, (q - r) % t, axis=0), res)
    return res


def _hgrn_kernel(hq_ref, hi_ref, hg_ref, hf_ref, lb_ref, omlb_ref, nw_ref, o_ref, lvl_sc, mask_sc, lo_sc,
                 *, tc):
    seq = hf_ref.shape[0]
    nlev = tc.bit_length() - 1
    lane = lax.broadcasted_iota(jnp.int32, (tc, LANES), 1)
    first = lane < HEAD_DIM
    row = lax.broadcasted_iota(jnp.int32, (tc, LANES), 0)
    rr = lax.broadcasted_iota(jnp.int32, (tc, tc), 0)
    cc = lax.broadcasted_iota(jnp.int32, (tc, tc), 1)
    diff = rr ^ cc
    bits = jnp.zeros((tc, tc), jnp.int32)
    for i in range(nlev):
        bits = bits + jnp.where((diff >> i) != 0, 1, 0)
    lvl_sc[...] = jnp.where(rr >= cc, bits, -1)
    for i in range(nlev):
        upper = ((row >> i) & 1) == 1
        lo_sc[i] = jnp.where(upper, 0.0, 1.0)
    mask_sc[0] = jnp.where(first, 1.0, 0.0).astype(BF16)
    mask_sc[1] = jnp.where(first, 0.0, 1.0).astype(BF16)
    sr = lax.broadcasted_iota(jnp.int32, (LANES, LANES), 0)
    sc = lax.broadcasted_iota(jnp.int32, (LANES, LANES), 1)
    blockdiag = (sr < HEAD_DIM) == (sc < HEAD_DIM)
    lb = lb_ref[...]
    omlb = omlb_ref[...]
    nw = nw_ref[...]

    def chunk(c, st):
        r0 = pl.multiple_of(c * tc, tc)
        z = hf_ref[pl.ds(r0, tc), :]
        q = _silu(hq_ref[pl.ds(r0, tc), :].astype(F32))
        v = hi_ref[pl.ds(r0, tc), :]
        g = hg_ref[pl.ds(r0, tc), :].astype(F32)
        e = jnp.exp(-jnp.abs(z))
        s_big = 1.0 / (1.0 + e)
        s_small = e * s_big
        pos = z >= 0.0
        f = lb + omlb * jnp.where(pos, s_big, s_small)
        kk = omlb * jnp.where(pos, s_small, s_big)

        head_a = mask_sc[0]
        head_b = mask_sc[1]
        q16 = q.astype(BF16)
        qf16 = (q * f).astype(BF16)
        s = lax.dot_general(jnp.concatenate([q16 * head_a, q16 * head_b, qf16 * head_a, qf16 * head_b], axis=0),
                            kk.astype(BF16), NT_DIMS, preferred_element_type=F32)
        lvl = lvl_sc[...]
        tot_a = jnp.where(lvl == 1, s[2 * tc:3 * tc], jnp.where(lvl == 0, s[0:tc], 0.0))
        tot_b = jnp.where(lvl == 1, s[3 * tc:], jnp.where(lvl == 0, s[tc:2 * tc], 0.0))
        a_m = f
        b_m = jnp.ones_like(f)
        for i in range(nlev):
            m = 1 << i
            if i > 0:
                qa = (q * a_m).astype(BF16)
                s = lax.dot_general(jnp.concatenate([qa * head_a, qa * head_b], axis=0),
                                    (kk * b_m).astype(BF16), NT_DIMS, preferred_element_type=F32)
                sel = lvl_sc[...] == i + 1
                tot_a = jnp.where(sel, s[:tc], tot_a)
                tot_b = jnp.where(sel, s[tc:], tot_b)
            lower_total = _block_row(a_m, 2 * m, m - 1)
            upper_total = _block_row(a_m, 2 * m, 2 * m - 1)
            upper = lo_sc[i] < 0.5
            a_m = jnp.where(upper, a_m * lower_total, a_m)
            b_m = jnp.where(upper, b_m, b_m * upper_total)

        o = lax.dot_general((q * a_m).astype(BF16), st.astype(BF16), NT_DIMS, preferred_element_type=F32)
        pv = jnp.dot(jnp.concatenate([tot_a, tot_b], axis=0).astype(BF16), v, preferred_element_type=F32)
        o = o + jnp.where(first, pv[:tc], pv[tc:])

        upd = jnp.dot(v.astype(F32).T.astype(BF16), (kk * b_m).astype(BF16), preferred_element_type=F32)
        st_new = st * a_m[tc - 1:tc, :] + jnp.where(blockdiag, upd, 0.0)

        o2 = o * o
        s_a = jnp.sum(jnp.where(first, o2, 0.0), axis=-1, keepdims=True)
        s_b = jnp.sum(jnp.where(first, 0.0, o2), axis=-1, keepdims=True)
        ms = jnp.where(first, s_a, s_b) * (1.0 / HEAD_DIM)
        o_ref[pl.ds(r0, tc), :] = (o * lax.rsqrt(ms + EPS) * nw * _silu(g)).astype(BF16)
        return st_new

    lax.fori_loop(0, seq // tc, chunk, jnp.zeros((LANES, LANES), F32), unroll=8)


def _hgrn_mixer(hqig, hf, lb, om_lb, nw2, tc):
    bsz, seq, _ = hf.shape
    tc = min(tc, seq)
    npair = HGRN_WIDTH // LANES
    kern = functools.partial(_hgrn_kernel, tc=tc)
    col = lambda off: pl.BlockSpec((None, seq, LANES), lambda b, p: (b, 0, off + p))
    par = pl.BlockSpec((1, LANES), lambda b, p: (0, p))
    return pl.pallas_call(
        kern,
        grid=(bsz, npair),
        in_specs=[col(0), col(npair), col(2 * npair), col(0), par, par,
                  pl.BlockSpec((1, LANES), lambda b, p: (0, 0))],
        out_specs=col(0),
        out_shape=jax.ShapeDtypeStruct((bsz, seq, HGRN_WIDTH), BF16),
        scratch_shapes=[
            pltpu.VMEM((tc, tc), jnp.int32),
            pltpu.VMEM((2, tc, LANES), BF16),
            pltpu.VMEM((tc.bit_length() - 1, tc, LANES), F32),
        ],
        compiler_params=_cparams(("arbitrary", "arbitrary")),
        name="hgrn_mixer",
    )(hqig, hqig, hqig, hf, lb, om_lb, nw2)


def _routing(logits_t):
    mx = jnp.max(logits_t, axis=0, keepdims=True)
    ex = jnp.exp(logits_t - mx)
    probs = ex / jnp.sum(ex, axis=0, keepdims=True)
    p = [probs[e:e + 1, :] for e in range(N_EXPERTS)]
    epg = N_EXPERTS // N_GROUPS
    scores = []
    for g in range(N_GROUPS):
        a, b, c, d = p[epg * g:epg * g + epg]
        hi1, lo1 = jnp.maximum(a, b), jnp.minimum(a, b)
        hi2, lo2 = jnp.maximum(c, d), jnp.minimum(c, d)
        top1 = jnp.maximum(hi1, hi2)
        top2 = jnp.maximum(jnp.minimum(hi1, hi2), jnp.maximum(lo1, lo2))
        scores.append(top1 + top2)
    best = jnp.zeros_like(scores[0], dtype=jnp.int32)
    bs = scores[0]
    for g in range(1, N_GROUPS):
        upd = scores[g] > bs
        best = jnp.where(upd, g, best)
        bs = jnp.where(upd, scores[g], bs)
    ig = []
    for j in range(epg):
        val = p[j]
        for g in range(1, N_GROUPS):
            val = jnp.where(best == g, p[epg * g + j], val)
        ig.append(val)
    i1 = jnp.zeros_like(best)
    w1 = ig[0]
    for j in range(1, epg):
        upd = ig[j] > w1
        i1 = jnp.where(upd, j, i1)
        w1 = jnp.where(upd, ig[j], w1)
    i2 = jnp.zeros_like(best)
    w2 = jnp.full_like(w1, -1.0)
    for j in range(epg):
        upd = jnp.logical_and(i1 != j, ig[j] > w2)
        i2 = jnp.where(upd, j, i2)
        w2 = jnp.where(upd, ig[j], w2)
    den = w1 + w2
    tw1 = w1 / den
    tw2 = w2 / den
    e1 = best * epg + i1
    e2 = best * epg + i2
    return [jnp.where(e1 == e, tw1, 0.0) + jnp.where(e2 == e, tw2, 0.0) for e in range(N_EXPERTS)], best


def _moe_kernel(fox_ref, s5_ref, hg_ref, x_ref, mod_ref, nf_ref, ns_ref, wo_ref,
                nw_ref, rw_ref, rb_ref, wg_ref, wu_ref, wd_ref, fnw_ref,
                o_ref, h_sc, comb_sc, combt_sc, hs_sc, ws_sc, ys_sc, acc_sc, cnt_sm, *, final_norm):
    grp = pl.program_id(2)
    tm, d = x_ref.shape
    epg = N_EXPERTS // N_GROUPS
    sub, cap = MOE_SUB, MOE_CAP
    ns = tm // sub
    gc = N_GROUPS * cap
    row_oh, row_slot = N_EXPERTS, N_EXPERTS + 8

    @pl.when(grp == 0)
    def _():
        for s in range(ns):
            toks = slice(s * sub, (s + 1) * sub)
            of = (_rms(fox_ref[toks, :].astype(F32)) * nf_ref[...]).astype(BF16)
            os5 = (_rms(s5_ref[toks, :].astype(F32)) * ns_ref[...]).astype(BF16)
            mixed = jnp.concatenate([of, os5, hg_ref[toks, :]], axis=1)
            mix = jnp.dot(mixed, wo_ref[...], preferred_element_type=F32)
            o_ref[toks, :] = x_ref[toks, :] + mod_ref[2:3, :] * mix
        h = _rms(o_ref[...]) * nw_ref[...]
        h = h * (1.0 + mod_ref[4:5, :]) + mod_ref[3:4, :]
        hb = h.astype(BF16)
        h_sc[...] = hb
        h_lo = (h - hb.astype(F32)).astype(BF16)
        hi_part = jnp.dot(hb, rw_ref[...], preferred_element_type=F32)
        lo_part = jnp.dot(h_lo, rw_ref[:, 0:LANES], preferred_element_type=F32)
        logits = hi_part[:, 0:LANES] + hi_part[:, LANES:] + lo_part + rb_ref[...]
        rows, best = _routing(logits.T[0:N_EXPERTS, :])
        combt_sc[...] = jnp.zeros_like(combt_sc)
        for j in range(N_EXPERTS):
            combt_sc[j:j + 1, :] = rows[j]
        for g in range(N_GROUPS):
            combt_sc[row_oh + g:row_oh + g + 1, :] = jnp.where(best == g, 1.0, 0.0)
        oh8 = combt_sc[row_oh:row_oh + 8, :]
        r = lax.broadcasted_iota(jnp.int32, (sub, sub), 0)
        c = lax.broadcasted_iota(jnp.int32, (sub, sub), 1)
        tri_u = jnp.where(r <= c, 1.0, 0.0).astype(BF16)
        incl = [jnp.dot(oh8[:, s * sub:(s + 1) * sub].astype(BF16), tri_u, preferred_element_type=F32)
                for s in range(ns)]
        cmax = incl[0][:, sub - 1:sub]
        for s in range(1, ns):
            cmax = jnp.maximum(cmax, incl[s][:, sub - 1:sub])
        for g in range(N_GROUPS):
            cnt_sm[g] = jnp.max(cmax[g:g + 1, :]).astype(jnp.int32)
        over = jnp.where(cmax > cap, 1.0, 0.0)
        best_f = best.astype(F32)
        for s in range(ns):
            cols = slice(s * sub, (s + 1) * sub)
            oh_s = oh8[:, cols]
            rank = jnp.sum(oh_s * incl[s], axis=0, keepdims=True) - 1.0
            dense = jnp.sum(oh_s * over, axis=0, keepdims=True)
            combt_sc[row_slot:row_slot + 1, cols] = jnp.where(dense > 0.5, -1.0, best_f[:, cols] * cap + rank)
        comb_sc[...] = combt_sc[...].T
        for s in range(ns):
            toks = slice(s * sub, (s + 1) * sub)
            slot_row = combt_sc[row_slot:row_slot + 1, toks].astype(jnp.int32)
            ri = lax.broadcasted_iota(jnp.int32, (gc, sub), 0)
            p = jnp.where(ri == slot_row, 1.0, 0.0).astype(BF16)
            hs_all = jnp.dot(p, h_sc[toks, :], preferred_element_type=F32).astype(BF16)
            cs = comb_sc[toks, :]
            c_hi = cs.astype(BF16)
            c_lo = (cs - c_hi.astype(F32)).astype(BF16)
            ws_all = jnp.dot(p, jnp.concatenate([c_hi, c_lo], axis=1), preferred_element_type=F32)
            ws_all = ws_all[:, 0:LANES] + ws_all[:, LANES:]
            for g in range(N_GROUPS):
                hs_sc[g, s * cap:(s + 1) * cap, :] = hs_all[g * cap:(g + 1) * cap, :]
                ws_sc[g, s * cap:(s + 1) * cap, :] = ws_all[g * cap:(g + 1) * cap, :]
        acc_sc[...] = jnp.zeros_like(acc_sc)

    def experts(h_rows, w_rows):
        lane = lax.broadcasted_iota(jnp.int32, w_rows.shape, 1)
        hid = []
        for j in range(epg):
            gate = jnp.dot(h_rows, wg_ref[j], preferred_element_type=F32)
            up = jnp.dot(h_rows, wu_ref[j], preferred_element_type=F32)
            w_e = jnp.sum(jnp.where(lane == grp * epg + j, w_rows, 0.0), axis=-1, keepdims=True)
            hid.append((_silu(gate) * up * w_e).astype(BF16))
        wd = wd_ref[...].reshape(epg * D_FF, d)
        return jnp.dot(jnp.concatenate(hid, axis=1), wd, preferred_element_type=F32)

    cnt = cnt_sm[grp]

    @pl.when(cnt <= cap)
    def _():
        ys = experts(hs_sc[grp], ws_sc[grp]).astype(BF16)
        for s in range(ns):
            ys_sc[s, grp] = ys[s * cap:(s + 1) * cap, :]

    @pl.when(cnt > cap)
    def _():
        acc_sc[...] += experts(h_sc[...], comb_sc[...])
        for s in range(ns):
            ys_sc[s, grp] = jnp.zeros((cap, d), BF16)

    @pl.when(grp == N_GROUPS - 1)
    def _():
        for s in range(ns):
            toks = slice(s * sub, (s + 1) * sub)
            slot_col = comb_sc[toks, row_slot:row_slot + 1].astype(jnp.int32)
            li = lax.broadcasted_iota(jnp.int32, (sub, gc), 1)
            pt = jnp.where(li == slot_col, 1.0, 0.0).astype(BF16)
            y = jnp.dot(pt, ys_sc[s].reshape(gc, d), preferred_element_type=F32) + acc_sc[toks, :]
            y = o_ref[toks, :] + mod_ref[5:6, :] * y
            if final_norm:
                y = _rms(y) * fnw_ref[...]
            o_ref[toks, :] = y


def _outproj_moe(o_fox, o_s5, o_hgrn, x, mod_l, nf, ns_w, w_out, norm_w, rw_pad, rb_pad, wg, wu, wd, fnw,
                 tm, final_norm):
    bsz, seq, d = x.shape
    tm = min(tm, seq)
    assert tm % MOE_SUB == 0
    ns = tm // MOE_SUB
    epg = N_EXPERTS // N_GROUPS
    kern = functools.partial(_moe_kernel, final_norm=final_norm)
    row = lambda w: pl.BlockSpec((None, tm, w), lambda b, i, e: (b, i, 0))
    return pl.pallas_call(
        kern,
        grid=(bsz, seq // tm, N_GROUPS),
        in_specs=[
            row(FOX_WIDTH), row(S5_WIDTH), row(HGRN_WIDTH), row(d),
            pl.BlockSpec((None, 6, d), lambda b, i, e: (b, 0, 0)),
            pl.BlockSpec((1, FOX_WIDTH), lambda b, i, e: (0, 0)),
            pl.BlockSpec((1, S5_WIDTH), lambda b, i, e: (0, 0)),
            pl.BlockSpec((d, d), lambda b, i, e: (0, 0)),
            pl.BlockSpec((1, d), lambda b, i, e: (0, 0)),
            pl.BlockSpec((d, 2 * LANES), lambda b, i, e: (0, 0)),
            pl.BlockSpec((1, LANES), lambda b, i, e: (0, 0)),
            pl.BlockSpec((epg, d, D_FF), lambda b, i, e: (e, 0, 0)),
            pl.BlockSpec((epg, d, D_FF), lambda b, i, e: (e, 0, 0)),
            pl.BlockSpec((epg, D_FF, d), lambda b, i, e: (e, 0, 0)),
            pl.BlockSpec((1, d), lambda b, i, e: (0, 0)),
        ],
        out_specs=pl.BlockSpec((None, tm, d), lambda b, i, e: (b, i, 0)),
        out_shape=jax.ShapeDtypeStruct((bsz, seq, d), F32),
        scratch_shapes=[
            pltpu.VMEM((tm, d), BF16),
            pltpu.VMEM((tm, LANES), F32),
            pltpu.VMEM((LANES, tm), F32),
            pltpu.VMEM((N_GROUPS, ns * MOE_CAP, d), BF16),
            pltpu.VMEM((N_GROUPS, ns * MOE_CAP, LANES), F32),
            pltpu.VMEM((ns, N_GROUPS, MOE_CAP, d), BF16),
            pltpu.VMEM((tm, d), F32),
            pltpu.SMEM((N_GROUPS,), jnp.int32),
        ],
        compiler_params=_cparams(("arbitrary", "arbitrary", "arbitrary")),
        name="moe",
    )(o_fox, o_s5, o_hgrn, x, mod_l, nf, ns_w, w_out, norm_w, rw_pad, rb_pad, wg, wu, wd, fnw)


def _prep_w_in(w_in_l, f_bias_l):
    a = FOX_WIDTH
    o_ff = 3 * a
    o_su = o_ff + FOX_HEADS
    o_hq = o_su + S5_WIDTH
    hw = HGRN_WIDTH
    fq = w_in_l[:, 0:a] * (HEAD_DIM ** -0.5 * LOG2E)
    fk, fv = w_in_l[:, a:2 * a], w_in_l[:, 2 * a:3 * a]
    ff = jnp.pad(w_in_l[:, o_ff:o_su], ((0, 0), (0, LANES - FOX_HEADS)))
    su = w_in_l[:, o_su:o_hq]
    hq = w_in_l[:, o_hq:o_hq + hw]
    hf = w_in_l[:, o_hq + hw:o_hq + 2 * hw]
    hi = w_in_l[:, o_hq + 2 * hw:o_hq + 3 * hw]
    hg = w_in_l[:, o_hq + 3 * hw:o_hq + 4 * hw]
    w = jnp.concatenate([fq, fk, fv, su, hq, hi, hg, hf, ff], axis=1).astype(BF16)
    fb = jnp.pad(f_bias_l.astype(F32), (0, LANES - FOX_HEADS)).reshape(1, LANES)
    return w, fb


def _prep_s5(a_re, a_im, b_re, b_im, c_re, c_im, log_dt, bsz):
    lam = lax.complex(a_re.astype(F32), a_im.astype(F32))
    dt = jnp.exp(log_dt.astype(F32))[:, None]
    a_bar = jnp.exp(lam * dt)
    b_bar = ((a_bar - 1.0) / lam)[..., None] * lax.complex(b_re.astype(F32), b_im.astype(F32))
    eye = jnp.eye(S5_GROUPS, dtype=F32)
    bm_re = jnp.einsum('gpc,gh->gchp', jnp.real(b_bar), eye).reshape(S5_WIDTH, S5_NSTATE)
    bm_im = jnp.einsum('gpc,gh->gchp', jnp.imag(b_bar), eye).reshape(S5_WIDTH, S5_NSTATE)
    bmat = jnp.concatenate([bm_re, bm_im], axis=1).astype(BF16)
    cm_re = jnp.einsum('gcp,gh->gphc', c_re.astype(F32), eye).reshape(S5_NSTATE, S5_WIDTH)
    cm_im = jnp.einsum('gcp,gh->gphc', c_im.astype(F32), eye).reshape(S5_NSTATE, S5_WIDTH)
    cmat = jnp.concatenate([cm_re, -cm_im], axis=0).astype(BF16)
    ar = jnp.broadcast_to(jnp.real(a_bar).reshape(1, S5_NSTATE), (bsz, S5_NSTATE))
    ai = jnp.broadcast_to(jnp.imag(a_bar).reshape(1, S5_NSTATE), (bsz, S5_NSTATE))
    return bmat, ar, ai, cmat


def kernel(x, c, ada_w, ada_b, norm_mix_w, norm_ffn_w, w_in, fox_f_bias, s5_a_re, s5_a_im, s5_b_re, s5_b_im, s5_c_re, s5_c_im, s5_d, s5_log_dt, s5_glu_w, s5_glu_b, hgrn_lb_logits, hgrn_norm_w, branch_norm_fox, branch_norm_s5, w_out, router_w, router_b, moe_w_gate, moe_w_up, moe_w_down, final_norm_w):
    bsz, seq, d = x.shape
    depth = w_in.shape[0]
    lb_cum = jnp.cumsum(jax.nn.softmax(hgrn_lb_logits.astype(F32), axis=0), axis=0)
    lower = lb_cum - lb_cum[0:1]
    om_lb = 1.0 - lower

    mod = _ada_mod(c, ada_w, ada_b)
    rw32 = jnp.pad(router_w.astype(F32), ((0, 0), (0, LANES - N_EXPERTS)))
    rw_hi = rw32.astype(BF16)
    rw_pad = jnp.concatenate([rw_hi, (rw32 - rw_hi.astype(F32)).astype(BF16)], axis=1)
    rb_pad = jnp.pad(router_b.astype(F32), (0, LANES - N_EXPERTS)).reshape(1, LANES)
    fnw = final_norm_w.reshape(1, d).astype(F32)

    w_pad, fb_pad = jax.vmap(_prep_w_in)(w_in, fox_f_bias)
    bmat, ar, ai, cmat = jax.vmap(functools.partial(_prep_s5, bsz=bsz))(
        s5_a_re, s5_a_im, s5_b_re, s5_b_im, s5_c_re, s5_c_im, s5_log_dt)
    glu_w = s5_glu_w.astype(BF16)
    wo = w_out.astype(BF16)
    wg, wu, wd = moe_w_gate.astype(BF16), moe_w_up.astype(BF16), moe_w_down.astype(BF16)

    for l in range(depth):
        qkv, su, hqig, hf, cum_t = _in_proj(x, mod[l], norm_mix_w[l].reshape(1, d), w_pad[l], fb_pad[l], tm=1024)
        o_fox = _fox_attention(qkv, cum_t, tq=256, tk=512)
        o_s5 = _s5_mixer(su, bmat[l], ar[l], ai[l], cmat[l], s5_d[l].reshape(1, S5_WIDTH).astype(F32),
                         glu_w[l], s5_glu_b[l].reshape(1, S5_WIDTH).astype(F32), ln=128)
        nw2 = jnp.tile(hgrn_norm_w[l].astype(F32), 2).reshape(1, LANES)
        o_hgrn = _hgrn_mixer(hqig, hf, lower[l].reshape(1, -1), om_lb[l].reshape(1, -1), nw2, tc=128)
        x = _outproj_moe(o_fox, o_s5, o_hgrn, x, mod[l],
                         branch_norm_fox[l].reshape(1, -1).astype(F32),
                         branch_norm_s5[l].reshape(1, -1).astype(F32),
                         wo[l],
                         norm_ffn_w[l].reshape(1, d).astype(F32), rw_pad, rb_pad,
                         wg[l], wu[l], wd[l],
                         fnw, tm=1024, final_norm=(l == depth - 1))
    return x
```
